```python
import math
import jax, jax.numpy as jnp
from jax import lax
import numpy as np

D_MODEL = 1024
BATCH = 4
SEQ = 4096
DEPTH = 4

MIX_WIDTH = D_MODEL
HEAD_DIM = 64
A_WIDTH = D_MODEL // 4
B_WIDTH = D_MODEL // 2
C_WIDTH = D_MODEL // 4
A_HEADS = A_WIDTH // HEAD_DIM
B_HEADS = B_WIDTH // HEAD_DIM
C_GROUPS = C_WIDTH // HEAD_DIM
IN_COLS = 2 * A_WIDTH + 3 * B_WIDTH + 2 * C_WIDTH
CHUNK = 128
MOBA_BLOCK = 256
MOBA_TOPK = 3
Q_BLK = 64
CONV_WIDTH = 31
MEM_LEN = 256
X_HEADS = 4
X_HEAD_DIM = D_MODEL // X_HEADS
D_FF = 4 * D_MODEL
EPS = 1e-6

kernel_name = "hymba_style_gmlp_moba_conformer_hybrid"


def rms_norm(x, g):
    xf = x.astype(jnp.float32)
    y = xf * lax.rsqrt(jnp.mean(xf * xf, axis=-1, keepdims=True) + EPS)
    return (y * g.astype(jnp.float32)).astype(x.dtype)


def layer_norm(x, g, b):
    xf = x.astype(jnp.float32)
    mu = jnp.mean(xf, axis=-1, keepdims=True)
    xc = xf - mu
    var = jnp.mean(xc * xc, axis=-1, keepdims=True)
    y = xc * lax.rsqrt(var + EPS) * g.astype(jnp.float32) + b.astype(jnp.float32)
    return y.astype(x.dtype)


def alibi_slopes(n_heads):
    return 2.0 ** (-8.0 * jnp.arange(1, n_heads + 1, dtype=jnp.float32) / n_heads)


def spatial_gating(z, ln_g, ln_b, w_s, b_s):
    Bn, S, _ = z.shape
    u, v = jnp.split(z, 2, axis=-1)
    v = layer_norm(v, ln_g, ln_b)
    v = v.reshape(Bn, S // CHUNK, CHUNK, A_HEADS, HEAD_DIM)
    causal = jnp.tril(jnp.ones((CHUNK, CHUNK), dtype=w_s.dtype))
    w = (w_s * causal[None]).astype(v.dtype)
    mixed = jnp.einsum('hts,bcshd->bcthd', w, v) + b_s.T.astype(v.dtype)[None, None, :, :, None]
    return u * mixed.reshape(Bn, S, A_WIDTH)


def moba_attention(q, k, v):
    Bn, H, S, Dh = q.shape
    s_pad = -(-S // MOBA_BLOCK) * MOBA_BLOCK
    pad = ((0, 0), (0, 0), (0, s_pad - S), (0, 0))
    q, k, v = jnp.pad(q, pad), jnp.pad(k, pad), jnp.pad(v, pad)
    nb = s_pad // MOBA_BLOCK
    kk = min(MOBA_TOPK, nb)
    n_q = s_pad // Q_BLK
    scale = Dh ** -0.5
    slopes = alibi_slopes(H)
    kb = k.reshape(Bn, H, nb, MOBA_BLOCK, Dh)
    vb = v.reshape(Bn, H, nb, MOBA_BLOCK, Dh)
    kmean = jnp.mean(kb.astype(jnp.float32), axis=3)
    qc = q.reshape(Bn, H, n_q, Q_BLK, Dh).transpose(2, 0, 1, 3, 4)
    bi = jnp.arange(Bn)[:, None, None, None]
    hi = jnp.arange(H)[None, :, None, None]
    offs = jnp.arange(MOBA_BLOCK)

    def one_block(args):
        qi, ci = args
        t = ci * Q_BLK + jnp.arange(Q_BLK)
        own = (ci * Q_BLK) // MOBA_BLOCK
        gate = jnp.einsum('bhqd,bhnd->bhqn', qi.astype(jnp.float32), kmean)
        gate = jnp.where(jnp.arange(nb) < own, gate, -jnp.inf)
        _, idx = lax.top_k(gate, kk)
        valid = idx < own
        ksel = kb[bi, hi, idx]
        vsel = vb[bi, hi, idx]
        s_past = jnp.einsum('bhqd,bhqkld->bhqkl', qi, ksel).astype(jnp.float32) * scale
        kpos = idx[..., None] * MOBA_BLOCK + offs
        dist = (t[None, None, :, None, None] - kpos).astype(jnp.float32)
        s_past = jnp.where(valid[..., None], s_past - slopes[None, :, None, None, None] * dist, -jnp.inf)
        k_own = lax.dynamic_index_in_dim(kb, own, axis=2, keepdims=False)
        v_own = lax.dynamic_index_in_dim(vb, own, axis=2, keepdims=False)
        s_own = jnp.einsum('bhqd,bhld->bhql', qi, k_own).astype(jnp.float32) * scale
        kpos_own = own * MOBA_BLOCK + offs
        dist_own = (t[:, None] - kpos_own[None, :]).astype(jnp.float32)
        s_own = jnp.where((dist_own >= 0)[None, None],
                          s_own - slopes[None, :, None, None] * dist_own[None, None], -jnp.inf)
        s_all = jnp.concatenate([s_past.reshape(Bn, H, Q_BLK, kk * MOBA_BLOCK), s_own], axis=-1)
        p = jax.nn.softmax(s_all, axis=-1).astype(v.dtype)
        p_past = p[..., :kk * MOBA_BLOCK].reshape(Bn, H, Q_BLK, kk, MOBA_BLOCK)
        p_own = p[..., kk * MOBA_BLOCK:]
        return (jnp.einsum('bhqkl,bhqkld->bhqd', p_past, vsel)
                + jnp.einsum('bhql,bhld->bhqd', p_own, v_own))

    out = lax.map(one_block, (qc, jnp.arange(n_q, dtype=jnp.int32)))
    out = out.transpose(1, 0, 3, 2, 4).reshape(Bn, s_pad, H * Dh)
    return out[:, :S]


def conformer_conv(z, w_dw, b_dw, gn_g, gn_b):
    Bn, S, _ = z.shape
    a, g = jnp.split(z, 2, axis=-1)
    y = a * jax.nn.sigmoid(g)
    rhs = w_dw.astype(y.dtype)[:, None, :]
    y = lax.conv_general_dilated(y, rhs, window_strides=(1,), padding=[(CONV_WIDTH - 1, 0)],
                                 dimension_numbers=('NWC', 'WIO', 'NWC'),
                                 feature_group_count=C_WIDTH)
    y = y + b_dw.astype(y.dtype)
    y = y.reshape(Bn, S, C_GROUPS, HEAD_DIM)
    y = layer_norm(y, gn_g.reshape(C_GROUPS, HEAD_DIM), gn_b.reshape(C_GROUPS, HEAD_DIM))
    return jax.nn.silu(y.reshape(Bn, S, C_WIDTH))


def memory_cross_attention(h, m, w_q, w_kv, w_o):
    Bn, S, _ = h.shape
    M = m.shape[1]
    q = (h @ w_q).reshape(Bn, S, X_HEADS, X_HEAD_DIM)
    k, v = jnp.split(m @ w_kv, 2, axis=-1)
    k = k.reshape(Bn, M, X_HEADS, X_HEAD_DIM)
    v = v.reshape(Bn, M, X_HEADS, X_HEAD_DIM)
    s = jnp.einsum('bshd,bmhd->bhsm', q, k).astype(jnp.float32) * (X_HEAD_DIM ** -0.5)
    p = jax.nn.softmax(s, axis=-1).astype(v.dtype)
    o = jnp.einsum('bhsm,bmhd->bshd', p, v).reshape(Bn, S, D_MODEL)
    return o @ w_o


def setup_inputs(seed: int = 0) -> dict:
    key = jax.random.key(seed)
    ks = jax.random.split(key, 24)
    f32 = jnp.float32
    L = DEPTH

    def nrm(k, shape, scale):
        return jax.random.normal(k, shape, f32) * scale

    def gain(k, shape):
        return 1.0 + 0.05 * jax.random.normal(k, shape, f32)

    return {
        "x": nrm(ks[0], (BATCH, SEQ, D_MODEL), 1.0),
        "mem": nrm(ks[1], (BATCH, MEM_LEN, D_MODEL), 1.0),
        "pre_mix_g": gain(ks[2], (L, D_MODEL)),
        "w_in": nrm(ks[3], (L, D_MODEL, IN_COLS), D_MODEL ** -0.5),
        "gate_ln_g": gain(ks[4], (L, A_WIDTH)),
        "gate_ln_b": nrm(ks[5], (L, A_WIDTH), 0.02),
        "w_s": nrm(ks[6], (L, A_HEADS, CHUNK, CHUNK), CHUNK ** -0.5),
        "b_s": nrm(ks[7], (L, A_HEADS, CHUNK), 0.02),
        "w_dw": nrm(ks[8], (L, CONV_WIDTH, C_WIDTH), CONV_WIDTH ** -0.5),
        "b_dw": nrm(ks[9], (L, C_WIDTH), 0.02),
        "conv_gn_g": gain(ks[10], (L, C_WIDTH)),
        "conv_gn_b": nrm(ks[11], (L, C_WIDTH), 0.02),
        "w_out": nrm(ks[12], (L, MIX_WIDTH, D_MODEL), MIX_WIDTH ** -0.5),
        "post_mix_g": gain(ks[13], (L, D_MODEL)),
        "pre_x_g": gain(ks[14], (L, D_MODEL)),
        "mem_g": gain(ks[15], (L, D_MODEL)),
        "w_xq": nrm(ks[16], (L, D_MODEL, D_MODEL), D_MODEL ** -0.5),
        "w_xkv": nrm(ks[17], (L, D_MODEL, 2 * D_MODEL), D_MODEL ** -0.5),
        "w_xo": nrm(ks[18], (L, D_MODEL, D_MODEL), D_MODEL ** -0.5),
        "post_x_g": gain(ks[19], (L, D_MODEL)),
        "pre_ffn_g": gain(ks[20], (L, D_MODEL)),
        "w_ff1": nrm(ks[21], (L, D_MODEL, D_FF), D_MODEL ** -0.5),
        "w_ff2": nrm(ks[22], (L, D_FF, D_MODEL), D_FF ** -0.5),
        "post_ffn_g": gain(ks[23], (L, D_MODEL)),
    }


def reference(x, mem, pre_mix_g, w_in, gate_ln_g, gate_ln_b, w_s, b_s, w_dw, b_dw,
              conv_gn_g, conv_gn_b, w_out, post_mix_g, pre_x_g, mem_g, w_xq, w_xkv,
              w_xo, post_x_g, pre_ffn_g, w_ff1, w_ff2, post_ffn_g):
    Bn, S, _ = x.shape
    split_a = 2 * A_WIDTH
    split_b = split_a + 3 * B_WIDTH
    for l in range(DEPTH):
        h = rms_norm(x, pre_mix_g[l])
        z = h @ w_in[l]
        za, zb, zc = z[..., :split_a], z[..., split_a:split_b], z[..., split_b:]
        ya = spatial_gating(jax.nn.gelu(za, approximate=False),
                            gate_ln_g[l], gate_ln_b[l], w_s[l], b_s[l])
        q, k, v = jnp.split(zb, 3, axis=-1)
        q, k, v = (a.reshape(Bn, S, B_HEADS, HEAD_DIM).transpose(0, 2, 1, 3) for a in (q, k, v))
        yb = moba_attention(q, k, v)
        yc = conformer_conv(zc, w_dw[l], b_dw[l], conv_gn_g[l], conv_gn_b[l])
        y = jnp.concatenate([ya, yb, yc], axis=-1) @ w_out[l]
        x = x + rms_norm(y, post_mix_g[l])
        h = rms_norm(x, pre_x_g[l])
        m = rms_norm(mem, mem_g[l])
        x = x + rms_norm(memory_cross_attention(h, m, w_xq[l], w_xkv[l], w_xo[l]), post_x_g[l])
        h = rms_norm(x, pre_ffn_g[l])
        f = jnp.square(jax.nn.relu(h @ w_ff1[l])) @ w_ff2[l]
        x = x + rms_norm(f, post_ffn_g[l])
    return x
```

```python
import functools
import math

import jax
import jax.numpy as jnp
from jax import lax
from jax.experimental import pallas as pl
from jax.experimental.pallas import tpu as pltpu

HEAD_DIM = 64
CHUNK = 128
MOBA_BLOCK = 256
MOBA_TOPK = 3
CONV_WIDTH = 31
X_HEADS = 4
EPS = 1e-6

LANES = 128
HALO = 32
CONV_ROWS = 64
TOKEN_TILE = 512
VMEM_LIMIT = 56 * 1024 * 1024

NEG_INF = float("-inf")


def _rms(x, g):
    return x * lax.rsqrt(jnp.mean(x * x, axis=-1, keepdims=True) + EPS) * g


def _const_spec(shape, index_map):
    return pl.BlockSpec(shape, index_map, pipeline_mode=pl.Buffered(1))


def _mem_kv_kernel(mem_ref, g_ref, w_ref, k_ref, v_ref):
    d = mem_ref.shape[-1]
    m = _rms(mem_ref[...], g_ref[...]).astype(jnp.bfloat16)
    kv = jnp.dot(m, w_ref[...], preferred_element_type=jnp.float32)
    k_ref[...] = kv[:, :d].astype(jnp.bfloat16)
    v_ref[...] = kv[:, d:].astype(jnp.bfloat16)


def _mem_kv(mem, mem_g, w_xkv):
    bn, m, d = mem.shape
    depth = w_xkv.shape[0]
    out = jax.ShapeDtypeStruct((depth, bn, m, d), jnp.bfloat16)
    return pl.pallas_call(
        _mem_kv_kernel,
        grid=(depth, bn),
        in_specs=[
            pl.BlockSpec((None, m, d), lambda l, b: (b, 0, 0)),
            pl.BlockSpec((None, 1, d), lambda l, b: (l, 0, 0)),
            pl.BlockSpec((None, d, 2 * d), lambda l, b: (l, 0, 0)),
        ],
        out_specs=[
            pl.BlockSpec((None, None, m, d), lambda l, b: (l, b, 0, 0)),
            pl.BlockSpec((None, None, m, d), lambda l, b: (l, b, 0, 0)),
        ],
        out_shape=[out, out],
        compiler_params=pltpu.CompilerParams(
            dimension_semantics=("arbitrary", "arbitrary"), vmem_limit_bytes=VMEM_LIMIT),
        name="mem_kv",
    )(mem, mem_g, w_xkv)


def _mix_in_kernel(x_ref, g_ref, win_ref, lng_ref, lnb_ref, ws_ref, bs_ref, wdw_ref, bdw_ref,
                   gng_ref, gnb_ref, ya_ref, q_ref, k_ref, v_ref, yc_ref, hist_ref,
                   *, a_width, b_width, c_width):
    tm = x_ref.shape[0]
    h = _rms(x_ref[...], g_ref[...]).astype(jnp.bfloat16)
    z = jnp.dot(h, win_ref[...], preferred_element_type=jnp.float32)

    o = 2 * a_width
    q_ref[...] = (z[:, o:o + b_width] * (HEAD_DIM ** -0.5)).astype(jnp.bfloat16)
    k_ref[...] = z[:, o + b_width:o + 2 * b_width].astype(jnp.bfloat16)
    v_ref[...] = z[:, o + 2 * b_width:o + 3 * b_width].astype(jnp.bfloat16)

    za = z[:, :2 * a_width]
    ga = 0.5 * za * (1.0 + lax.erf(za * (2.0 ** -0.5)))
    u = ga[:, :a_width]
    vv = ga[:, a_width:]
    mu = jnp.mean(vv, axis=-1, keepdims=True)
    vc = vv - mu
    var = jnp.mean(vc * vc, axis=-1, keepdims=True)
    vn = (vc * lax.rsqrt(var + EPS) * lng_ref[...] + lnb_ref[...]).astype(jnp.bfloat16)
    n_heads = a_width // HEAD_DIM
    t_idx = lax.broadcasted_iota(jnp.int32, (CHUNK, n_heads * CHUNK), 0)
    s_idx = lax.broadcasted_iota(jnp.int32, (CHUNK, n_heads * CHUNK), 1) % CHUNK
    w_tril = jnp.where(s_idx <= t_idx, ws_ref[...], 0.0).astype(jnp.bfloat16)
    r_head = lax.broadcasted_iota(jnp.int32, (n_heads * CHUNK, a_width), 0) // CHUNK
    c_head = lax.broadcasted_iota(jnp.int32, (n_heads * CHUNK, a_width), 1) // HEAD_DIM
    head_match = r_head == c_head
    for c in range(tm // CHUNK):
        rows = slice(c * CHUNK, (c + 1) * CHUNK)
        v_chunk = vn[rows]
        v_bd = jnp.where(head_match, jnp.concatenate([v_chunk] * n_heads, axis=0), 0.0)
        mixed = jnp.dot(w_tril, v_bd.astype(jnp.bfloat16), preferred_element_type=jnp.float32)
        ya_ref[rows, :] = (u[rows] * (mixed + bs_ref[...])).astype(jnp.bfloat16)

    o = 2 * a_width + 3 * b_width
    zc = z[:, o:]
    glu = zc[:, :c_width] * (1.0 / (1.0 + jnp.exp(-zc[:, c_width:])))

    @pl.when(pl.program_id(1) == 0)
    def _():
        hist_ref[0:HALO, :] = jnp.zeros((HALO, c_width), jnp.float32)

    hist_ref[HALO:HALO + tm, :] = glu
    gr = lax.broadcasted_iota(jnp.int32, (c_width, c_width), 0) // HEAD_DIM
    gc = lax.broadcasted_iota(jnp.int32, (c_width, c_width), 1) // HEAD_DIM
    gmat = jnp.where(gr == gc, 1.0 / HEAD_DIM, 0.0).astype(jnp.float32)
    first = HALO - (CONV_WIDTH - 1)
    for c in range(tm // CONV_ROWS):
        r0 = c * CONV_ROWS
        acc = jnp.zeros((CONV_ROWS, c_width), jnp.float32)
        for j in range(CONV_WIDTH):
            acc = acc + wdw_ref[j:j + 1, :] * hist_ref[r0 + first + j:r0 + first + j + CONV_ROWS, :]
        y = acc + bdw_ref[...]
        gmu = jnp.dot(y, gmat, preferred_element_type=jnp.float32, precision=lax.Precision.HIGHEST)
        yc_ = y - gmu
        gvar = jnp.dot(yc_ * yc_, gmat, preferred_element_type=jnp.float32,
                       precision=lax.Precision.HIGHEST)
        yn = yc_ * lax.rsqrt(gvar + EPS) * gng_ref[...] + gnb_ref[...]
        yc_ref[r0:r0 + CONV_ROWS, :] = (yn * (1.0 / (1.0 + jnp.exp(-yn)))).astype(jnp.bfloat16)
    hist_ref[0:HALO, :] = hist_ref[tm:tm + HALO, :]


def _mix_in(x, l, pre_g, w_in, ln_g, ln_b, ws_cat, bs_x, w_dw, b_dw, gn_g, gn_b,
            *, a_width, b_width, c_width):
    bn, s, d = x.shape
    tm = min(TOKEN_TILE, s)
    in_cols = w_in.shape[-1]
    n_heads = a_width // HEAD_DIM
    lsel3 = lambda b, i: (l, 0, 0)
    tok = lambda w: pl.BlockSpec((None, tm, w), lambda b, i: (b, i, 0))
    bf = lambda w: jax.ShapeDtypeStruct((bn, s, w), jnp.bfloat16)
    return pl.pallas_call(
        functools.partial(_mix_in_kernel, a_width=a_width, b_width=b_width, c_width=c_width),
        grid=(bn, s // tm),
        in_specs=[
            tok(d),
            pl.BlockSpec((None, 1, d), lsel3),
            _const_spec((None, d, in_cols), lsel3),
            pl.BlockSpec((None, 1, a_width), lsel3),
            pl.BlockSpec((None, 1, a_width), lsel3),
            pl.BlockSpec((None, CHUNK, n_heads * CHUNK), lsel3),
            pl.BlockSpec((None, CHUNK, a_width), lsel3),
            pl.BlockSpec((None, CONV_WIDTH, c_width), lsel3),
            pl.BlockSpec((None, 1, c_width), lsel3),
            pl.BlockSpec((None, 1, c_width), lsel3),
            pl.BlockSpec((None, 1, c_width), lsel3),
        ],
        out_specs=[tok(a_width), tok(b_width), tok(b_width), tok(b_width), tok(c_width)],
        out_shape=[bf(a_width), bf(b_width), bf(b_width), bf(b_width), bf(c_width)],
        scratch_shapes=[pltpu.VMEM((tm + HALO, c_width), jnp.float32)],
        compiler_params=pltpu.CompilerParams(
            dimension_semantics=("arbitrary", "arbitrary"), vmem_limit_bytes=VMEM_LIMIT),
        name="mix_in",
    )(x, pre_g, w_in, ln_g, ln_b, ws_cat, bs_x, w_dw, b_dw, gn_g, gn_b)


def _moba_kernel(slopes_ref, q_ref, k_ref, v_ref, o_ref, kmean_ref, *, n_blocks):
    blk = MOBA_BLOCK
    hp = pl.program_id(1)
    qi = pl.program_id(2)
    heads_per_step = LANES // HEAD_DIM

    @pl.when(qi == 0)
    def _():
        for n in range(n_blocks):
            kb = k_ref[n * blk:(n + 1) * blk, :].astype(jnp.float32)
            kmean_ref[n:n + 1, :] = jnp.sum(kb, axis=0, keepdims=True) * (1.0 / blk)

    q2 = q_ref[...]
    lane = lax.broadcasted_iota(jnp.int32, (blk, LANES), 1)
    row = lax.broadcasted_iota(jnp.int32, (blk, blk), 0)
    col = lax.broadcasted_iota(jnp.int32, (blk, blk), 1)
    d0 = (row - col).astype(jnp.float32)
    causal = row >= col
    blk_id = lax.broadcasted_iota(jnp.int32, (n_blocks, blk), 0)
    sel_lane = lax.broadcasted_iota(jnp.int32, (blk, LANES), 1)
    own0 = pl.multiple_of(qi * blk, blk)
    k_own = k_ref[pl.ds(own0, blk), :]
    v_own = v_ref[pl.ds(own0, blk), :]
    nt = (((1,), (1,)), ((), ()))

    out = jnp.zeros((blk, LANES), jnp.float32)
    for hh in range(heads_per_step):
        in_head = (lane >= hh * HEAD_DIM) & (lane < (hh + 1) * HEAD_DIM)
        qh = jnp.where(in_head, q2, 0.0).astype(jnp.bfloat16)
        slope = slopes_ref[hp * heads_per_step + hh]

        gate = lax.dot_general(kmean_ref[...], qh.astype(jnp.float32), nt,
                               preferred_element_type=jnp.float32,
                               precision=lax.Precision.HIGHEST)
        rank = jnp.zeros((n_blocks, blk), jnp.int32)
        for m in range(n_blocks):
            gm = gate[m:m + 1, :]
            ahead = (gm > gate) | ((gm == gate) & (m < blk_id))
            rank = rank + jnp.where(ahead & (m < qi), 1, 0)
        chosen = jnp.where((rank < MOBA_TOPK) & (blk_id < qi), 1.0, 0.0)
        chosen_pad = jnp.concatenate(
            [chosen, jnp.zeros((LANES - n_blocks, blk), jnp.float32)], axis=0)
        chosen_t = chosen_pad.T

        slope_d0 = slope * d0
        s = lax.dot_general(qh, k_own, nt, preferred_element_type=jnp.float32)
        s = jnp.where(causal, s - slope_d0, NEG_INF)
        m_run = jnp.max(s, axis=-1, keepdims=True)
        p = jnp.exp(s - m_run)
        l_run = jnp.sum(p, axis=-1, keepdims=True)
        acc = jnp.dot(p.astype(jnp.bfloat16), v_own, preferred_element_type=jnp.float32)

        def past_block(n, carry):
            m_run, l_run, acc = carry
            start = pl.multiple_of(n * blk, blk)
            kb = k_ref[pl.ds(start, blk), :]
            vb = v_ref[pl.ds(start, blk), :]
            picked = jnp.sum(jnp.where(sel_lane == n, chosen_t, 0.0), axis=-1, keepdims=True)
            s = lax.dot_general(qh, kb, nt, preferred_element_type=jnp.float32)
            far = slope * ((qi - n) * blk).astype(jnp.float32)
            s = jnp.where(picked > 0.5, s - slope_d0 - far, NEG_INF)
            m_new = jnp.maximum(m_run, jnp.max(s, axis=-1, keepdims=True))
            alpha = jnp.exp(m_run - m_new)
            p = jnp.exp(s - m_new)
            l_new = alpha * l_run + jnp.sum(p, axis=-1, keepdims=True)
            acc_new = alpha * acc + jnp.dot(p.astype(jnp.bfloat16), vb,
                                            preferred_element_type=jnp.float32)
            return m_new, l_new, acc_new

        m_run, l_run, acc = lax.fori_loop(0, qi, past_block, (m_run, l_run, acc))
        out = jnp.where(in_head, acc / l_run, out)
    o_ref[...] = out.astype(jnp.bfloat16)


def _moba(q, k, v, slopes):
    bn, s, width = q.shape
    assert s % MOBA_BLOCK == 0 and width % LANES == 0
    n_blocks = s // MOBA_BLOCK
    assert n_blocks <= LANES
    qspec = pl.BlockSpec((None, MOBA_BLOCK, LANES), lambda b, hp, qi: (b, qi, hp))
    kvspec = pl.BlockSpec((None, s, LANES), lambda b, hp, qi: (b, 0, hp))
    return pl.pallas_call(
        functools.partial(_moba_kernel, n_blocks=n_blocks),
        grid=(bn, width // LANES, n_blocks),
        in_specs=[pl.BlockSpec(memory_space=pltpu.SMEM), qspec, kvspec, kvspec],
        out_specs=qspec,
        out_shape=jax.ShapeDtypeStruct((bn, s, width), jnp.bfloat16),
        scratch_shapes=[pltpu.VMEM((n_blocks, LANES), jnp.float32)],
        compiler_params=pltpu.CompilerParams(
            dimension_semantics=("arbitrary", "arbitrary", "arbitrary"),
            vmem_limit_bytes=VMEM_LIMIT),
        name="moba",
    )(slopes, q, k, v)


def _mix_out_kernel(x_ref, ya_ref, yb_ref, yc_ref, wout_ref, gpost_ref, gprex_ref, wq_ref,
                    kx_ref, vx_ref, wo_ref, gpostx_ref, o_ref):
    d = x_ref.shape[-1]
    dh = d // X_HEADS
    y = jnp.concatenate([ya_ref[...], yb_ref[...], yc_ref[...]], axis=-1)
    mix = jnp.dot(y, wout_ref[...], preferred_element_type=jnp.float32)
    x1 = x_ref[...] + _rms(mix, gpost_ref[...])

    h = _rms(x1, gprex_ref[...]).astype(jnp.bfloat16)
    qx = (jnp.dot(h, wq_ref[...], preferred_element_type=jnp.float32) * (dh ** -0.5)
          ).astype(jnp.bfloat16)
    nt = (((1,), (1,)), ((), ()))
    heads = []
    for hh in range(X_HEADS):
        cols = slice(hh * dh, (hh + 1) * dh)
        s = lax.dot_general(qx[:, cols], kx_ref[:, cols], nt, preferred_element_type=jnp.float32)
        p = jnp.exp(s - jnp.max(s, axis=-1, keepdims=True))
        denom = jnp.sum(p, axis=-1, keepdims=True)
        oh = jnp.dot(p.astype(jnp.bfloat16), vx_ref[:, cols], preferred_element_type=jnp.float32)
        heads.append((oh / denom).astype(jnp.bfloat16))
    att = jnp.dot(jnp.concatenate(heads, axis=-1), wo_ref[...], preferred_element_type=jnp.float32)
    o_ref[...] = x1 + _rms(att, gpostx_ref[...])


def _mix_out(x, ya, yb, yc, l, w_out, g_post, g_prex, w_xq, kx, vx, w_xo, g_postx):
    bn, s, d = x.shape
    tm = min(TOKEN_TILE, s)
    m = kx.shape[2]
    lsel3 = lambda b, i: (l, 0, 0)
    tok = lambda w: pl.BlockSpec((None, tm, w), lambda b, i: (b, i, 0))
    gain = pl.BlockSpec((None, 1, d), lsel3)
    wsq = _const_spec((None, d, d), lsel3)
    mem = pl.BlockSpec((None, None, m, d), lambda b, i: (l, b, 0, 0))
    return pl.pallas_call(
        _mix_out_kernel,
        grid=(bn, s // tm),
        in_specs=[tok(d), tok(ya.shape[-1]), tok(yb.shape[-1]), tok(yc.shape[-1]),
                  wsq, gain, gain, wsq, mem, mem, wsq, gain],
        out_specs=tok(d),
        out_shape=jax.ShapeDtypeStruct((bn, s, d), jnp.float32),
        compiler_params=pltpu.CompilerParams(
            dimension_semantics=("arbitrary", "arbitrary"), vmem_limit_bytes=VMEM_LIMIT),
        name="mix_out",
    )(x, ya, yb, yc, w_out, g_post, g_prex, w_xq, kx, vx, w_xo, g_postx)


def _ffn_kernel(x_ref, gpre_ref, w1_ref, w2_ref, gpost_ref, o_ref):
    x = x_ref[...]
    h = _rms(x, gpre_ref[...]).astype(jnp.bfloat16)
    a = jnp.maximum(jnp.dot(h, w1_ref[...], preferred_element_type=jnp.float32), 0.0)
    f = jnp.dot((a * a).astype(jnp.bfloat16), w2_ref[...], preferred_element_type=jnp.float32)
    o_ref[...] = x + _rms(f, gpost_ref[...])


def _ffn(x, l, g_pre, w1, w2, g_post):
    bn, s, d = x.shape
    tm = min(TOKEN_TILE, s)
    dff = w1.shape[-1]
    lsel3 = lambda b, i: (l, 0, 0)
    tok = pl.BlockSpec((None, tm, d), lambda b, i: (b, i, 0))
    gain = pl.BlockSpec((None, 1, d), lsel3)
    return pl.pallas_call(
        _ffn_kernel,
        grid=(bn, s // tm),
        in_specs=[tok, gain, _const_spec((None, d, dff), lsel3),
                  _const_spec((None, dff, d), lsel3), gain],
        out_specs=tok,
        out_shape=jax.ShapeDtypeStruct((bn, s, d), jnp.float32),
        compiler_params=pltpu.CompilerParams(
            dimension_semantics=("arbitrary", "arbitrary"), vmem_limit_bytes=VMEM_LIMIT),
        name="ffn",
    )(x, g_pre, w1, w2, g_post)


def kernel(x, mem, pre_mix_g, w_in, gate_ln_g, gate_ln_b, w_s, b_s, w_dw, b_dw, conv_gn_g, conv_gn_b, w_out, post_mix_g, pre_x_g, mem_g, w_xq, w_xkv, w_xo, post_x_g, pre_ffn_g, w_ff1, w_ff2, post_ffn_g):
    depth, d, _ = w_in.shape
    a_width = gate_ln_g.shape[-1]
    c_width = w_dw.shape[-1]
    b_width = d - a_width - c_width
    a_heads = a_width // HEAD_DIM
    b_heads = b_width // HEAD_DIM
    assert w_in.shape[-1] == 2 * a_width + 3 * b_width + 2 * c_width
    assert w_s.shape[1:] == (a_heads, CHUNK, CHUNK) and w_dw.shape[1] == CONV_WIDTH
    assert x.shape[1] % MOBA_BLOCK == 0 and x.shape[1] % TOKEN_TILE == 0

    bf16 = lambda w: w.astype(jnp.bfloat16)
    row = lambda g: g[:, None, :]
    ws_cat = w_s.transpose(0, 2, 1, 3).reshape(depth, CHUNK, a_heads * CHUNK)
    bs_x = jnp.repeat(b_s.transpose(0, 2, 1), HEAD_DIM, axis=-1)
    slopes = 2.0 ** (-8.0 * jnp.arange(1, b_heads + 1, dtype=jnp.float32) / b_heads)
    w_in_b, w_out_b, w_xq_b, w_xkv_b, w_xo_b, w_ff1_b, w_ff2_b = map(
        bf16, (w_in, w_out, w_xq, w_xkv, w_xo, w_ff1, w_ff2))

    kx, vx = _mem_kv(mem, row(mem_g), w_xkv_b)
    for l in range(depth):
        ya, q, k, v, yc = _mix_in(
            x, l, row(pre_mix_g), w_in_b, row(gate_ln_g), row(gate_ln_b), ws_cat, bs_x, w_dw,
            row(b_dw), row(conv_gn_g), row(conv_gn_b),
            a_width=a_width, b_width=b_width, c_width=c_width)
        yb = _moba(q, k, v, slopes)
        x = _mix_out(x, ya, yb, yc, l, w_out_b, row(post_mix_g), row(pre_x_g), w_xq_b, kx, vx,
                     w_xo_b, row(post_x_g))
        x = _ffn(x, l, row(pre_ffn_g), w_ff1_b, w_ff2_b, row(post_ffn_g))
    return x
```

```python
import functools

import jax
import jax.numpy as jnp
from jax import lax
from jax.experimental import pallas as pl
from jax.experimental.pallas import tpu as pltpu

HEAD_DIM = 64
CHUNK = 128
MOBA_BLOCK = 256
MOBA_TOPK = 3
CONV_WIDTH = 31
X_HEADS = 4
EPS = 1e-6

LANES = 128
HALO = 32
CONV_ROWS = 64
TOKEN_TILE = 512
VMEM_LIMIT = 56 * 1024 * 1024

NEG_INF = float("-inf")


def _rms(x, g):
    return x * lax.rsqrt(jnp.mean(x * x, axis=-1, keepdims=True) + EPS) * g


def _const_spec(shape, index_map):
    return pl.BlockSpec(shape, index_map, pipeline_mode=pl.Buffered(1))


def _mem_kv_kernel(mem_ref, g_ref, w_ref, k_ref, v_ref):
    d = mem_ref.shape[-1]
    m = _rms(mem_ref[...], g_ref[...]).astype(jnp.bfloat16)
    kv = jnp.dot(m, w_ref[...], preferred_element_type=jnp.float32)
    k_ref[...] = kv[:, :d].astype(jnp.bfloat16)
    v_ref[...] = kv[:, d:].astype(jnp.bfloat16)


def _mem_kv(mem, mem_g, w_xkv):
    bn, m, d = mem.shape
    depth = w_xkv.shape[0]
    out = jax.ShapeDtypeStruct((depth, bn, m, d), jnp.bfloat16)
    return pl.pallas_call(
        _mem_kv_kernel,
        grid=(depth, bn),
        in_specs=[
            pl.BlockSpec((None, m, d), lambda l, b: (b, 0, 0)),
            pl.BlockSpec((None, 1, d), lambda l, b: (l, 0, 0)),
            pl.BlockSpec((None, d, 2 * d), lambda l, b: (l, 0, 0)),
        ],
        out_specs=[
            pl.BlockSpec((None, None, m, d), lambda l, b: (l, b, 0, 0)),
            pl.BlockSpec((None, None, m, d), lambda l, b: (l, b, 0, 0)),
        ],
        out_shape=[out, out],
        compiler_params=pltpu.CompilerParams(
            dimension_semantics=("arbitrary", "arbitrary"), vmem_limit_bytes=VMEM_LIMIT),
        name="mem_kv",
    )(mem, mem_g, w_xkv)


def _mix_in_kernel(x_ref, g_ref, win_ref, lng_ref, lnb_ref, ws_ref, bs_ref, wdw_ref, bdw_ref,
                   gng_ref, gnb_ref, ya_ref, q_ref, k_ref, v_ref, yc_ref, hist_ref,
                   *, a_width, b_width, c_width):
    tm = x_ref.shape[0]
    h = _rms(x_ref[...], g_ref[...]).astype(jnp.bfloat16)
    z = jnp.dot(h, win_ref[...], preferred_element_type=jnp.float32)

    o = 2 * a_width
    q_ref[...] = (z[:, o:o + b_width] * (HEAD_DIM ** -0.5)).astype(jnp.bfloat16)
    k_ref[...] = z[:, o + b_width:o + 2 * b_width].astype(jnp.bfloat16)
    v_ref[...] = z[:, o + 2 * b_width:o + 3 * b_width].astype(jnp.bfloat16)

    za = z[:, :2 * a_width]
    ga = 0.5 * za * (1.0 + lax.erf(za * (2.0 ** -0.5)))
    u = ga[:, :a_width]
    vv = ga[:, a_width:]
    mu = jnp.mean(vv, axis=-1, keepdims=True)
    vc = vv - mu
    var = jnp.mean(vc * vc, axis=-1, keepdims=True)
    vn = (vc * lax.rsqrt(var + EPS) * lng_ref[...] + lnb_ref[...]).astype(jnp.bfloat16)
    n_heads = a_width // HEAD_DIM
    t_idx = lax.broadcasted_iota(jnp.int32, (CHUNK, n_heads * CHUNK), 0)
    s_idx = lax.broadcasted_iota(jnp.int32, (CHUNK, n_heads * CHUNK), 1) % CHUNK
    w_tril = jnp.where(s_idx <= t_idx, ws_ref[...], 0.0).astype(jnp.bfloat16)
    r_head = lax.broadcasted_iota(jnp.int32, (n_heads * CHUNK, a_width), 0) // CHUNK
    c_head = lax.broadcasted_iota(jnp.int32, (n_heads * CHUNK, a_width), 1) // HEAD_DIM
    head_match = r_head == c_head
    for c in range(tm // CHUNK):
        rows = slice(c * CHUNK, (c + 1) * CHUNK)
        v_chunk = vn[rows]
        v_bd = jnp.where(head_match, jnp.concatenate([v_chunk] * n_heads, axis=0), 0.0)
        mixed = jnp.dot(w_tril, v_bd.astype(jnp.bfloat16), preferred_element_type=jnp.float32)
        ya_ref[rows, :] = (u[rows] * (mixed + bs_ref[...])).astype(jnp.bfloat16)

    o = 2 * a_width + 3 * b_width
    zc = z[:, o:]
    glu = zc[:, :c_width] * (1.0 / (1.0 + jnp.exp(-zc[:, c_width:])))

    @pl.when(pl.program_id(1) == 0)
    def _():
        hist_ref[0:HALO, :] = jnp.zeros((HALO, c_width), jnp.float32)

    hist_ref[HALO:HALO + tm, :] = glu
    gr = lax.broadcasted_iota(jnp.int32, (c_width, c_width), 0) // HEAD_DIM
    gc = lax.broadcasted_iota(jnp.int32, (c_width, c_width), 1) // HEAD_DIM
    gmat = jnp.where(gr == gc, 1.0 / HEAD_DIM, 0.0).astype(jnp.float32)
    first = HALO - (CONV_WIDTH - 1)
    for c in range(tm // CONV_ROWS):
        r0 = c * CONV_ROWS
        acc = jnp.zeros((CONV_ROWS, c_width), jnp.float32)
        for j in range(CONV_WIDTH):
            acc = acc + wdw_ref[j:j + 1, :] * hist_ref[r0 + first + j:r0 + first + j + CONV_ROWS, :]
        y = acc + bdw_ref[...]
        gmu = jnp.dot(y, gmat, preferred_element_type=jnp.float32, precision=lax.Precision.HIGHEST)
        yc_ = y - gmu
        gvar = jnp.dot(yc_ * yc_, gmat, preferred_element_type=jnp.float32,
                       precision=lax.Precision.HIGHEST)
        yn = yc_ * lax.rsqrt(gvar + EPS) * gng_ref[...] + gnb_ref[...]
        yc_ref[r0:r0 + CONV_ROWS, :] = (yn * (1.0 / (1.0 + jnp.exp(-yn)))).astype(jnp.bfloat16)
    hist_ref[0:HALO, :] = hist_ref[tm:tm + HALO, :]


def _mix_in(x, l, pre_g, w_in, ln_g, ln_b, ws_cat, bs_x, w_dw, b_dw, gn_g, gn_b,
            *, a_width, b_width, c_width):
    bn, s, d = x.shape
    tm = min(TOKEN_TILE, s)
    in_cols = w_in.shape[-1]
    n_heads = a_width // HEAD_DIM
    lsel3 = lambda b, i: (l, 0, 0)
    tok = lambda w: pl.BlockSpec((None, tm, w), lambda b, i: (b, i, 0))
    bf = lambda w: jax.ShapeDtypeStruct((bn, s, w), jnp.bfloat16)
    return pl.pallas_call(
        functools.partial(_mix_in_kernel, a_width=a_width, b_width=b_width, c_width=c_width),
        grid=(bn, s // tm),
        in_specs=[
            tok(d),
            pl.BlockSpec((None, 1, d), lsel3),
            _const_spec((None, d, in_cols), lsel3),
            pl.BlockSpec((None, 1, a_width), lsel3),
            pl.BlockSpec((None, 1, a_width), lsel3),
            pl.BlockSpec((None, CHUNK, n_heads * CHUNK), lsel3),
            pl.BlockSpec((None, CHUNK, a_width), lsel3),
            pl.BlockSpec((None, CONV_WIDTH, c_width), lsel3),
            pl.BlockSpec((None, 1, c_width), lsel3),
            pl.BlockSpec((None, 1, c_width), lsel3),
            pl.BlockSpec((None, 1, c_width), lsel3),
        ],
        out_specs=[tok(a_width), tok(b_width), tok(b_width), tok(b_width), tok(c_width)],
        out_shape=[bf(a_width), bf(b_width), bf(b_width), bf(b_width), bf(c_width)],
        scratch_shapes=[pltpu.VMEM((tm + HALO, c_width), jnp.float32)],
        compiler_params=pltpu.CompilerParams(
            dimension_semantics=("arbitrary", "arbitrary"), vmem_limit_bytes=VMEM_LIMIT),
        name="mix_in",
    )(x, pre_g, w_in, ln_g, ln_b, ws_cat, bs_x, w_dw, b_dw, gn_g, gn_b)


ALIBI_SLOTS = 4
HEADS_PER_TILE = LANES // HEAD_DIM


def _moba_kernel(slopes_ref, q_ref, k_ref, v_ref, o_ref, kmean_ref, kaug_ref, vt_ref, neg_ref,
                 s_ref, p_ref, acc_ref, *, n_blocks):
    blk = MOBA_BLOCK
    hp = pl.program_id(1)
    qi = pl.program_id(2)

    def other_head_base(hh):
        return ((hh + 1) % HEADS_PER_TILE) * HEAD_DIM

    @pl.when(qi == 0)
    def _():
        lane = lax.broadcasted_iota(jnp.int32, (blk, LANES), 1)
        j_pos = lax.broadcasted_iota(jnp.int32, (blk, LANES), 0).astype(jnp.float32)
        for n in range(n_blocks):
            rows = slice(n * blk, (n + 1) * blk)
            kf = k_ref[rows, :].astype(jnp.float32)
            kmean_ref[n:n + 1, :] = jnp.sum(kf, axis=0, keepdims=True) * (1.0 / blk)
            for hh in range(HEADS_PER_TILE):
                base = other_head_base(hh)
                aug = jnp.where(lane == base, j_pos,
                                jnp.where(lane == base + 1, float(n),
                                          jnp.where((lane == base + 2) | (lane == base + 3), 1.0, 0.0)))
                in_head = (lane >= hh * HEAD_DIM) & (lane < (hh + 1) * HEAD_DIM)
                kaug_ref[hh, n] = jnp.where(in_head, kf, aug).astype(jnp.bfloat16)
            vt_ref[n] = v_ref[rows, :].astype(jnp.float32).T.astype(jnp.bfloat16)

    q_t = q_ref[...].astype(jnp.float32).T
    feat = lax.broadcasted_iota(jnp.int32, (LANES, blk), 0)
    i_pos = lax.broadcasted_iota(jnp.int32, (LANES, blk), 1).astype(jnp.float32)
    km_lane = lax.broadcasted_iota(jnp.int32, (n_blocks, LANES), 1)
    blk_id = lax.broadcasted_iota(jnp.int32, (n_blocks, blk), 0)
    qi_f = qi.astype(jnp.float32)
    q_aug = []
    for hh in range(HEADS_PER_TILE):
        slope = slopes_ref[hp * HEADS_PER_TILE + hh]
        base = other_head_base(hh)
        in_head = (feat >= hh * HEAD_DIM) & (feat < (hh + 1) * HEAD_DIM)
        aug = jnp.where(feat == base, slope,
                        jnp.where(feat == base + 1, slope * blk,
                                  jnp.where(feat == base + 2, -slope * blk * qi_f,
                                            jnp.where(feat == base + 3, -slope * i_pos, 0.0))))
        q_aug.append(jnp.where(in_head, q_t, aug).astype(jnp.bfloat16))

        km_h = jnp.where((km_lane >= hh * HEAD_DIM) & (km_lane < (hh + 1) * HEAD_DIM),
                         kmean_ref[...], 0.0)
        gate = jnp.dot(km_h, jnp.where(in_head, q_t, 0.0), preferred_element_type=jnp.float32,
                       precision=lax.Precision.HIGHEST)
        rank = jnp.zeros((n_blocks, blk), jnp.int32)
        for m in range(n_blocks):
            gm = gate[m:m + 1, :]
            ahead = (gm > gate) | ((gm == gate) & (m < blk_id))
            rank = rank + jnp.where(ahead & (m < qi), 1, 0)
        neg = jnp.where((rank < MOBA_TOPK) & (blk_id < qi), 0.0, NEG_INF)
        neg_ref[hh, 0] = jnp.zeros((1, blk), jnp.float32)
        for u in range(1, n_blocks):
            neg_ref[hh, u] = neg[u - 1:u, :]

    def block_of(u):
        return jnp.where(u == 0, qi, jnp.minimum(u - 1, n_blocks - 1))

    key_i = lax.broadcasted_iota(jnp.int32, (blk, blk), 0)
    qry_i = lax.broadcasted_iota(jnp.int32, (blk, blk), 1)
    causal = key_i <= qry_i
    for hh in range(HEADS_PER_TILE):
        s = jnp.dot(kaug_ref[hh, qi], q_aug[hh], preferred_element_type=jnp.float32)
        s_ref[0, hh] = jnp.where(causal, s, NEG_INF)
        p_ref[1, hh] = jnp.zeros((blk, blk), jnp.bfloat16)
        acc_ref[hh] = jnp.zeros((LANES, blk), jnp.float32)

    def step(u, cur, stats):
        vt = vt_ref[block_of(jnp.maximum(u - 1, 0))]
        k_next = jnp.minimum(u, n_blocks - 1)
        pv = [jnp.dot(vt, p_ref[1 - cur, hh], preferred_element_type=jnp.float32)
              for hh in range(HEADS_PER_TILE)]
        for hh in range(HEADS_PER_TILE):
            s_ref[1 - cur, hh] = jnp.dot(kaug_ref[hh, k_next], q_aug[hh],
                                         preferred_element_type=jnp.float32)
        new = []
        for hh in range(HEADS_PER_TILE):
            m_run, l_run = stats[2 * hh:2 * hh + 2]
            s = s_ref[cur, hh] + neg_ref[hh, u]
            m_new = jnp.maximum(m_run, jnp.max(s, axis=0, keepdims=True))
            alpha = jnp.exp(m_run - m_new)
            p = jnp.exp(s - m_new)
            l_new = alpha * l_run + jnp.sum(p, axis=0, keepdims=True)
            p_ref[cur, hh] = p.astype(jnp.bfloat16)
            acc_ref[hh] = alpha * (acc_ref[hh] + pv[hh])
            new += [m_new, l_new]
        return tuple(new)

    def step_pair(i, stats):
        return step(2 * i + 1, 1, step(2 * i, 0, stats))

    row0 = jnp.zeros((1, blk), jnp.float32)
    stats = lax.fori_loop(0, (qi + 2) // 2, step_pair,
                          (row0 + NEG_INF, row0) * HEADS_PER_TILE)
    last = 2 * ((qi + 2) // 2) - 1
    vt = vt_ref[block_of(last)]
    out_t = None
    for hh in range(HEADS_PER_TILE):
        acc = acc_ref[hh] + jnp.dot(vt, p_ref[1, hh], preferred_element_type=jnp.float32)
        o_h = acc / stats[2 * hh + 1]
        out_t = o_h if out_t is None else jnp.where(feat >= hh * HEAD_DIM, o_h, out_t)
    o_ref[...] = out_t.T.astype(jnp.bfloat16)


def _moba(q, k, v, slopes):
    bn, s, width = q.shape
    assert s % MOBA_BLOCK == 0 and width % LANES == 0
    n_blocks = s // MOBA_BLOCK
    assert HEAD_DIM >= ALIBI_SLOTS and HEADS_PER_TILE >= 2
    assert n_blocks <= 256 and MOBA_BLOCK <= 256
    qspec = pl.BlockSpec((None, MOBA_BLOCK, LANES), lambda b, hp, qi: (b, qi, hp))
    kvspec = pl.BlockSpec((None, s, LANES), lambda b, hp, qi: (b, 0, hp))
    return pl.pallas_call(
        functools.partial(_moba_kernel, n_blocks=n_blocks),
        grid=(bn, width // LANES, n_blocks),
        in_specs=[pl.BlockSpec(memory_space=pltpu.SMEM), qspec, kvspec, kvspec],
        out_specs=qspec,
        out_shape=jax.ShapeDtypeStruct((bn, s, width), jnp.bfloat16),
        scratch_shapes=[
            pltpu.VMEM((n_blocks, LANES), jnp.float32),
            pltpu.VMEM((HEADS_PER_TILE, n_blocks, MOBA_BLOCK, LANES), jnp.bfloat16),
            pltpu.VMEM((n_blocks, LANES, MOBA_BLOCK), jnp.bfloat16),
            pltpu.VMEM((HEADS_PER_TILE, n_blocks, 1, MOBA_BLOCK), jnp.float32),
            pltpu.VMEM((2, HEADS_PER_TILE, MOBA_BLOCK, MOBA_BLOCK), jnp.float32),
            pltpu.VMEM((2, HEADS_PER_TILE, MOBA_BLOCK, MOBA_BLOCK), jnp.bfloat16),
            pltpu.VMEM((HEADS_PER_TILE, LANES, MOBA_BLOCK), jnp.float32),
        ],
        compiler_params=pltpu.CompilerParams(
            dimension_semantics=("arbitrary", "arbitrary", "arbitrary"),
            vmem_limit_bytes=VMEM_LIMIT),
        name="moba",
    )(slopes, q, k, v)


def _mix_out_kernel(x_ref, ya_ref, yb_ref, yc_ref, wout_ref, gpost_ref, gprex_ref, wq_ref,
                    kx_ref, vx_ref, wo_ref, gpostx_ref, o_ref):
    d = x_ref.shape[-1]
    dh = d // X_HEADS
    y = jnp.concatenate([ya_ref[...], yb_ref[...], yc_ref[...]], axis=-1)
    mix = jnp.dot(y, wout_ref[...], preferred_element_type=jnp.float32)
    x1 = x_ref[...] + _rms(mix, gpost_ref[...])

    h = _rms(x1, gprex_ref[...]).astype(jnp.bfloat16)
    qx = (jnp.dot(h, wq_ref[...], preferred_element_type=jnp.float32) * (dh ** -0.5)
          ).astype(jnp.bfloat16)
    nt = (((1,), (1,)), ((), ()))
    heads = []
    for hh in range(X_HEADS):
        cols = slice(hh * dh, (hh + 1) * dh)
        s = lax.dot_general(qx[:, cols], kx_ref[:, cols], nt, preferred_element_type=jnp.float32)
        p = jnp.exp(s - jnp.max(s, axis=-1, keepdims=True))
        denom = jnp.sum(p, axis=-1, keepdims=True)
        oh = jnp.dot(p.astype(jnp.bfloat16), vx_ref[:, cols], preferred_element_type=jnp.float32)
        heads.append((oh / denom).astype(jnp.bfloat16))
    att = jnp.dot(jnp.concatenate(heads, axis=-1), wo_ref[...], preferred_element_type=jnp.float32)
    o_ref[...] = x1 + _rms(att, gpostx_ref[...])


def _mix_out(x, ya, yb, yc, l, w_out, g_post, g_prex, w_xq, kx, vx, w_xo, g_postx):
    bn, s, d = x.shape
    tm = min(TOKEN_TILE, s)
    m = kx.shape[2]
    lsel3 = lambda b, i: (l, 0, 0)
    tok = lambda w: pl.BlockSpec((None, tm, w), lambda b, i: (b, i, 0))
    gain = pl.BlockSpec((None, 1, d), lsel3)
    wsq = _const_spec((None, d, d), lsel3)
    mem = pl.BlockSpec((None, None, m, d), lambda b, i: (l, b, 0, 0))
    return pl.pallas_call(
        _mix_out_kernel,
        grid=(bn, s // tm),
        in_specs=[tok(d), tok(ya.shape[-1]), tok(yb.shape[-1]), tok(yc.shape[-1]),
                  wsq, gain, gain, wsq, mem, mem, wsq, gain],
        out_specs=tok(d),
        out_shape=jax.ShapeDtypeStruct((bn, s, d), jnp.float32),
        compiler_params=pltpu.CompilerParams(
            dimension_semantics=("arbitrary", "arbitrary"), vmem_limit_bytes=VMEM_LIMIT),
        name="mix_out",
    )(x, ya, yb, yc, w_out, g_post, g_prex, w_xq, kx, vx, w_xo, g_postx)


def _ffn_kernel(x_ref, gpre_ref, w1_ref, w2_ref, gpost_ref, o_ref):
    x = x_ref[...]
    h = _rms(x, gpre_ref[...]).astype(jnp.bfloat16)
    a = jnp.maximum(jnp.dot(h, w1_ref[...], preferred_element_type=jnp.float32), 0.0)
    f = jnp.dot((a * a).astype(jnp.bfloat16), w2_ref[...], preferred_element_type=jnp.float32)
    o_ref[...] = x + _rms(f, gpost_ref[...])


def _ffn(x, l, g_pre, w1, w2, g_post):
    bn, s, d = x.shape
    tm = min(TOKEN_TILE, s)
    dff = w1.shape[-1]
    lsel3 = lambda b, i: (l, 0, 0)
    tok = pl.BlockSpec((None, tm, d), lambda b, i: (b, i, 0))
    gain = pl.BlockSpec((None, 1, d), lsel3)
    return pl.pallas_call(
        _ffn_kernel,
        grid=(bn, s // tm),
        in_specs=[tok, gain, _const_spec((None, d, dff), lsel3),
                  _const_spec((None, dff, d), lsel3), gain],
        out_specs=tok,
        out_shape=jax.ShapeDtypeStruct((bn, s, d), jnp.float32),
        compiler_params=pltpu.CompilerParams(
            dimension_semantics=("arbitrary", "arbitrary"), vmem_limit_bytes=VMEM_LIMIT),
        name="ffn",
    )(x, g_pre, w1, w2, g_post)


def kernel(x, mem, pre_mix_g, w_in, gate_ln_g, gate_ln_b, w_s, b_s, w_dw, b_dw, conv_gn_g, conv_gn_b, w_out, post_mix_g, pre_x_g, mem_g, w_xq, w_xkv, w_xo, post_x_g, pre_ffn_g, w_ff1, w_ff2, post_ffn_g):
    depth, d, _ = w_in.shape
    a_width = gate_ln_g.shape[-1]
    c_width = w_dw.shape[-1]
    b_width = d - a_width - c_width
    a_heads = a_width // HEAD_DIM
    b_heads = b_width // HEAD_DIM
    assert w_in.shape[-1] == 2 * a_width + 3 * b_width + 2 * c_width
    assert w_s.shape[1:] == (a_heads, CHUNK, CHUNK) and w_dw.shape[1] == CONV_WIDTH
    assert x.shape[1] % MOBA_BLOCK == 0 and x.shape[1] % TOKEN_TILE == 0
    assert 8 % b_heads == 0

    bf16 = lambda w: w.astype(jnp.bfloat16)
    row = lambda g: g[:, None, :]
    ws_cat = w_s.transpose(0, 2, 1, 3).reshape(depth, CHUNK, a_heads * CHUNK)
    bs_x = jnp.repeat(b_s.transpose(0, 2, 1), HEAD_DIM, axis=-1)
    slopes = 2.0 ** (-8.0 * jnp.arange(1, b_heads + 1, dtype=jnp.float32) / b_heads)
    w_in_b, w_out_b, w_xq_b, w_xkv_b, w_xo_b, w_ff1_b, w_ff2_b = map(
        bf16, (w_in, w_out, w_xq, w_xkv, w_xo, w_ff1, w_ff2))

    kx, vx = _mem_kv(mem, row(mem_g), w_xkv_b)
    for l in range(depth):
        ya, q, k, v, yc = _mix_in(
            x, l, row(pre_mix_g), w_in_b, row(gate_ln_g), row(gate_ln_b), ws_cat, bs_x, w_dw,
            row(b_dw), row(conv_gn_g), row(conv_gn_b),
            a_width=a_width, b_width=b_width, c_width=c_width)
        yb = _moba(q, k, v, slopes)
        x = _mix_out(x, ya, yb, yc, l, w_out_b, row(post_mix_g), row(pre_x_g), w_xq_b, kx, vx,
                     w_xo_b, row(post_x_g))
        x = _ffn(x, l, row(pre_ffn_g), w_ff1_b, w_ff2_b, row(post_ffn_g))
    return x
```

```python
import functools

import jax
import jax.numpy as jnp
from jax import lax
from jax.experimental import pallas as pl
from jax.experimental.pallas import tpu as pltpu

HEAD_DIM = 64
CHUNK = 128
MOBA_BLOCK = 256
MOBA_TOPK = 3
CONV_WIDTH = 31
X_HEADS = 4
EPS = 1e-6

LANES = 128
HALO = 32
CONV_ROWS = 64
TOKEN_TILE = 512
VMEM_LIMIT = 56 * 1024 * 1024

NEG_INF = float("-inf")


def _rms(x, g):
    return x * lax.rsqrt(jnp.mean(x * x, axis=-1, keepdims=True) + EPS) * g


def _const_spec(shape, index_map):
    return pl.BlockSpec(shape, index_map, pipeline_mode=pl.Buffered(1))


def _mem_kv_kernel(mem_ref, g_ref, w_ref, k_ref, v_ref):
    d = mem_ref.shape[-1]
    m = _rms(mem_ref[...], g_ref[...]).astype(jnp.bfloat16)
    kv = jnp.dot(m, w_ref[...], preferred_element_type=jnp.float32)
    k_ref[...] = kv[:, :d].astype(jnp.bfloat16)
    v_ref[...] = kv[:, d:].astype(jnp.bfloat16)


def _mem_kv(mem, mem_g, w_xkv):
    bn, m, d = mem.shape
    depth = w_xkv.shape[0]
    out = jax.ShapeDtypeStruct((depth, bn, m, d), jnp.bfloat16)
    return pl.pallas_call(
        _mem_kv_kernel,
        grid=(depth, bn),
        in_specs=[
            pl.BlockSpec((None, m, d), lambda l, b: (b, 0, 0)),
            pl.BlockSpec((None, 1, d), lambda l, b: (l, 0, 0)),
            pl.BlockSpec((None, d, 2 * d), lambda l, b: (l, 0, 0)),
        ],
        out_specs=[
            pl.BlockSpec((None, None, m, d), lambda l, b: (l, b, 0, 0)),
            pl.BlockSpec((None, None, m, d), lambda l, b: (l, b, 0, 0)),
        ],
        out_shape=[out, out],
        compiler_params=pltpu.CompilerParams(
            dimension_semantics=("arbitrary", "arbitrary"), vmem_limit_bytes=VMEM_LIMIT),
        name="mem_kv",
    )(mem, mem_g, w_xkv)


def _mix_in_kernel(x_ref, g_ref, win_ref, lng_ref, lnb_ref, ws_ref, bs_ref, wdw_ref, bdw_ref,
                   gng_ref, gnb_ref, ya_ref, q_ref, k_ref, v_ref, yc_ref, hist_ref,
                   *, a_width, b_width, c_width):
    tm = x_ref.shape[0]
    h = _rms(x_ref[...], g_ref[...]).astype(jnp.bfloat16)
    z = jnp.dot(h, win_ref[...], preferred_element_type=jnp.float32)

    o = 2 * a_width
    q_ref[...] = (z[:, o:o + b_width] * (HEAD_DIM ** -0.5)).astype(jnp.bfloat16)
    k_ref[...] = z[:, o + b_width:o + 2 * b_width].astype(jnp.bfloat16)
    v_ref[...] = z[:, o + 2 * b_width:o + 3 * b_width].astype(jnp.bfloat16)

    za = z[:, :2 * a_width]
    ga = 0.5 * za * (1.0 + lax.erf(za * (2.0 ** -0.5)))
    u = ga[:, :a_width]
    vv = ga[:, a_width:]
    mu = jnp.mean(vv, axis=-1, keepdims=True)
    vc = vv - mu
    var = jnp.mean(vc * vc, axis=-1, keepdims=True)
    vn = (vc * lax.rsqrt(var + EPS) * lng_ref[...] + lnb_ref[...]).astype(jnp.bfloat16)
    n_heads = a_width // HEAD_DIM
    t_idx = lax.broadcasted_iota(jnp.int32, (CHUNK, n_heads * CHUNK), 0)
    s_idx = lax.broadcasted_iota(jnp.int32, (CHUNK, n_heads * CHUNK), 1) % CHUNK
    w_tril = jnp.where(s_idx <= t_idx, ws_ref[...], 0.0).astype(jnp.bfloat16)
    r_head = lax.broadcasted_iota(jnp.int32, (n_heads * CHUNK, a_width), 0) // CHUNK
    c_head = lax.broadcasted_iota(jnp.int32, (n_heads * CHUNK, a_width), 1) // HEAD_DIM
    head_match = r_head == c_head
    for c in range(tm // CHUNK):
        rows = slice(c * CHUNK, (c + 1) * CHUNK)
        v_chunk = vn[rows]
        v_bd = jnp.where(head_match, jnp.concatenate([v_chunk] * n_heads, axis=0), 0.0)
        mixed = jnp.dot(w_tril, v_bd.astype(jnp.bfloat16), preferred_element_type=jnp.float32)
        ya_ref[rows, :] = (u[rows] * (mixed + bs_ref[...])).astype(jnp.bfloat16)

    o = 2 * a_width + 3 * b_width
    zc = z[:, o:]
    glu = zc[:, :c_width] * (1.0 / (1.0 + jnp.exp(-zc[:, c_width:])))

    @pl.when(pl.program_id(1) == 0)
    def _():
        hist_ref[0:HALO, :] = jnp.zeros((HALO, c_width), jnp.float32)

    hist_ref[HALO:HALO + tm, :] = glu
    gr = lax.broadcasted_iota(jnp.int32, (c_width, c_width), 0) // HEAD_DIM
    gc = lax.broadcasted_iota(jnp.int32, (c_width, c_width), 1) // HEAD_DIM
    gmat = jnp.where(gr == gc, 1.0 / HEAD_DIM, 0.0).astype(jnp.float32)
    first = HALO - (CONV_WIDTH - 1)
    for c in range(tm // CONV_ROWS):
        r0 = c * CONV_ROWS
        acc = jnp.zeros((CONV_ROWS, c_width), jnp.float32)
        for j in range(CONV_WIDTH):
            acc = acc + wdw_ref[j:j + 1, :] * hist_ref[r0 + first + j:r0 + first + j + CONV_ROWS, :]
        y = acc + bdw_ref[...]
        gmu = jnp.dot(y, gmat, preferred_element_type=jnp.float32, precision=lax.Precision.HIGHEST)
        yc_ = y - gmu
        gvar = jnp.dot(yc_ * yc_, gmat, preferred_element_type=jnp.float32,
                       precision=lax.Precision.HIGHEST)
        yn = yc_ * lax.rsqrt(gvar + EPS) * gng_ref[...] + gnb_ref[...]
        yc_ref[r0:r0 + CONV_ROWS, :] = (yn * (1.0 / (1.0 + jnp.exp(-yn)))).astype(jnp.bfloat16)
    hist_ref[0:HALO, :] = hist_ref[tm:tm + HALO, :]


def _mix_in(x, l, pre_g, w_in, ln_g, ln_b, ws_cat, bs_x, w_dw, b_dw, gn_g, gn_b,
            *, a_width, b_width, c_width):
    bn, s, d = x.shape
    tm = min(TOKEN_TILE, s)
    in_cols = w_in.shape[-1]
    n_heads = a_width // HEAD_DIM
    lsel3 = lambda b, i: (l, 0, 0)
    tok = lambda w: pl.BlockSpec((None, tm, w), lambda b, i: (b, i, 0))
    bf = lambda w: jax.ShapeDtypeStruct((bn, s, w), jnp.bfloat16)
    return pl.pallas_call(
        functools.partial(_mix_in_kernel, a_width=a_width, b_width=b_width, c_width=c_width),
        grid=(bn, s // tm),
        in_specs=[
            tok(d),
            pl.BlockSpec((None, 1, d), lsel3),
            _const_spec((None, d, in_cols), lsel3),
            pl.BlockSpec((None, 1, a_width), lsel3),
            pl.BlockSpec((None, 1, a_width), lsel3),
            pl.BlockSpec((None, CHUNK, n_heads * CHUNK), lsel3),
            pl.BlockSpec((None, CHUNK, a_width), lsel3),
            pl.BlockSpec((None, CONV_WIDTH, c_width), lsel3),
            pl.BlockSpec((None, 1, c_width), lsel3),
            pl.BlockSpec((None, 1, c_width), lsel3),
            pl.BlockSpec((None, 1, c_width), lsel3),
        ],
        out_specs=[tok(a_width), tok(b_width), tok(b_width), tok(b_width), tok(c_width)],
        out_shape=[bf(a_width), bf(b_width), bf(b_width), bf(b_width), bf(c_width)],
        scratch_shapes=[pltpu.VMEM((tm + HALO, c_width), jnp.float32)],
        compiler_params=pltpu.CompilerParams(
            dimension_semantics=("arbitrary", "arbitrary"), vmem_limit_bytes=VMEM_LIMIT),
        name="mix_in",
    )(x, pre_g, w_in, ln_g, ln_b, ws_cat, bs_x, w_dw, b_dw, gn_g, gn_b)


ALIBI_SLOTS = 4
HEADS_PER_TILE = LANES // HEAD_DIM
BF16_SUBLANES = 16
PV_ROWS = HEAD_DIM + BF16_SUBLANES
GATE_PIECES = 3


def _moba_kernel(slopes_ref, q_ref, k_ref, v_ref, o_ref, km_ref, kaug_ref, vt_ref, neg_ref,
                 gate_ref, s_ref, p_ref, acc_ref, *, n_blocks):
    blk = MOBA_BLOCK
    n_tiles = q_ref.shape[-1] // LANES
    n_heads = n_tiles * HEADS_PER_TILE
    hg = pl.program_id(1)
    qi = pl.program_id(2)

    def own_lanes(h, idx):
        hh = h % HEADS_PER_TILE
        return (idx >= hh * HEAD_DIM) & (idx < (hh + 1) * HEAD_DIM)

    def slot_base(h):
        return ((h + 1) % HEADS_PER_TILE) * HEAD_DIM

    @pl.when(qi == 0)
    def _():
        lane = lax.broadcasted_iota(jnp.int32, (blk, LANES), 1)
        j_pos = lax.broadcasted_iota(jnp.int32, (blk, LANES), 0).astype(jnp.float32)
        ones_rows = (lax.broadcasted_iota(jnp.int32, (BF16_SUBLANES, blk), 0) == 0
                     ).astype(jnp.float32)
        km_row = lax.broadcasted_iota(jnp.int32, (n_blocks, LANES), 0)
        km_lane = lax.broadcasted_iota(jnp.int32, (n_blocks, LANES), 1)
        for t in range(n_tiles):
            cols = slice(t * LANES, (t + 1) * LANES)
            kmean = jnp.zeros((n_blocks, LANES), jnp.float32)
            for n in range(n_blocks):
                rows = slice(n * blk, (n + 1) * blk)
                kf = k_ref[rows, cols].astype(jnp.float32)
                kmean = jnp.where(km_row == n, jnp.sum(kf, axis=0, keepdims=True) * (1.0 / blk), kmean)
                v_t = v_ref[rows, cols].astype(jnp.float32).T
                for hh in range(HEADS_PER_TILE):
                    h = t * HEADS_PER_TILE + hh
                    base = slot_base(h)
                    aug = jnp.where(lane == base, j_pos,
                                    jnp.where(lane == base + 1, float(n),
                                              jnp.where((lane == base + 2) | (lane == base + 3), 1.0, 0.0)))
                    kaug_ref[h, n] = jnp.where(own_lanes(h, lane), kf, aug).astype(jnp.bfloat16)
                    vt_ref[h, n] = jnp.concatenate(
                        [v_t[hh * HEAD_DIM:(hh + 1) * HEAD_DIM], ones_rows], axis=0).astype(jnp.bfloat16)
            for hh in range(HEADS_PER_TILE):
                h = t * HEADS_PER_TILE + hh
                rest = jnp.where(own_lanes(h, km_lane), kmean, 0.0)
                pieces = []
                for _ in range(GATE_PIECES):
                    piece = rest.astype(jnp.bfloat16).astype(jnp.float32)
                    pieces.append(piece)
                    rest = rest - piece
                km_ref[h] = jnp.concatenate(pieces[::-1], axis=0).astype(jnp.bfloat16)

    feat = lax.broadcasted_iota(jnp.int32, (LANES, blk), 0)
    i_pos = lax.broadcasted_iota(jnp.int32, (LANES, blk), 1).astype(jnp.float32)
    qi_f = qi.astype(jnp.float32)
    q_aug = []
    for t in range(n_tiles):
        q_t = q_ref[:, t * LANES:(t + 1) * LANES].astype(jnp.float32).T
        for hh in range(HEADS_PER_TILE):
            h = t * HEADS_PER_TILE + hh
            slope = slopes_ref[hg * n_heads + h]
            base = slot_base(h)
            aug = jnp.where(feat == base, slope,
                            jnp.where(feat == base + 1, slope * blk,
                                      jnp.where(feat == base + 2, -slope * blk * qi_f,
                                                jnp.where(feat == base + 3, -slope * i_pos, 0.0))))
            q_aug.append(jnp.where(own_lanes(h, feat), q_t, aug).astype(jnp.bfloat16))

    def block_of(u):
        return jnp.where(u == 0, qi, jnp.minimum(u - 1, n_blocks - 1))

    key_i = lax.broadcasted_iota(jnp.int32, (blk, blk), 0)
    qry_i = lax.broadcasted_iota(jnp.int32, (blk, blk), 1)
    causal = key_i <= qry_i
    for h in range(n_heads):
        s = jnp.dot(kaug_ref[h, qi], q_aug[h], preferred_element_type=jnp.float32)
        s_ref[0, h] = jnp.where(causal, s, NEG_INF)
        p_ref[1, h] = jnp.zeros((blk, blk), jnp.bfloat16)
        acc_ref[h] = jnp.zeros((PV_ROWS, blk), jnp.float32)

    blk_id = lax.broadcasted_iota(jnp.int32, (n_blocks, blk), 0)
    for h in range(n_heads):
        g3 = jnp.dot(km_ref[h], q_aug[h], preferred_element_type=jnp.float32)
        gate = g3[:n_blocks] + g3[n_blocks:2 * n_blocks] + g3[2 * n_blocks:]
        gate = jnp.where(blk_id < qi, gate, NEG_INF)
        gate_ref[h] = gate
        ranks = [jnp.zeros((n_blocks, blk), jnp.float32) for _ in range(4)]
        for m in range(n_blocks - 1):
            gm = jnp.broadcast_to(gate_ref[h, m:m + 1, :], (n_blocks, blk))
            ahead = (gm > gate) | ((gm == gate) & (blk_id > m))
            ranks[m % 4] = ranks[m % 4] + jnp.where(ahead, 1.0, 0.0)
        rank = (ranks[0] + ranks[1]) + (ranks[2] + ranks[3])
        neg = jnp.where((rank < MOBA_TOPK) & (blk_id < qi), 0.0, NEG_INF)
        neg_ref[h, 0] = jnp.zeros((1, blk), jnp.float32)
        for u in range(1, n_blocks):
            neg_ref[h, u] = neg[u - 1:u, :]

    def step(u, cur, m_runs):
        b_prev = block_of(jnp.maximum(u - 1, 0))
        k_next = jnp.minimum(u, n_blocks - 1)
        pv = [jnp.dot(vt_ref[h, b_prev], p_ref[1 - cur, h], preferred_element_type=jnp.float32)
              for h in range(n_heads)]
        for h in range(n_heads):
            s_ref[1 - cur, h] = jnp.dot(kaug_ref[h, k_next], q_aug[h],
                                        preferred_element_type=jnp.float32)
        m_news = []
        for h in range(n_heads):
            s = s_ref[cur, h]
            neg = neg_ref[h, u]
            m_new = jnp.maximum(m_runs[h], jnp.max(s, axis=0, keepdims=True) + neg)
            alpha = jnp.exp(m_runs[h] - m_new)
            p = jnp.exp(s - (m_new - neg))
            p_ref[cur, h] = p.astype(jnp.bfloat16)
            acc_ref[h] = alpha * (acc_ref[h] + pv[h])
            m_news.append(m_new)
        return tuple(m_news)

    def step_pair(i, m_runs):
        return step(2 * i + 1, 1, step(2 * i, 0, m_runs))

    m_init = jnp.full((1, blk), NEG_INF, jnp.float32)
    lax.fori_loop(0, (qi + 2) // 2, step_pair, (m_init,) * n_heads)
    last = 2 * ((qi + 2) // 2) - 1
    for t in range(n_tiles):
        outs = []
        for h in range(t * HEADS_PER_TILE, (t + 1) * HEADS_PER_TILE):
            acc = acc_ref[h] + jnp.dot(vt_ref[h, block_of(last)], p_ref[1, h],
                                       preferred_element_type=jnp.float32)
            outs.append(acc[:HEAD_DIM] / acc[HEAD_DIM:HEAD_DIM + 1])
        o_ref[:, t * LANES:(t + 1) * LANES] = jnp.concatenate(outs, axis=0).T.astype(jnp.bfloat16)


MOBA_TILES_PER_STEP = 2


def _moba(q, k, v, slopes):
    bn, s, width = q.shape
    step_w = MOBA_TILES_PER_STEP * LANES
    assert s % MOBA_BLOCK == 0 and width % step_w == 0
    n_blocks = s // MOBA_BLOCK
    n_heads = MOBA_TILES_PER_STEP * HEADS_PER_TILE
    assert HEAD_DIM >= ALIBI_SLOTS and HEADS_PER_TILE >= 2
    assert n_blocks <= 256 and MOBA_BLOCK <= 256
    qspec = pl.BlockSpec((None, MOBA_BLOCK, step_w), lambda b, hg, qi: (b, qi, hg))
    kvspec = pl.BlockSpec((None, s, step_w), lambda b, hg, qi: (b, 0, hg))
    return pl.pallas_call(
        functools.partial(_moba_kernel, n_blocks=n_blocks),
        grid=(bn, width // step_w, n_blocks),
        in_specs=[pl.BlockSpec(memory_space=pltpu.SMEM), qspec, kvspec, kvspec],
        out_specs=qspec,
        out_shape=jax.ShapeDtypeStruct((bn, s, width), jnp.bfloat16),
        scratch_shapes=[
            pltpu.VMEM((n_heads, GATE_PIECES * n_blocks, LANES), jnp.bfloat16),
            pltpu.VMEM((n_heads, n_blocks, MOBA_BLOCK, LANES), jnp.bfloat16),
            pltpu.VMEM((n_heads, n_blocks, PV_ROWS, MOBA_BLOCK), jnp.bfloat16),
            pltpu.VMEM((n_heads, n_blocks, 1, MOBA_BLOCK), jnp.float32),
            pltpu.VMEM((n_heads, n_blocks, MOBA_BLOCK), jnp.float32),
            pltpu.VMEM((2, n_heads, MOBA_BLOCK, MOBA_BLOCK), jnp.float32),
            pltpu.VMEM((2, n_heads, MOBA_BLOCK, MOBA_BLOCK), jnp.bfloat16),
            pltpu.VMEM((n_heads, PV_ROWS, MOBA_BLOCK), jnp.float32),
        ],
        compiler_params=pltpu.CompilerParams(
            dimension_semantics=("arbitrary", "arbitrary", "arbitrary"),
            vmem_limit_bytes=VMEM_LIMIT),
        name="moba",
    )(slopes, q, k, v)


def _mix_out_kernel(x_ref, ya_ref, yb_ref, yc_ref, wout_ref, gpost_ref, gprex_ref, wq_ref,
                    kx_ref, vx_ref, wo_ref, gpostx_ref, o_ref):
    d = x_ref.shape[-1]
    dh = d // X_HEADS
    y = jnp.concatenate([ya_ref[...], yb_ref[...], yc_ref[...]], axis=-1)
    mix = jnp.dot(y, wout_ref[...], preferred_element_type=jnp.float32)
    x1 = x_ref[...] + _rms(mix, gpost_ref[...])

    h = _rms(x1, gprex_ref[...]).astype(jnp.bfloat16)
    qx = (jnp.dot(h, wq_ref[...], preferred_element_type=jnp.float32) * (dh ** -0.5)
          ).astype(jnp.bfloat16)
    nt = (((1,), (1,)), ((), ()))
    heads = []
    for hh in range(X_HEADS):
        cols = slice(hh * dh, (hh + 1) * dh)
        s = lax.dot_general(qx[:, cols], kx_ref[:, cols], nt, preferred_element_type=jnp.float32)
        p = jnp.exp(s - jnp.max(s, axis=-1, keepdims=True))
        denom = jnp.sum(p, axis=-1, keepdims=True)
        oh = jnp.dot(p.astype(jnp.bfloat16), vx_ref[:, cols], preferred_element_type=jnp.float32)
        heads.append((oh / denom).astype(jnp.bfloat16))
    att = jnp.dot(jnp.concatenate(heads, axis=-1), wo_ref[...], preferred_element_type=jnp.float32)
    o_ref[...] = x1 + _rms(att, gpostx_ref[...])


def _mix_out(x, ya, yb, yc, l, w_out, g_post, g_prex, w_xq, kx, vx, w_xo, g_postx):
    bn, s, d = x.shape
    tm = min(TOKEN_TILE, s)
    m = kx.shape[2]
    lsel3 = lambda b, i: (l, 0, 0)
    tok = lambda w: pl.BlockSpec((None, tm, w), lambda b, i: (b, i, 0))
    gain = pl.BlockSpec((None, 1, d), lsel3)
    wsq = _const_spec((None, d, d), lsel3)
    mem = pl.BlockSpec((None, None, m, d), lambda b, i: (l, b, 0, 0))
    return pl.pallas_call(
        _mix_out_kernel,
        grid=(bn, s // tm),
        in_specs=[tok(d), tok(ya.shape[-1]), tok(yb.shape[-1]), tok(yc.shape[-1]),
                  wsq, gain, gain, wsq, mem, mem, wsq, gain],
        out_specs=tok(d),
        out_shape=jax.ShapeDtypeStruct((bn, s, d), jnp.float32),
        compiler_params=pltpu.CompilerParams(
            dimension_semantics=("arbitrary", "arbitrary"), vmem_limit_bytes=VMEM_LIMIT),
        name="mix_out",
    )(x, ya, yb, yc, w_out, g_post, g_prex, w_xq, kx, vx, w_xo, g_postx)


def _ffn_kernel(x_ref, gpre_ref, w1_ref, w2_ref, gpost_ref, o_ref):
    x = x_ref[...]
    h = _rms(x, gpre_ref[...]).astype(jnp.bfloat16)
    a = jnp.maximum(jnp.dot(h, w1_ref[...], preferred_element_type=jnp.float32), 0.0)
    f = jnp.dot((a * a).astype(jnp.bfloat16), w2_ref[...], preferred_element_type=jnp.float32)
    o_ref[...] = x + _rms(f, gpost_ref[...])


def _ffn(x, l, g_pre, w1, w2, g_post):
    bn, s, d = x.shape
    tm = min(TOKEN_TILE, s)
    dff = w1.shape[-1]
    lsel3 = lambda b, i: (l, 0, 0)
    tok = pl.BlockSpec((None, tm, d), lambda b, i: (b, i, 0))
    gain = pl.BlockSpec((None, 1, d), lsel3)
    return pl.pallas_call(
        _ffn_kernel,
        grid=(bn, s // tm),
        in_specs=[tok, gain, _const_spec((None, d, dff), lsel3),
                  _const_spec((None, dff, d), lsel3), gain],
        out_specs=tok,
        out_shape=jax.ShapeDtypeStruct((bn, s, d), jnp.float32),
        compiler_params=pltpu.CompilerParams(
            dimension_semantics=("arbitrary", "arbitrary"), vmem_limit_bytes=VMEM_LIMIT),
        name="ffn",
    )(x, g_pre, w1, w2, g_post)


def kernel(x, mem, pre_mix_g, w_in, gate_ln_g, gate_ln_b, w_s, b_s, w_dw, b_dw, conv_gn_g, conv_gn_b, w_out, post_mix_g, pre_x_g, mem_g, w_xq, w_xkv, w_xo, post_x_g, pre_ffn_g, w_ff1, w_ff2, post_ffn_g):
    depth, d, _ = w_in.shape
    a_width = gate_ln_g.shape[-1]
    c_width = w_dw.shape[-1]
    b_width = d - a_width - c_width
    a_heads = a_width // HEAD_DIM
    b_heads = b_width // HEAD_DIM
    assert w_in.shape[-1] == 2 * a_width + 3 * b_width + 2 * c_width
    assert w_s.shape[1:] == (a_heads, CHUNK, CHUNK) and w_dw.shape[1] == CONV_WIDTH
    assert x.shape[1] % MOBA_BLOCK == 0 and x.shape[1] % TOKEN_TILE == 0
    assert 8 % b_heads == 0

    bf16 = lambda w: w.astype(jnp.bfloat16)
    row = lambda g: g[:, None, :]
    ws_cat = w_s.transpose(0, 2, 1, 3).reshape(depth, CHUNK, a_heads * CHUNK)
    bs_x = jnp.repeat(b_s.transpose(0, 2, 1), HEAD_DIM, axis=-1)
    slopes = 2.0 ** (-8.0 * jnp.arange(1, b_heads + 1, dtype=jnp.float32) / b_heads)
    w_in_b, w_out_b, w_xq_b, w_xkv_b, w_xo_b, w_ff1_b, w_ff2_b = map(
        bf16, (w_in, w_out, w_xq, w_xkv, w_xo, w_ff1, w_ff2))

    kx, vx = _mem_kv(mem, row(mem_g), w_xkv_b)
    for l in range(depth):
        ya, q, k, v, yc = _mix_in(
            x, l, row(pre_mix_g), w_in_b, row(gate_ln_g), row(gate_ln_b), ws_cat, bs_x, w_dw,
            row(b_dw), row(conv_gn_g), row(conv_gn_b),
            a_width=a_width, b_width=b_width, c_width=c_width)
        yb = _moba(q, k, v, slopes)
        x = _mix_out(x, ya, yb, yc, l, w_out_b, row(post_mix_g), row(pre_x_g), w_xq_b, kx, vx,
                     w_xo_b, row(post_x_g))
        x = _ffn(x, l, row(pre_ffn_g), w_ff1_b, w_ff2_b, row(post_ffn_g))
    return x
```

```python
import functools
import math

import jax
import jax.numpy as jnp
from jax import lax
from jax.experimental import pallas as pl
from jax.experimental.pallas import tpu as pltpu

HEAD_DIM = 64
CHUNK = 128
MOBA_BLOCK = 256
MOBA_TOPK = 3
CONV_WIDTH = 31
X_HEADS = 4
EPS = 1e-6
LOG2_E = math.log2(math.e)

LANES = 128
SUBLANES = 8
HALO = 32
CONV_ROWS = 64
TOKEN_TILE = 512
VMEM_LIMIT = 56 * 1024 * 1024

NEG_INF = float("-inf")


def _rms(x, g):
    return x * lax.rsqrt(jnp.mean(x * x, axis=-1, keepdims=True) + EPS) * g


def _const_spec(shape, index_map):
    return pl.BlockSpec(shape, index_map, pipeline_mode=pl.Buffered(1))


def _mem_kv_kernel(mem_ref, g_ref, w_ref, k_ref, v_ref):
    d = mem_ref.shape[-1]
    m = _rms(mem_ref[...], g_ref[...]).astype(jnp.bfloat16)
    kv = jnp.dot(m, w_ref[...], preferred_element_type=jnp.float32)
    k_ref[...] = kv[:, :d].astype(jnp.bfloat16)
    v_ref[...] = kv[:, d:].astype(jnp.bfloat16)


def _mem_kv(mem, mem_g, w_xkv):
    bn, m, d = mem.shape
    depth = w_xkv.shape[0]
    out = jax.ShapeDtypeStruct((depth, bn, m, d), jnp.bfloat16)
    return pl.pallas_call(
        _mem_kv_kernel,
        grid=(depth, bn),
        in_specs=[
            pl.BlockSpec((None, m, d), lambda l, b: (b, 0, 0)),
            pl.BlockSpec((None, 1, d), lambda l, b: (l, 0, 0)),
            pl.BlockSpec((None, d, 2 * d), lambda l, b: (l, 0, 0)),
        ],
        out_specs=[
            pl.BlockSpec((None, None, m, d), lambda l, b: (l, b, 0, 0)),
            pl.BlockSpec((None, None, m, d), lambda l, b: (l, b, 0, 0)),
        ],
        out_shape=[out, out],
        compiler_params=pltpu.CompilerParams(
            dimension_semantics=("arbitrary", "arbitrary"), vmem_limit_bytes=VMEM_LIMIT),
        name="mem_kv",
    )(mem, mem_g, w_xkv)


def _mix_in_kernel(x_ref, g_ref, win_ref, lng_ref, lnb_ref, ws_ref, bs_ref, wdw_ref, bdw_ref,
                   gng_ref, gnb_ref, ya_ref, q_ref, k_ref, v_ref, yc_ref, hist_ref, conv_ref, shift_ref,
                   *, a_width, b_width, c_width):
    tm = x_ref.shape[0]

    @pl.when(pl.program_id(1) == 0)
    def _():
        hist_ref[0:HALO, :] = jnp.zeros((HALO, c_width), jnp.float32)

    h = _rms(x_ref[...], g_ref[...]).astype(jnp.bfloat16)

    def project(lo, hi):
        return jnp.dot(h, win_ref[:, lo:hi], preferred_element_type=jnp.float32)

    o = 2 * a_width + 3 * b_width
    zc = project(o, o + 2 * c_width)
    hist_ref[HALO:HALO + tm, :] = zc[:, :c_width] * (1.0 / (1.0 + jnp.exp(-zc[:, c_width:])))

    o = 2 * a_width
    q_ref[...] = (project(o, o + b_width) * (HEAD_DIM ** -0.5 * LOG2_E)).astype(jnp.bfloat16)
    k_ref[...] = project(o + b_width, o + 2 * b_width).astype(jnp.bfloat16)
    v_ref[...] = project(o + 2 * b_width, o + 3 * b_width).astype(jnp.bfloat16)
    za = project(0, 2 * a_width)

    first = HALO - (CONV_WIDTH - 1)
    for c in range(tm // CONV_ROWS):
        r0 = c * CONV_ROWS
        acc = jnp.zeros((CONV_ROWS, c_width), jnp.float32)
        for phase in range(SUBLANES):
            taps = [j for j in range(CONV_WIDTH) if (first + j) % SUBLANES == phase]
            span = first + taps[-1] - phase
            rows = CONV_ROWS + span
            shift_ref[0:rows, :] = hist_ref[r0 + phase:r0 + phase + rows, :]
            for j in taps:
                off = first + j - phase
                acc = acc + wdw_ref[j:j + 1, :] * shift_ref[off:off + CONV_ROWS, :]
        conv_ref[r0:r0 + CONV_ROWS, :] = acc + bdw_ref[...]
    hist_ref[0:HALO, :] = hist_ref[tm:tm + HALO, :]

    ga = 0.5 * za * (1.0 + lax.erf(za * (2.0 ** -0.5)))
    u = ga[:, :a_width]
    vv = ga[:, a_width:]
    mu = jnp.mean(vv, axis=-1, keepdims=True)
    vc = vv - mu
    var = jnp.mean(vc * vc, axis=-1, keepdims=True)
    vn = (vc * lax.rsqrt(var + EPS) * lng_ref[...] + lnb_ref[...]).astype(jnp.bfloat16)
    n_heads = a_width // HEAD_DIM
    t_idx = lax.broadcasted_iota(jnp.int32, (CHUNK, n_heads * CHUNK), 0)
    s_idx = lax.broadcasted_iota(jnp.int32, (CHUNK, n_heads * CHUNK), 1) % CHUNK
    w_tril = jnp.where(s_idx <= t_idx, ws_ref[...], 0.0).astype(jnp.bfloat16)
    r_head = lax.broadcasted_iota(jnp.int32, (n_heads * CHUNK, a_width), 0) // CHUNK
    c_head = lax.broadcasted_iota(jnp.int32, (n_heads * CHUNK, a_width), 1) // HEAD_DIM
    head_match = r_head == c_head
    for c in range(tm // CHUNK):
        rows = slice(c * CHUNK, (c + 1) * CHUNK)
        v_chunk = vn[rows]
        v_bd = jnp.where(head_match, jnp.concatenate([v_chunk] * n_heads, axis=0), 0.0)
        mixed = jnp.dot(w_tril, v_bd.astype(jnp.bfloat16), preferred_element_type=jnp.float32)
        ya_ref[rows, :] = (u[rows] * (mixed + bs_ref[...])).astype(jnp.bfloat16)

    gr = lax.broadcasted_iota(jnp.int32, (2 * c_width, c_width), 0) % c_width // HEAD_DIM
    gc = lax.broadcasted_iota(jnp.int32, (2 * c_width, c_width), 1) // HEAD_DIM
    gmat2 = jnp.where(gr == gc, 1.0 / HEAD_DIM, 0.0).astype(jnp.bfloat16)

    def group_mean(a):
        hi = a.astype(jnp.bfloat16)
        lo = (a - hi.astype(jnp.float32)).astype(jnp.bfloat16)
        return jnp.dot(jnp.concatenate([hi, lo], axis=-1), gmat2, preferred_element_type=jnp.float32)

    y = conv_ref[...]
    yd = y - group_mean(y)
    yn = yd * lax.rsqrt(group_mean(yd * yd) + EPS) * gng_ref[...] + gnb_ref[...]
    yc_ref[...] = (yn * (1.0 / (1.0 + jnp.exp(-yn)))).astype(jnp.bfloat16)


def _mix_in(x, l, pre_g, w_in, ln_g, ln_b, ws_cat, bs_x, w_dw, b_dw, gn_g, gn_b,
            *, a_width, b_width, c_width):
    bn, s, d = x.shape
    tm = min(TOKEN_TILE, s)
    in_cols = w_in.shape[-1]
    n_heads = a_width // HEAD_DIM
    lsel3 = lambda b, i: (l, 0, 0)
    tok = lambda w: pl.BlockSpec((None, tm, w), lambda b, i: (b, i, 0))
    bf = lambda w: jax.ShapeDtypeStruct((bn, s, w), jnp.bfloat16)
    return pl.pallas_call(
        functools.partial(_mix_in_kernel, a_width=a_width, b_width=b_width, c_width=c_width),
        grid=(bn, s // tm),
        in_specs=[
            tok(d),
            pl.BlockSpec((None, 1, d), lsel3),
            _const_spec((None, d, in_cols), lsel3),
            pl.BlockSpec((None, 1, a_width), lsel3),
            pl.BlockSpec((None, 1, a_width), lsel3),
            pl.BlockSpec((None, CHUNK, n_heads * CHUNK), lsel3),
            pl.BlockSpec((None, CHUNK, a_width), lsel3),
            pl.BlockSpec((None, CONV_WIDTH, c_width), lsel3),
            pl.BlockSpec((None, 1, c_width), lsel3),
            pl.BlockSpec((None, 1, c_width), lsel3),
            pl.BlockSpec((None, 1, c_width), lsel3),
        ],
        out_specs=[tok(a_width), tok(b_width), tok(b_width), tok(b_width), tok(c_width)],
        out_shape=[bf(a_width), bf(b_width), bf(b_width), bf(b_width), bf(c_width)],
        scratch_shapes=[pltpu.VMEM((tm + HALO, c_width), jnp.float32),
                        pltpu.VMEM((tm, c_width), jnp.float32),
                        pltpu.VMEM((CONV_ROWS + HALO, c_width), jnp.float32)],
        compiler_params=pltpu.CompilerParams(
            dimension_semantics=("arbitrary", "arbitrary"), vmem_limit_bytes=VMEM_LIMIT),
        name="mix_in",
    )(x, pre_g, w_in, ln_g, ln_b, ws_cat, bs_x, w_dw, b_dw, gn_g, gn_b)


ALIBI_TERMS = 4
ALIBI_PIECES = 3
ALIBI_ROWS = 16
HEADS_PER_TILE = LANES // HEAD_DIM
BF16_SUBLANES = 16
PV_ROWS = HEAD_DIM + BF16_SUBLANES
GATE_PIECES = 3


def _moba_kernel(slopes_ref, q_ref, k_ref, v_ref, o_ref, km_ref, kaug_ref, vt_ref, neg_ref,
                 gate_ref, s_ref, p_ref, acc_ref, *, n_blocks):
    blk = MOBA_BLOCK
    n_tiles = q_ref.shape[-1] // LANES
    n_heads = n_tiles * HEADS_PER_TILE
    hg = pl.program_id(1)
    qi = pl.program_id(2)

    def own_lanes(h, idx):
        hh = h % HEADS_PER_TILE
        return (idx >= hh * HEAD_DIM) & (idx < (hh + 1) * HEAD_DIM)

    def slot_base(h):
        return ((h + 1) % HEADS_PER_TILE) * HEAD_DIM

    @pl.when(qi == 0)
    def _():
        lane = lax.broadcasted_iota(jnp.int32, (blk, LANES), 1)
        j_pos = lax.broadcasted_iota(jnp.int32, (blk, LANES), 0).astype(jnp.float32)
        ones_rows = (lax.broadcasted_iota(jnp.int32, (BF16_SUBLANES, blk), 0) == 0
                     ).astype(jnp.float32)
        km_row = lax.broadcasted_iota(jnp.int32, (n_blocks, LANES), 0)
        km_lane = lax.broadcasted_iota(jnp.int32, (n_blocks, LANES), 1)
        for t in range(n_tiles):
            cols = slice(t * LANES, (t + 1) * LANES)
            kmean = jnp.zeros((n_blocks, LANES), jnp.float32)
            for n in range(n_blocks):
                rows = slice(n * blk, (n + 1) * blk)
                kf = k_ref[rows, cols].astype(jnp.float32)
                kmean = jnp.where(km_row == n, jnp.sum(kf, axis=0, keepdims=True) * (1.0 / blk), kmean)
                v_t = v_ref[rows, cols].astype(jnp.float32).T
                for hh in range(HEADS_PER_TILE):
                    h = t * HEADS_PER_TILE + hh
                    term = (lane - slot_base(h)) // ALIBI_PIECES
                    aug = jnp.where(term == 0, j_pos,
                                    jnp.where(term == 1, float(n),
                                              jnp.where((term == 2) | (term == 3), 1.0, 0.0)))
                    kaug_ref[h, n] = jnp.where(own_lanes(h, lane), kf, aug).astype(jnp.bfloat16)
                    vt_ref[h, n] = jnp.concatenate(
                        [v_t[hh * HEAD_DIM:(hh + 1) * HEAD_DIM], ones_rows], axis=0).astype(jnp.bfloat16)
            for hh in range(HEADS_PER_TILE):
                h = t * HEADS_PER_TILE + hh
                rest = jnp.where(own_lanes(h, km_lane), kmean, 0.0)
                pieces = []
                for _ in range(GATE_PIECES):
                    piece = rest.astype(jnp.bfloat16).astype(jnp.float32)
                    pieces.append(piece)
                    rest = rest - piece
                km_ref[h] = jnp.concatenate(pieces[::-1], axis=0).astype(jnp.bfloat16)

    slot = lax.broadcasted_iota(jnp.int32, (ALIBI_ROWS, blk), 0)
    i_pos = lax.broadcasted_iota(jnp.int32, (ALIBI_ROWS, blk), 1).astype(jnp.float32)
    qi_f = qi.astype(jnp.float32)
    pad = jnp.zeros((HEAD_DIM - ALIBI_ROWS, blk), jnp.float32)
    q_aug = []
    for t in range(n_tiles):
        q_t = q_ref[:, t * LANES:(t + 1) * LANES].astype(jnp.float32).T
        for hh in range(HEADS_PER_TILE):
            h = t * HEADS_PER_TILE + hh
            c = slopes_ref[hg * n_heads + h] * LOG2_E
            term = slot // ALIBI_PIECES
            full = jnp.where(term == 0, c,
                             jnp.where(term == 1, c * blk,
                                       jnp.where(term == 2, -(c * (blk * qi_f)),
                                                 jnp.where(term == 3, -(c * i_pos), 0.0))))
            piece = full.astype(jnp.bfloat16).astype(jnp.float32)
            aug = piece
            for r in range(1, ALIBI_PIECES):
                full = full - piece
                piece = full.astype(jnp.bfloat16).astype(jnp.float32)
                aug = jnp.where(slot % ALIBI_PIECES == r, piece, aug)
            own = q_t[hh * HEAD_DIM:(hh + 1) * HEAD_DIM]
            rows = [own, aug, pad] if slot_base(h) > 0 else [aug, pad, own]
            q_aug.append(jnp.concatenate(rows, axis=0).astype(jnp.bfloat16))

    def block_of(u):
        return jnp.where(u == 0, qi, jnp.minimum(u - 1, n_blocks - 1))

    key_i = lax.broadcasted_iota(jnp.int32, (blk, blk), 0)
    qry_i = lax.broadcasted_iota(jnp.int32, (blk, blk), 1)
    causal = key_i <= qry_i
    for h in range(n_heads):
        s = jnp.dot(kaug_ref[h, qi], q_aug[h], preferred_element_type=jnp.float32)
        s_ref[0, h] = jnp.where(causal, s, NEG_INF)
        p_ref[1, h] = jnp.zeros((blk, blk), jnp.bfloat16)
        acc_ref[h] = jnp.zeros((PV_ROWS, blk), jnp.float32)

    blk_id = lax.broadcasted_iota(jnp.int32, (n_blocks, blk), 0)
    for h in range(n_heads):
        g3 = jnp.dot(km_ref[h], q_aug[h], preferred_element_type=jnp.float32)
        gate = g3[:n_blocks] + g3[n_blocks:2 * n_blocks] + g3[2 * n_blocks:]
        gate = jnp.where(blk_id < qi, gate, NEG_INF)
        gate_ref[h] = gate
        ranks = [jnp.zeros((n_blocks, blk), jnp.float32) for _ in range(4)]
        for m in range(n_blocks - 1):
            gm = jnp.broadcast_to(gate_ref[h, m:m + 1, :], (n_blocks, blk))
            ahead = (gm > gate) | ((gm == gate) & (blk_id > m))
            ranks[m % 4] = ranks[m % 4] + jnp.where(ahead, 1.0, 0.0)
        rank = (ranks[0] + ranks[1]) + (ranks[2] + ranks[3])
        neg = jnp.where((rank < MOBA_TOPK) & (blk_id < qi), 0.0, NEG_INF)
        neg_ref[h, 0] = jnp.zeros((1, blk), jnp.float32)
        for u in range(1, n_blocks):
            neg_ref[h, u] = neg[u - 1:u, :]

    def step(u, cur, m_runs):
        b_prev = block_of(jnp.maximum(u - 1, 0))
        k_next = jnp.minimum(u, n_blocks - 1)
        pv = [jnp.dot(vt_ref[h, b_prev], p_ref[1 - cur, h], preferred_element_type=jnp.float32)
              for h in range(n_heads)]
        for h in range(n_heads):
            s_ref[1 - cur, h] = jnp.dot(kaug_ref[h, k_next], q_aug[h],
                                        preferred_element_type=jnp.float32)
        m_news = []
        for h in range(n_heads):
            s = s_ref[cur, h]
            neg = neg_ref[h, u]
            m_new = jnp.maximum(m_runs[h], jnp.max(s, axis=0, keepdims=True) + neg)
            alpha = jnp.exp2(m_runs[h] - m_new)
            p = jnp.exp2(s - (m_new - neg))
            p_ref[cur, h] = p.astype(jnp.bfloat16)
            acc_ref[h] = alpha * (acc_ref[h] + pv[h])
            m_news.append(m_new)
        return tuple(m_news)

    def step_pair(i, m_runs):
        return step(2 * i + 1, 1, step(2 * i, 0, m_runs))

    m_init = jnp.full((1, blk), NEG_INF, jnp.float32)
    lax.fori_loop(0, (qi + 2) // 2, step_pair, (m_init,) * n_heads)
    last = 2 * ((qi + 2) // 2) - 1
    for t in range(n_tiles):
        outs = []
        for h in range(t * HEADS_PER_TILE, (t + 1) * HEADS_PER_TILE):
            acc = acc_ref[h] + jnp.dot(vt_ref[h, block_of(last)], p_ref[1, h],
                                       preferred_element_type=jnp.float32)
            outs.append(acc[:HEAD_DIM] / acc[HEAD_DIM:HEAD_DIM + 1])
        o_ref[:, t * LANES:(t + 1) * LANES] = jnp.concatenate(outs, axis=0).T.astype(jnp.bfloat16)


MOBA_TILES_PER_STEP = 2


def _moba(q, k, v, slopes):
    bn, s, width = q.shape
    step_w = MOBA_TILES_PER_STEP * LANES
    assert s % MOBA_BLOCK == 0 and width % step_w == 0
    n_blocks = s // MOBA_BLOCK
    n_heads = MOBA_TILES_PER_STEP * HEADS_PER_TILE
    assert ALIBI_TERMS * ALIBI_PIECES <= ALIBI_ROWS <= HEAD_DIM and HEADS_PER_TILE == 2
    assert n_blocks <= 256 and MOBA_BLOCK <= 256
    qspec = pl.BlockSpec((None, MOBA_BLOCK, step_w), lambda b, hg, qi: (b, qi, hg))
    kvspec = pl.BlockSpec((None, s, step_w), lambda b, hg, qi: (b, 0, hg))
    return pl.pallas_call(
        functools.partial(_moba_kernel, n_blocks=n_blocks),
        grid=(bn, width // step_w, n_blocks),
        in_specs=[pl.BlockSpec(memory_space=pltpu.SMEM), qspec, kvspec, kvspec],
        out_specs=qspec,
        out_shape=jax.ShapeDtypeStruct((bn, s, width), jnp.bfloat16),
        scratch_shapes=[
            pltpu.VMEM((n_heads, GATE_PIECES * n_blocks, LANES), jnp.bfloat16),
            pltpu.VMEM((n_heads, n_blocks, MOBA_BLOCK, LANES), jnp.bfloat16),
            pltpu.VMEM((n_heads, n_blocks, PV_ROWS, MOBA_BLOCK), jnp.bfloat16),
            pltpu.VMEM((n_heads, n_blocks, 1, MOBA_BLOCK), jnp.float32),
            pltpu.VMEM((n_heads, n_blocks, MOBA_BLOCK), jnp.float32),
            pltpu.VMEM((2, n_heads, MOBA_BLOCK, MOBA_BLOCK), jnp.float32),
            pltpu.VMEM((2, n_heads, MOBA_BLOCK, MOBA_BLOCK), jnp.bfloat16),
            pltpu.VMEM((n_heads, PV_ROWS, MOBA_BLOCK), jnp.float32),
        ],
        compiler_params=pltpu.CompilerParams(
            dimension_semantics=("arbitrary", "arbitrary", "arbitrary"),
            vmem_limit_bytes=VMEM_LIMIT),
        name="moba",
    )(slopes, q, k, v)


def _mix_out_kernel(x_ref, ya_ref, yb_ref, yc_ref, wout_ref, gpost_ref, gprex_ref, wq_ref,
                    kx_ref, vx_ref, wo_ref, gpostx_ref, o_ref):
    d = x_ref.shape[-1]
    dh = d // X_HEADS
    y = jnp.concatenate([ya_ref[...], yb_ref[...], yc_ref[...]], axis=-1)
    mix = jnp.dot(y, wout_ref[...], preferred_element_type=jnp.float32)
    x1 = x_ref[...] + _rms(mix, gpost_ref[...])

    h = _rms(x1, gprex_ref[...]).astype(jnp.bfloat16)
    qx = (jnp.dot(h, wq_ref[...], preferred_element_type=jnp.float32) * (dh ** -0.5)
          ).astype(jnp.bfloat16)
    nt = (((1,), (1,)), ((), ()))
    heads = []
    for hh in range(X_HEADS):
        cols = slice(hh * dh, (hh + 1) * dh)
        s = lax.dot_general(qx[:, cols], kx_ref[:, cols], nt, preferred_element_type=jnp.float32)
        p = jnp.exp(s - jnp.max(s, axis=-1, keepdims=True))
        denom = jnp.sum(p, axis=-1, keepdims=True)
        oh = jnp.dot(p.astype(jnp.bfloat16), vx_ref[:, cols], preferred_element_type=jnp.float32)
        heads.append((oh / denom).astype(jnp.bfloat16))
    att = jnp.dot(jnp.concatenate(heads, axis=-1), wo_ref[...], preferred_element_type=jnp.float32)
    o_ref[...] = x1 + _rms(att, gpostx_ref[...])


def _mix_out(x, ya, yb, yc, l, w_out, g_post, g_prex, w_xq, kx, vx, w_xo, g_postx):
    bn, s, d = x.shape
    tm = min(TOKEN_TILE, s)
    m = kx.shape[2]
    lsel3 = lambda b, i: (l, 0, 0)
    tok = lambda w: pl.BlockSpec((None, tm, w), lambda b, i: (b, i, 0))
    gain = pl.BlockSpec((None, 1, d), lsel3)
    wsq = _const_spec((None, d, d), lsel3)
    mem = pl.BlockSpec((None, None, m, d), lambda b, i: (l, b, 0, 0))
    return pl.pallas_call(
        _mix_out_kernel,
        grid=(bn, s // tm),
        in_specs=[tok(d), tok(ya.shape[-1]), tok(yb.shape[-1]), tok(yc.shape[-1]),
                  wsq, gain, gain, wsq, mem, mem, wsq, gain],
        out_specs=tok(d),
        out_shape=jax.ShapeDtypeStruct((bn, s, d), jnp.float32),
        compiler_params=pltpu.CompilerParams(
            dimension_semantics=("arbitrary", "arbitrary"), vmem_limit_bytes=VMEM_LIMIT),
        name="mix_out",
    )(x, ya, yb, yc, w_out, g_post, g_prex, w_xq, kx, vx, w_xo, g_postx)


def _ffn_kernel(x_ref, gpre_ref, w1_ref, w2_ref, gpost_ref, o_ref):
    x = x_ref[...]
    h = _rms(x, gpre_ref[...]).astype(jnp.bfloat16)
    a = jnp.maximum(jnp.dot(h, w1_ref[...], preferred_element_type=jnp.float32), 0.0)
    f = jnp.dot((a * a).astype(jnp.bfloat16), w2_ref[...], preferred_element_type=jnp.float32)
    o_ref[...] = x + _rms(f, gpost_ref[...])


def _ffn(x, l, g_pre, w1, w2, g_post):
    bn, s, d = x.shape
    tm = min(TOKEN_TILE, s)
    dff = w1.shape[-1]
    lsel3 = lambda b, i: (l, 0, 0)
    tok = pl.BlockSpec((None, tm, d), lambda b, i: (b, i, 0))
    gain = pl.BlockSpec((None, 1, d), lsel3)
    return pl.pallas_call(
        _ffn_kernel,
        grid=(bn, s // tm),
        in_specs=[tok, gain, _const_spec((None, d, dff), lsel3),
                  _const_spec((None, dff, d), lsel3), gain],
        out_specs=tok,
        out_shape=jax.ShapeDtypeStruct((bn, s, d), jnp.float32),
        compiler_params=pltpu.CompilerParams(
            dimension_semantics=("arbitrary", "arbitrary"), vmem_limit_bytes=VMEM_LIMIT),
        name="ffn",
    )(x, g_pre, w1, w2, g_post)


def kernel(x, mem, pre_mix_g, w_in, gate_ln_g, gate_ln_b, w_s, b_s, w_dw, b_dw, conv_gn_g, conv_gn_b, w_out, post_mix_g, pre_x_g, mem_g, w_xq, w_xkv, w_xo, post_x_g, pre_ffn_g, w_ff1, w_ff2, post_ffn_g):
    depth, d, _ = w_in.shape
    a_width = gate_ln_g.shape[-1]
    c_width = w_dw.shape[-1]
    b_width = d - a_width - c_width
    a_heads = a_width // HEAD_DIM
    b_heads = b_width // HEAD_DIM
    assert w_in.shape[-1] == 2 * a_width + 3 * b_width + 2 * c_width
    assert w_s.shape[1:] == (a_heads, CHUNK, CHUNK) and w_dw.shape[1] == CONV_WIDTH
    assert x.shape[1] % MOBA_BLOCK == 0 and x.shape[1] % TOKEN_TILE == 0

    bf16 = lambda w: w.astype(jnp.bfloat16)
    row = lambda g: g[:, None, :]
    ws_cat = w_s.transpose(0, 2, 1, 3).reshape(depth, CHUNK, a_heads * CHUNK)
    bs_x = jnp.repeat(b_s.transpose(0, 2, 1), HEAD_DIM, axis=-1)
    slopes = 2.0 ** (-8.0 * jnp.arange(1, b_heads + 1, dtype=jnp.float32) / b_heads)
    w_in_b, w_out_b, w_xq_b, w_xkv_b, w_xo_b, w_ff1_b, w_ff2_b = map(
        bf16, (w_in, w_out, w_xq, w_xkv, w_xo, w_ff1, w_ff2))

    kx, vx = _mem_kv(mem, row(mem_g), w_xkv_b)
    for l in range(depth):
        ya, q, k, v, yc = _mix_in(
            x, l, row(pre_mix_g), w_in_b, row(gate_ln_g), row(gate_ln_b), ws_cat, bs_x, w_dw,
            row(b_dw), row(conv_gn_g), row(conv_gn_b),
            a_width=a_width, b_width=b_width, c_width=c_width)
        yb = _moba(q, k, v, slopes)
        x = _mix_out(x, ya, yb, yc, l, w_out_b, row(post_mix_g), row(pre_x_g), w_xq_b, kx, vx,
                     w_xo_b, row(post_x_g))
        x = _ffn(x, l, row(pre_ffn_g), w_ff1_b, w_ff2_b, row(post_ffn_g))
    return x
```

```python
import functools
import math

import jax
import jax.numpy as jnp
from jax import lax
from jax.experimental import pallas as pl
from jax.experimental.pallas import tpu as pltpu

HEAD_DIM = 64
CHUNK = 128
MOBA_BLOCK = 256
MOBA_TOPK = 3
CONV_WIDTH = 31
X_HEADS = 4
EPS = 1e-6
LOG2_E = math.log2(math.e)

LANES = 128
SUBLANES = 8
HALO = 32
CONV_ROWS = 128
TOKEN_TILE = 512
ROW_GROUPS = 2
VMEM_LIMIT = 56 * 1024 * 1024

NEG_INF = float("-inf")


def _rms(x, g):
    return x * lax.rsqrt(jnp.mean(x * x, axis=-1, keepdims=True) + EPS) * g


def _const_spec(shape, index_map):
    return pl.BlockSpec(shape, index_map, pipeline_mode=pl.Buffered(1))


def _mem_kv_kernel(mem_ref, g_ref, w_ref, k_ref, v_ref):
    d = mem_ref.shape[-1]
    m = _rms(mem_ref[...], g_ref[...]).astype(jnp.bfloat16)
    kv = jnp.dot(m, w_ref[...], preferred_element_type=jnp.float32)
    k_ref[...] = kv[:, :d].astype(jnp.bfloat16)
    v_ref[...] = kv[:, d:].astype(jnp.bfloat16)


def _mem_kv(mem, mem_g, w_xkv):
    bn, m, d = mem.shape
    depth = w_xkv.shape[0]
    out = jax.ShapeDtypeStruct((depth, bn, m, d), jnp.bfloat16)
    return pl.pallas_call(
        _mem_kv_kernel,
        grid=(depth, bn),
        in_specs=[
            pl.BlockSpec((None, m, d), lambda l, b: (b, 0, 0)),
            pl.BlockSpec((None, 1, d), lambda l, b: (l, 0, 0)),
            pl.BlockSpec((None, d, 2 * d), lambda l, b: (l, 0, 0)),
        ],
        out_specs=[
            pl.BlockSpec((None, None, m, d), lambda l, b: (l, b, 0, 0)),
            pl.BlockSpec((None, None, m, d), lambda l, b: (l, b, 0, 0)),
        ],
        out_shape=[out, out],
        compiler_params=pltpu.CompilerParams(
            dimension_semantics=("arbitrary", "arbitrary"), vmem_limit_bytes=VMEM_LIMIT),
        name="mem_kv",
    )(mem, mem_g, w_xkv)


def _mix_in_kernel(x_ref, g_ref, win_ref, lng_ref, lnb_ref, ws_ref, bs_ref, wdw_ref, bdw_ref,
                   gng_ref, gnb_ref, ya_ref, q_ref, k_ref, v_ref, yc_ref, hist_ref, conv_ref, shift_ref,
                   *, a_width, b_width, c_width):
    tm = x_ref.shape[0]

    @pl.when(pl.program_id(1) == 0)
    def _():
        hist_ref[0:HALO, :] = jnp.zeros((HALO, c_width), jnp.float32)

    h = _rms(x_ref[...], g_ref[...]).astype(jnp.bfloat16)

    def project(lo, hi):
        return jnp.dot(h, win_ref[:, lo:hi], preferred_element_type=jnp.float32)

    o = 2 * a_width + 3 * b_width
    zc = project(o, o + 2 * c_width)
    hist_ref[HALO:HALO + tm, :] = zc[:, :c_width] * (1.0 / (1.0 + jnp.exp(-zc[:, c_width:])))

    o = 2 * a_width
    q = (project(o, o + b_width) * (HEAD_DIM ** -0.5 * LOG2_E)).astype(jnp.bfloat16)
    k = project(o + b_width, o + 2 * b_width).astype(jnp.bfloat16)
    v = project(o + 2 * b_width, o + 3 * b_width).astype(jnp.bfloat16)
    za = project(0, 2 * a_width)

    first = HALO - (CONV_WIDTH - 1)
    for c in range(tm // CONV_ROWS):
        r0 = c * CONV_ROWS
        acc = jnp.zeros((CONV_ROWS, c_width), jnp.float32)
        for phase in range(SUBLANES):
            taps = [j for j in range(CONV_WIDTH) if (first + j) % SUBLANES == phase]
            span = first + taps[-1] - phase
            rows = CONV_ROWS + span
            shift_ref[phase, 0:rows, :] = hist_ref[r0 + phase:r0 + phase + rows, :]
            for j in taps:
                off = first + j - phase
                acc = acc + wdw_ref[j:j + 1, :] * shift_ref[phase, off:off + CONV_ROWS, :]
        conv_ref[r0:r0 + CONV_ROWS, :] = acc + bdw_ref[...]
    hist_ref[0:HALO, :] = hist_ref[tm:tm + HALO, :]
    q_ref[...] = q
    k_ref[...] = k
    v_ref[...] = v

    ga = 0.5 * za * (1.0 + lax.erf(za * (2.0 ** -0.5)))
    u = ga[:, :a_width]
    vv = ga[:, a_width:]
    mu = jnp.mean(vv, axis=-1, keepdims=True)
    vc = vv - mu
    var = jnp.mean(vc * vc, axis=-1, keepdims=True)
    vn = (vc * lax.rsqrt(var + EPS) * lng_ref[...] + lnb_ref[...]).astype(jnp.bfloat16)
    n_heads = a_width // HEAD_DIM
    t_idx = lax.broadcasted_iota(jnp.int32, (CHUNK, n_heads * CHUNK), 0)
    s_idx = lax.broadcasted_iota(jnp.int32, (CHUNK, n_heads * CHUNK), 1) % CHUNK
    w_tril = jnp.where(s_idx <= t_idx, ws_ref[...], 0.0).astype(jnp.bfloat16)
    r_head = lax.broadcasted_iota(jnp.int32, (n_heads * CHUNK, a_width), 0) // CHUNK
    c_head = lax.broadcasted_iota(jnp.int32, (n_heads * CHUNK, a_width), 1) // HEAD_DIM
    head_match = r_head == c_head
    for c in range(tm // CHUNK):
        rows = slice(c * CHUNK, (c + 1) * CHUNK)
        v_chunk = vn[rows]
        v_bd = jnp.where(head_match, jnp.concatenate([v_chunk] * n_heads, axis=0), 0.0)
        mixed = jnp.dot(w_tril, v_bd.astype(jnp.bfloat16), preferred_element_type=jnp.float32)
        ya_ref[rows, :] = (u[rows] * (mixed + bs_ref[...])).astype(jnp.bfloat16)

    gr = lax.broadcasted_iota(jnp.int32, (2 * c_width, c_width), 0) % c_width // HEAD_DIM
    gc = lax.broadcasted_iota(jnp.int32, (2 * c_width, c_width), 1) // HEAD_DIM
    gmat2 = jnp.where(gr == gc, 1.0 / HEAD_DIM, 0.0).astype(jnp.bfloat16)

    def group_mean(a):
        hi = a.astype(jnp.bfloat16)
        lo = (a - hi.astype(jnp.float32)).astype(jnp.bfloat16)
        return jnp.dot(jnp.concatenate([hi, lo], axis=-1), gmat2, preferred_element_type=jnp.float32)

    y = conv_ref[...]
    yd = y - group_mean(y)
    yn = yd * lax.rsqrt(group_mean(yd * yd) + EPS) * gng_ref[...] + gnb_ref[...]
    yc_ref[...] = (yn * (1.0 / (1.0 + jnp.exp(-yn)))).astype(jnp.bfloat16)


def _mix_in(x, l, pre_g, w_in, ln_g, ln_b, ws_cat, bs_x, w_dw, b_dw, gn_g, gn_b,
            *, a_width, b_width, c_width):
    bn, s, d = x.shape
    tm = min(TOKEN_TILE, s)
    in_cols = w_in.shape[-1]
    n_heads = a_width // HEAD_DIM
    lsel3 = lambda b, i: (l, 0, 0)
    tok = lambda w: pl.BlockSpec((None, tm, w), lambda b, i: (b, i, 0))
    bf = lambda w: jax.ShapeDtypeStruct((bn, s, w), jnp.bfloat16)
    return pl.pallas_call(
        functools.partial(_mix_in_kernel, a_width=a_width, b_width=b_width, c_width=c_width),
        grid=(bn, s // tm),
        in_specs=[
            tok(d),
            pl.BlockSpec((None, 1, d), lsel3),
            _const_spec((None, d, in_cols), lsel3),
            pl.BlockSpec((None, 1, a_width), lsel3),
            pl.BlockSpec((None, 1, a_width), lsel3),
            pl.BlockSpec((None, CHUNK, n_heads * CHUNK), lsel3),
            pl.BlockSpec((None, CHUNK, a_width), lsel3),
            pl.BlockSpec((None, CONV_WIDTH, c_width), lsel3),
            pl.BlockSpec((None, 1, c_width), lsel3),
            pl.BlockSpec((None, 1, c_width), lsel3),
            pl.BlockSpec((None, 1, c_width), lsel3),
        ],
        out_specs=[tok(a_width), tok(b_width), tok(b_width), tok(b_width), tok(c_width)],
        out_shape=[bf(a_width), bf(b_width), bf(b_width), bf(b_width), bf(c_width)],
        scratch_shapes=[pltpu.VMEM((tm + HALO, c_width), jnp.float32),
                        pltpu.VMEM((tm, c_width), jnp.float32),
                        pltpu.VMEM((SUBLANES, CONV_ROWS + HALO, c_width), jnp.float32)],
        compiler_params=pltpu.CompilerParams(
            dimension_semantics=("arbitrary", "arbitrary"), vmem_limit_bytes=VMEM_LIMIT),
        name="mix_in",
    )(x, pre_g, w_in, ln_g, ln_b, ws_cat, bs_x, w_dw, b_dw, gn_g, gn_b)


ALIBI_TERMS = 4
ALIBI_PIECES = 3
ALIBI_ROWS = 16
HEADS_PER_TILE = LANES // HEAD_DIM
BF16_SUBLANES = 16
PV_ROWS = HEAD_DIM + BF16_SUBLANES
GATE_PIECES = 3

def _moba_kernel(slopes_ref, q_ref, k_ref, v_ref, o_ref, km_ref, kaug_ref, vt_ref, neg_ref,
                 gate_ref, s_ref, p_ref, acc_ref, *, n_blocks):
    blk = MOBA_BLOCK
    n_tiles = q_ref.shape[-1] // LANES
    n_heads = n_tiles * HEADS_PER_TILE
    hg = pl.program_id(1)
    qi = pl.program_id(2)

    def own_lanes(h, idx):
        hh = h % HEADS_PER_TILE
        return (idx >= hh * HEAD_DIM) & (idx < (hh + 1) * HEAD_DIM)

    def slot_base(h):
        return ((h + 1) % HEADS_PER_TILE) * HEAD_DIM

    @pl.when(qi == 0)
    def _():
        lane = lax.broadcasted_iota(jnp.int32, (blk, LANES), 1)
        j_pos = lax.broadcasted_iota(jnp.int32, (blk, LANES), 0).astype(jnp.float32)
        ones_rows = (lax.broadcasted_iota(jnp.int32, (BF16_SUBLANES, blk), 0) == 0
                     ).astype(jnp.float32)
        km_row = lax.broadcasted_iota(jnp.int32, (n_blocks, LANES), 0)
        km_lane = lax.broadcasted_iota(jnp.int32, (n_blocks, LANES), 1)
        for t in range(n_tiles):
            cols = slice(t * LANES, (t + 1) * LANES)
            kmean = jnp.zeros((n_blocks, LANES), jnp.float32)
            for n in range(n_blocks):
                rows = slice(n * blk, (n + 1) * blk)
                kf = k_ref[rows, cols].astype(jnp.float32)
                kmean = jnp.where(km_row == n, jnp.sum(kf, axis=0, keepdims=True) * (1.0 / blk), kmean)
                v_t = v_ref[rows, cols].astype(jnp.float32).T
                for hh in range(HEADS_PER_TILE):
                    h = t * HEADS_PER_TILE + hh
                    term = (lane - slot_base(h)) // ALIBI_PIECES
                    aug = jnp.where(term == 0, j_pos,
                                    jnp.where(term == 1, float(n),
                                              jnp.where((term == 2) | (term == 3), 1.0, 0.0)))
                    kaug_ref[h, n] = jnp.where(own_lanes(h, lane), kf, aug).astype(jnp.bfloat16)
                    vt_ref[h, n] = jnp.concatenate(
                        [v_t[hh * HEAD_DIM:(hh + 1) * HEAD_DIM], ones_rows], axis=0).astype(jnp.bfloat16)
            for hh in range(HEADS_PER_TILE):
                h = t * HEADS_PER_TILE + hh
                rest = jnp.where(own_lanes(h, km_lane), kmean, 0.0)
                pieces = []
                for _ in range(GATE_PIECES):
                    piece = rest.astype(jnp.bfloat16).astype(jnp.float32)
                    pieces.append(piece)
                    rest = rest - piece
                km_ref[h] = jnp.concatenate(pieces[::-1], axis=0).astype(jnp.bfloat16)

    slot = lax.broadcasted_iota(jnp.int32, (ALIBI_ROWS, blk), 0)
    i_pos = lax.broadcasted_iota(jnp.int32, (ALIBI_ROWS, blk), 1).astype(jnp.float32)
    qi_f = qi.astype(jnp.float32)
    pad = jnp.zeros((HEAD_DIM - ALIBI_ROWS, blk), jnp.float32)
    q_aug = []
    for t in range(n_tiles):
        q_t = q_ref[:, t * LANES:(t + 1) * LANES].astype(jnp.float32).T
        for hh in range(HEADS_PER_TILE):
            h = t * HEADS_PER_TILE + hh
            c = slopes_ref[hg * n_heads + h] * LOG2_E
            term = slot // ALIBI_PIECES
            full = jnp.where(term == 0, c,
                             jnp.where(term == 1, c * blk,
                                       jnp.where(term == 2, -(c * (blk * qi_f)),
                                                 jnp.where(term == 3, -(c * i_pos), 0.0))))
            piece = full.astype(jnp.bfloat16).astype(jnp.float32)
            aug = piece
            for r in range(1, ALIBI_PIECES):
                full = full - piece
                piece = full.astype(jnp.bfloat16).astype(jnp.float32)
                aug = jnp.where(slot % ALIBI_PIECES == r, piece, aug)
            own = q_t[hh * HEAD_DIM:(hh + 1) * HEAD_DIM]
            rows = [own, aug, pad] if slot_base(h) > 0 else [aug, pad, own]
            q_aug.append(jnp.concatenate(rows, axis=0).astype(jnp.bfloat16))

    def block_of(u):
        return jnp.where(u == 0, qi, jnp.minimum(u - 1, n_blocks - 1))

    key_i = lax.broadcasted_iota(jnp.int32, (blk, blk), 0)
    qry_i = lax.broadcasted_iota(jnp.int32, (blk, blk), 1)
    causal = key_i <= qry_i
    for h in range(n_heads):
        s = jnp.dot(kaug_ref[h, qi], q_aug[h], preferred_element_type=jnp.float32)
        s_ref[0, h] = jnp.where(causal, s, NEG_INF)
        p_ref[1, h] = jnp.zeros((blk, blk), jnp.bfloat16)
        acc_ref[h] = jnp.zeros((PV_ROWS, blk), jnp.float32)

    blk_id = lax.broadcasted_iota(jnp.int32, (n_blocks, blk), 0)
    for h in range(n_heads):
        g3 = jnp.dot(km_ref[h], q_aug[h], preferred_element_type=jnp.float32)
        gate = g3[:n_blocks] + g3[n_blocks:2 * n_blocks] + g3[2 * n_blocks:]
        gate = jnp.where(blk_id < qi, gate, NEG_INF)
        gate_ref[h] = gate
        ranks = [jnp.zeros((n_blocks, blk), jnp.float32) for _ in range(4)]
        for m in range(n_blocks - 1):
            gm = jnp.broadcast_to(gate_ref[h, m:m + 1, :], (n_blocks, blk))
            ahead = (gm > gate) | ((gm == gate) & (blk_id > m))
            ranks[m % 4] = ranks[m % 4] + jnp.where(ahead, 1.0, 0.0)
        rank = (ranks[0] + ranks[1]) + (ranks[2] + ranks[3])
        neg = jnp.where((rank < MOBA_TOPK) & (blk_id < qi), 0.0, NEG_INF)
        neg_ref[h, 0] = jnp.zeros((1, blk), jnp.float32)
        for u in range(1, n_blocks):
            neg_ref[h, u] = neg[u - 1:u, :]

    def step(u, cur, m_runs):
        b_prev = block_of(jnp.maximum(u - 1, 0))
        k_next = jnp.minimum(u, n_blocks - 1)
        pv = [jnp.dot(vt_ref[h, b_prev], p_ref[1 - cur, h], preferred_element_type=jnp.float32)
              for h in range(n_heads)]
        for h in range(n_heads):
            s_ref[1 - cur, h] = jnp.dot(kaug_ref[h, k_next], q_aug[h],
                                        preferred_element_type=jnp.float32)
        m_news = []
        for h in range(n_heads):
            s = s_ref[cur, h]
            neg = neg_ref[h, u]
            m_new = jnp.maximum(m_runs[h], jnp.max(s, axis=0, keepdims=True) + neg)
            alpha = jnp.exp2(m_runs[h] - m_new)
            p = jnp.exp2(s - (m_new - neg))
            p_ref[cur, h] = p.astype(jnp.bfloat16)
            acc_ref[h] = alpha * (acc_ref[h] + pv[h])
            m_news.append(m_new)
        return tuple(m_news)

    def step_pair(i, m_runs):
        return step(2 * i + 1, 1, step(2 * i, 0, m_runs))

    m_init = jnp.full((1, blk), NEG_INF, jnp.float32)
    lax.fori_loop(0, (qi + 2) // 2, step_pair, (m_init,) * n_heads)
    last = 2 * ((qi + 2) // 2) - 1
    for t in range(n_tiles):
        outs = []
        for h in range(t * HEADS_PER_TILE, (t + 1) * HEADS_PER_TILE):
            acc = acc_ref[h] + jnp.dot(vt_ref[h, block_of(last)], p_ref[1, h],
                                       preferred_element_type=jnp.float32)
            outs.append(acc[:HEAD_DIM] / acc[HEAD_DIM:HEAD_DIM + 1])
        o_ref[:, t * LANES:(t + 1) * LANES] = jnp.concatenate(outs, axis=0).T.astype(jnp.bfloat16)


MOBA_TILES_PER_STEP = 2


def _moba(q, k, v, slopes):
    bn, s, width = q.shape
    step_w = MOBA_TILES_PER_STEP * LANES
    assert s % MOBA_BLOCK == 0 and width % step_w == 0
    n_blocks = s // MOBA_BLOCK
    n_heads = MOBA_TILES_PER_STEP * HEADS_PER_TILE
    assert ALIBI_TERMS * ALIBI_PIECES <= ALIBI_ROWS <= HEAD_DIM and HEADS_PER_TILE == 2
    assert n_blocks <= 256 and MOBA_BLOCK <= 256
    qspec = pl.BlockSpec((None, MOBA_BLOCK, step_w), lambda b, hg, qi: (b, qi, hg))
    kvspec = pl.BlockSpec((None, s, step_w), lambda b, hg, qi: (b, 0, hg))
    return pl.pallas_call(
        functools.partial(_moba_kernel, n_blocks=n_blocks),
        grid=(bn, width // step_w, n_blocks),
        in_specs=[pl.BlockSpec(memory_space=pltpu.SMEM), qspec, kvspec, kvspec],
        out_specs=qspec,
        out_shape=jax.ShapeDtypeStruct((bn, s, width), jnp.bfloat16),
        scratch_shapes=[
            pltpu.VMEM((n_heads, GATE_PIECES * n_blocks, LANES), jnp.bfloat16),
            pltpu.VMEM((n_heads, n_blocks, MOBA_BLOCK, LANES), jnp.bfloat16),
            pltpu.VMEM((n_heads, n_blocks, PV_ROWS, MOBA_BLOCK), jnp.bfloat16),
            pltpu.VMEM((n_heads, n_blocks, 1, MOBA_BLOCK), jnp.float32),
            pltpu.VMEM((n_heads, n_blocks, MOBA_BLOCK), jnp.float32),
            pltpu.VMEM((2, n_heads, MOBA_BLOCK, MOBA_BLOCK), jnp.float32),
            pltpu.VMEM((2, n_heads, MOBA_BLOCK, MOBA_BLOCK), jnp.bfloat16),
            pltpu.VMEM((n_heads, PV_ROWS, MOBA_BLOCK), jnp.float32),
        ],
        compiler_params=pltpu.CompilerParams(
            dimension_semantics=("arbitrary", "arbitrary", "arbitrary"),
            vmem_limit_bytes=VMEM_LIMIT),
        name="moba",
    )(slopes, q, k, v)


def _mix_out_kernel(x_ref, ya_ref, yb_ref, yc_ref, wout_ref, gpost_ref, gprex_ref, wq_ref,
                    kx_ref, vx_ref, wo_ref, gpostx_ref, o_ref):
    d = x_ref.shape[-1]
    dh = d // X_HEADS
    nt = (((1,), (1,)), ((), ()))
    rows = x_ref.shape[0] // ROW_GROUPS
    groups = [slice(r * rows, (r + 1) * rows) for r in range(ROW_GROUPS)]

    ys = [jnp.concatenate([ya_ref[g, :], yb_ref[g, :], yc_ref[g, :]], axis=-1) for g in groups]
    mixes = [jnp.dot(y, wout_ref[...], preferred_element_type=jnp.float32) for y in ys]
    x1s = [x_ref[g, :] + _rms(mix, gpost_ref[...]) for g, mix in zip(groups, mixes)]
    hs = [_rms(x1, gprex_ref[...]).astype(jnp.bfloat16) for x1 in x1s]
    qxs = [(jnp.dot(h, wq_ref[...], preferred_element_type=jnp.float32) * (dh ** -0.5)
            ).astype(jnp.bfloat16) for h in hs]
    outs = []
    for qx in qxs:
        heads = []
        for hh in range(X_HEADS):
            cols = slice(hh * dh, (hh + 1) * dh)
            s = lax.dot_general(qx[:, cols], kx_ref[:, cols], nt, preferred_element_type=jnp.float32)
            p = jnp.exp(s - jnp.max(s, axis=-1, keepdims=True))
            denom = jnp.sum(p, axis=-1, keepdims=True)
            oh = jnp.dot(p.astype(jnp.bfloat16), vx_ref[:, cols], preferred_element_type=jnp.float32)
            heads.append((oh / denom).astype(jnp.bfloat16))
        outs.append(jnp.concatenate(heads, axis=-1))
    atts = [jnp.dot(o, wo_ref[...], preferred_element_type=jnp.float32) for o in outs]
    for g, x1, att in zip(groups, x1s, atts):
        o_ref[g, :] = x1 + _rms(att, gpostx_ref[...])


def _mix_out(x, ya, yb, yc, l, w_out, g_post, g_prex, w_xq, kx, vx, w_xo, g_postx):
    bn, s, d = x.shape
    tm = min(2 * TOKEN_TILE, s)
    m = kx.shape[2]
    lsel3 = lambda b, i: (l, 0, 0)
    tok = lambda w: pl.BlockSpec((None, tm, w), lambda b, i: (b, i, 0))
    gain = pl.BlockSpec((None, 1, d), lsel3)
    wsq = _const_spec((None, d, d), lsel3)
    mem = pl.BlockSpec((None, None, m, d), lambda b, i: (l, b, 0, 0))
    return pl.pallas_call(
        _mix_out_kernel,
        grid=(bn, s // tm),
        in_specs=[tok(d), tok(ya.shape[-1]), tok(yb.shape[-1]), tok(yc.shape[-1]),
                  wsq, gain, gain, wsq, mem, mem, wsq, gain],
        out_specs=tok(d),
        out_shape=jax.ShapeDtypeStruct((bn, s, d), jnp.float32),
        compiler_params=pltpu.CompilerParams(
            dimension_semantics=("arbitrary", "arbitrary"), vmem_limit_bytes=VMEM_LIMIT),
        name="mix_out",
    )(x, ya, yb, yc, w_out, g_post, g_prex, w_xq, kx, vx, w_xo, g_postx)


def _ffn_kernel(x_ref, gpre_ref, w1_ref, w2_ref, gpost_ref, o_ref):
    x = x_ref[...]
    h = _rms(x, gpre_ref[...]).astype(jnp.bfloat16)
    a = jnp.maximum(jnp.dot(h, w1_ref[...], preferred_element_type=jnp.float32), 0.0)
    f = jnp.dot((a * a).astype(jnp.bfloat16), w2_ref[...], preferred_element_type=jnp.float32)
    o_ref[...] = x + _rms(f, gpost_ref[...])


def _ffn(x, l, g_pre, w1, w2, g_post):
    bn, s, d = x.shape
    tm = min(TOKEN_TILE, s)
    dff = w1.shape[-1]
    lsel3 = lambda b, i: (l, 0, 0)
    tok = pl.BlockSpec((None, tm, d), lambda b, i: (b, i, 0))
    gain = pl.BlockSpec((None, 1, d), lsel3)
    return pl.pallas_call(
        _ffn_kernel,
        grid=(bn, s // tm),
        in_specs=[tok, gain, _const_spec((None, d, dff), lsel3),
                  _const_spec((None, dff, d), lsel3), gain],
        out_specs=tok,
        out_shape=jax.ShapeDtypeStruct((bn, s, d), jnp.float32),
        compiler_params=pltpu.CompilerParams(
            dimension_semantics=("arbitrary", "arbitrary"), vmem_limit_bytes=VMEM_LIMIT),
        name="ffn",
    )(x, g_pre, w1, w2, g_post)


def kernel(x, mem, pre_mix_g, w_in, gate_ln_g, gate_ln_b, w_s, b_s, w_dw, b_dw, conv_gn_g, conv_gn_b, w_out, post_mix_g, pre_x_g, mem_g, w_xq, w_xkv, w_xo, post_x_g, pre_ffn_g, w_ff1, w_ff2, post_ffn_g):
    depth, d, _ = w_in.shape
    a_width = gate_ln_g.shape[-1]
    c_width = w_dw.shape[-1]
    b_width = d - a_width - c_width
    a_heads = a_width // HEAD_DIM
    b_heads = b_width // HEAD_DIM
    assert w_in.shape[-1] == 2 * a_width + 3 * b_width + 2 * c_width
    assert w_s.shape[1:] == (a_heads, CHUNK, CHUNK) and w_dw.shape[1] == CONV_WIDTH
    assert x.shape[1] % MOBA_BLOCK == 0 and x.shape[1] % TOKEN_TILE == 0

    bf16 = lambda w: w.astype(jnp.bfloat16)
    row = lambda g: g[:, None, :]
    ws_cat = w_s.transpose(0, 2, 1, 3).reshape(depth, CHUNK, a_heads * CHUNK)
    bs_x = jnp.repeat(b_s.transpose(0, 2, 1), HEAD_DIM, axis=-1)
    slopes = 2.0 ** (-8.0 * jnp.arange(1, b_heads + 1, dtype=jnp.float32) / b_heads)
    w_in_b, w_out_b, w_xq_b, w_xkv_b, w_xo_b, w_ff1_b, w_ff2_b = map(
        bf16, (w_in, w_out, w_xq, w_xkv, w_xo, w_ff1, w_ff2))

    kx, vx = _mem_kv(mem, row(mem_g), w_xkv_b)
    for l in range(depth):
        ya, q, k, v, yc = _mix_in(
            x, l, row(pre_mix_g), w_in_b, row(gate_ln_g), row(gate_ln_b), ws_cat, bs_x, w_dw,
            row(b_dw), row(conv_gn_g), row(conv_gn_b),
            a_width=a_width, b_width=b_width, c_width=c_width)
        yb = _moba(q, k, v, slopes)
        x = _mix_out(x, ya, yb, yc, l, w_out_b, row(post_mix_g), row(pre_x_g), w_xq_b, kx, vx,
                     w_xo_b, row(post_x_g))
        x = _ffn(x, l, row(pre_ffn_g), w_ff1_b, w_ff2_b, row(post_ffn_g))
    return x
```

```python
import functools
import math

import jax
import jax.numpy as jnp
from jax import lax
from jax.experimental import pallas as pl
from jax.experimental.pallas import tpu as pltpu

HEAD_DIM = 64
CHUNK = 128
MOBA_BLOCK = 256
MOBA_TOPK = 3
CONV_WIDTH = 31
X_HEADS = 4
EPS = 1e-6
LOG2_E = math.log2(math.e)

LANES = 128
SUBLANES = 8
HALO = 32
CONV_ROWS = 128
TOKEN_TILE = 512
ROW_GROUPS = 2
VMEM_LIMIT = 56 * 1024 * 1024

NEG_INF = float("-inf")


def _rms(x, g):
    return x * lax.rsqrt(jnp.mean(x * x, axis=-1, keepdims=True) + EPS) * g


def _const_spec(shape, index_map):
    return pl.BlockSpec(shape, index_map, pipeline_mode=pl.Buffered(1))


def _mem_kv_kernel(mem_ref, g_ref, w_ref, k_ref, v_ref):
    d = mem_ref.shape[-1]
    m = _rms(mem_ref[...], g_ref[...]).astype(jnp.bfloat16)
    kv = jnp.dot(m, w_ref[...], preferred_element_type=jnp.float32)
    k_ref[...] = kv[:, :d].astype(jnp.bfloat16)
    v_ref[...] = kv[:, d:].astype(jnp.bfloat16)


def _mem_kv(mem, mem_g, w_xkv):
    bn, m, d = mem.shape
    depth = w_xkv.shape[0]
    out = jax.ShapeDtypeStruct((depth, bn, m, d), jnp.bfloat16)
    return pl.pallas_call(
        _mem_kv_kernel,
        grid=(depth, bn),
        in_specs=[
            pl.BlockSpec((None, m, d), lambda l, b: (b, 0, 0)),
            pl.BlockSpec((None, 1, d), lambda l, b: (l, 0, 0)),
            pl.BlockSpec((None, d, 2 * d), lambda l, b: (l, 0, 0)),
        ],
        out_specs=[
            pl.BlockSpec((None, None, m, d), lambda l, b: (l, b, 0, 0)),
            pl.BlockSpec((None, None, m, d), lambda l, b: (l, b, 0, 0)),
        ],
        out_shape=[out, out],
        compiler_params=pltpu.CompilerParams(
            dimension_semantics=("arbitrary", "arbitrary"), vmem_limit_bytes=VMEM_LIMIT),
        name="mem_kv",
    )(mem, mem_g, w_xkv)


def _mix_in_kernel(x_ref, g_ref, win_ref, lng_ref, lnb_ref, ws_ref, bs_ref, wdw_ref, bdw_ref,
                   gng_ref, gnb_ref, ya_ref, q_ref, k_ref, v_ref, yc_ref, hist_ref, conv_ref, shift_ref,
                   *, a_width, b_width, c_width):
    tm = x_ref.shape[0]

    @pl.when(pl.program_id(1) == 0)
    def _():
        hist_ref[0:HALO, :] = jnp.zeros((HALO, c_width), jnp.float32)

    h = _rms(x_ref[...], g_ref[...]).astype(jnp.bfloat16)

    def project(lo, hi):
        return jnp.dot(h, win_ref[:, lo:hi], preferred_element_type=jnp.float32)

    o = 2 * a_width + 3 * b_width
    zc = project(o, o + 2 * c_width)
    hist_ref[HALO:HALO + tm, :] = zc[:, :c_width] * (1.0 / (1.0 + jnp.exp(-zc[:, c_width:])))

    o = 2 * a_width
    q = (project(o, o + b_width) * (HEAD_DIM ** -0.5 * LOG2_E)).astype(jnp.bfloat16)
    k = project(o + b_width, o + 2 * b_width).astype(jnp.bfloat16)
    v = project(o + 2 * b_width, o + 3 * b_width).astype(jnp.bfloat16)
    za = project(0, 2 * a_width)

    first = HALO - (CONV_WIDTH - 1)
    for c in range(tm // CONV_ROWS):
        r0 = c * CONV_ROWS
        acc = jnp.zeros((CONV_ROWS, c_width), jnp.float32)
        for phase in range(SUBLANES):
            taps = [j for j in range(CONV_WIDTH) if (first + j) % SUBLANES == phase]
            span = first + taps[-1] - phase
            rows = CONV_ROWS + span
            shift_ref[phase, 0:rows, :] = hist_ref[r0 + phase:r0 + phase + rows, :]
            for j in taps:
                off = first + j - phase
                acc = acc + wdw_ref[j:j + 1, :] * shift_ref[phase, off:off + CONV_ROWS, :]
        conv_ref[r0:r0 + CONV_ROWS, :] = acc + bdw_ref[...]
    hist_ref[0:HALO, :] = hist_ref[tm:tm + HALO, :]
    q_ref[...] = q
    k_ref[...] = k
    v_ref[...] = v

    ga = 0.5 * za * (1.0 + lax.erf(za * (2.0 ** -0.5)))
    u = ga[:, :a_width]
    vv = ga[:, a_width:]
    mu = jnp.mean(vv, axis=-1, keepdims=True)
    vc = vv - mu
    var = jnp.mean(vc * vc, axis=-1, keepdims=True)
    vn = (vc * lax.rsqrt(var + EPS) * lng_ref[...] + lnb_ref[...]).astype(jnp.bfloat16)
    n_heads = a_width // HEAD_DIM
    t_idx = lax.broadcasted_iota(jnp.int32, (CHUNK, n_heads * CHUNK), 0)
    s_idx = lax.broadcasted_iota(jnp.int32, (CHUNK, n_heads * CHUNK), 1) % CHUNK
    w_tril = jnp.where(s_idx <= t_idx, ws_ref[...], 0.0).astype(jnp.bfloat16)
    r_head = lax.broadcasted_iota(jnp.int32, (n_heads * CHUNK, a_width), 0) // CHUNK
    c_head = lax.broadcasted_iota(jnp.int32, (n_heads * CHUNK, a_width), 1) // HEAD_DIM
    head_match = r_head == c_head
    for c in range(tm // CHUNK):
        rows = slice(c * CHUNK, (c + 1) * CHUNK)
        v_chunk = vn[rows]
        v_bd = jnp.where(head_match, jnp.concatenate([v_chunk] * n_heads, axis=0), 0.0)
        mixed = jnp.dot(w_tril, v_bd.astype(jnp.bfloat16), preferred_element_type=jnp.float32)
        ya_ref[rows, :] = (u[rows] * (mixed + bs_ref[...])).astype(jnp.bfloat16)

    gr = lax.broadcasted_iota(jnp.int32, (2 * c_width, c_width), 0) % c_width // HEAD_DIM
    gc = lax.broadcasted_iota(jnp.int32, (2 * c_width, c_width), 1) // HEAD_DIM
    gmat2 = jnp.where(gr == gc, 1.0 / HEAD_DIM, 0.0).astype(jnp.bfloat16)

    def group_mean(a):
        hi = a.astype(jnp.bfloat16)
        lo = (a - hi.astype(jnp.float32)).astype(jnp.bfloat16)
        return jnp.dot(jnp.concatenate([hi, lo], axis=-1), gmat2, preferred_element_type=jnp.float32)

    y = conv_ref[...]
    yd = y - group_mean(y)
    yn = yd * lax.rsqrt(group_mean(yd * yd) + EPS) * gng_ref[...] + gnb_ref[...]
    yc_ref[...] = (yn * (1.0 / (1.0 + jnp.exp(-yn)))).astype(jnp.bfloat16)


def _mix_in(x, l, pre_g, w_in, ln_g, ln_b, ws_cat, bs_x, w_dw, b_dw, gn_g, gn_b,
            *, a_width, b_width, c_width):
    bn, s, d = x.shape
    tm = min(2 * TOKEN_TILE, s)
    in_cols = w_in.shape[-1]
    n_heads = a_width // HEAD_DIM
    lsel3 = lambda b, i: (l, 0, 0)
    tok = lambda w: pl.BlockSpec((None, tm, w), lambda b, i: (b, i, 0))
    bf = lambda w: jax.ShapeDtypeStruct((bn, s, w), jnp.bfloat16)
    return pl.pallas_call(
        functools.partial(_mix_in_kernel, a_width=a_width, b_width=b_width, c_width=c_width),
        grid=(bn, s // tm),
        in_specs=[
            tok(d),
            pl.BlockSpec((None, 1, d), lsel3),
            _const_spec((None, d, in_cols), lsel3),
            pl.BlockSpec((None, 1, a_width), lsel3),
            pl.BlockSpec((None, 1, a_width), lsel3),
            pl.BlockSpec((None, CHUNK, n_heads * CHUNK), lsel3),
            pl.BlockSpec((None, CHUNK, a_width), lsel3),
            pl.BlockSpec((None, CONV_WIDTH, c_width), lsel3),
            pl.BlockSpec((None, 1, c_width), lsel3),
            pl.BlockSpec((None, 1, c_width), lsel3),
            pl.BlockSpec((None, 1, c_width), lsel3),
        ],
        out_specs=[tok(a_width), tok(b_width), tok(b_width), tok(b_width), tok(c_width)],
        out_shape=[bf(a_width), bf(b_width), bf(b_width), bf(b_width), bf(c_width)],
        scratch_shapes=[pltpu.VMEM((tm + HALO, c_width), jnp.float32),
                        pltpu.VMEM((tm, c_width), jnp.float32),
                        pltpu.VMEM((SUBLANES, CONV_ROWS + HALO, c_width), jnp.float32)],
        compiler_params=pltpu.CompilerParams(
            dimension_semantics=("arbitrary", "arbitrary"), vmem_limit_bytes=VMEM_LIMIT),
        name="mix_in",
    )(x, pre_g, w_in, ln_g, ln_b, ws_cat, bs_x, w_dw, b_dw, gn_g, gn_b)


ALIBI_TERMS = 4
ALIBI_PIECES = 3
ALIBI_ROWS = 16
HEADS_PER_TILE = LANES // HEAD_DIM
BF16_SUBLANES = 16
PV_ROWS = HEAD_DIM + BF16_SUBLANES
GATE_PIECES = 3


def _own_lanes(h, idx):
    hh = h % HEADS_PER_TILE
    return (idx >= hh * HEAD_DIM) & (idx < (hh + 1) * HEAD_DIM)


def _slot_base(h):
    return ((h + 1) % HEADS_PER_TILE) * HEAD_DIM


def _moba_kernel(slopes_ref, q_ref, k_ref, v_ref, o_ref, km_ref, kaug_ref, vt_ref, neg_ref,
                 gate_ref, s_ref, p_ref, acc_ref, *, n_blocks):
    blk = MOBA_BLOCK
    n_tiles = q_ref.shape[-1] // LANES

    def prepare():
        lane = lax.broadcasted_iota(jnp.int32, (blk, LANES), 1)
        j_pos = lax.broadcasted_iota(jnp.int32, (blk, LANES), 0).astype(jnp.float32)
        ones_rows = (lax.broadcasted_iota(jnp.int32, (BF16_SUBLANES, blk), 0) == 0
                     ).astype(jnp.float32)
        km_row = lax.broadcasted_iota(jnp.int32, (n_blocks, LANES), 0)
        km_lane = lax.broadcasted_iota(jnp.int32, (n_blocks, LANES), 1)
        for t in range(n_tiles):
            cols = slice(t * LANES, (t + 1) * LANES)
            kmean = jnp.zeros((n_blocks, LANES), jnp.float32)
            for n in range(n_blocks):
                rows = slice(n * blk, (n + 1) * blk)
                kf = k_ref[rows, cols].astype(jnp.float32)
                kmean = jnp.where(km_row == n, jnp.sum(kf, axis=0, keepdims=True) * (1.0 / blk), kmean)
                v_t = v_ref[rows, cols].astype(jnp.float32).T
                for hh in range(HEADS_PER_TILE):
                    h = t * HEADS_PER_TILE + hh
                    term = (lane - _slot_base(h)) // ALIBI_PIECES
                    aug = jnp.where(term == 0, j_pos,
                                    jnp.where(term == 1, float(n),
                                              jnp.where((term == 2) | (term == 3), 1.0, 0.0)))
                    kaug_ref[h, n] = jnp.where(_own_lanes(h, lane), kf, aug).astype(jnp.bfloat16)
                    vt_ref[h, n] = jnp.concatenate(
                        [v_t[hh * HEAD_DIM:(hh + 1) * HEAD_DIM], ones_rows], axis=0).astype(jnp.bfloat16)
            for hh in range(HEADS_PER_TILE):
                h = t * HEADS_PER_TILE + hh
                rest = jnp.where(_own_lanes(h, km_lane), kmean, 0.0)
                pieces = []
                for _ in range(GATE_PIECES):
                    piece = rest.astype(jnp.bfloat16).astype(jnp.float32)
                    pieces.append(piece)
                    rest = rest - piece
                km_ref[h] = jnp.concatenate(pieces[::-1], axis=0).astype(jnp.bfloat16)

    prepare()
    scratch = (km_ref, kaug_ref, vt_ref, neg_ref, gate_ref, s_ref, p_ref, acc_ref)

    def sweep(qi, carry):
        _moba_tile(qi, slopes_ref, q_ref, o_ref, *scratch, n_blocks=n_blocks)
        return carry

    lax.fori_loop(0, n_blocks, sweep, 0)


def _moba_tile(qi, slopes_ref, q_ref, o_ref, km_ref, kaug_ref, vt_ref, neg_ref, gate_ref, s_ref,
               p_ref, acc_ref, *, n_blocks):
    blk = MOBA_BLOCK
    n_tiles = q_ref.shape[-1] // LANES
    n_heads = n_tiles * HEADS_PER_TILE
    hg = pl.program_id(1)
    q_rows = pl.ds(pl.multiple_of(qi * blk, blk), blk)

    slot = lax.broadcasted_iota(jnp.int32, (ALIBI_ROWS, blk), 0)
    i_pos = lax.broadcasted_iota(jnp.int32, (ALIBI_ROWS, blk), 1).astype(jnp.float32)
    qi_f = qi.astype(jnp.float32)
    pad = jnp.zeros((HEAD_DIM - ALIBI_ROWS, blk), jnp.float32)
    q_aug = []
    for t in range(n_tiles):
        q_t = q_ref[q_rows, t * LANES:(t + 1) * LANES].astype(jnp.float32).T
        for hh in range(HEADS_PER_TILE):
            h = t * HEADS_PER_TILE + hh
            c = slopes_ref[hg * n_heads + h] * LOG2_E
            term = slot // ALIBI_PIECES
            full = jnp.where(term == 0, c,
                             jnp.where(term == 1, c * blk,
                                       jnp.where(term == 2, -(c * (blk * qi_f)),
                                                 jnp.where(term == 3, -(c * i_pos), 0.0))))
            piece = full.astype(jnp.bfloat16).astype(jnp.float32)
            aug = piece
            for r in range(1, ALIBI_PIECES):
                full = full - piece
                piece = full.astype(jnp.bfloat16).astype(jnp.float32)
                aug = jnp.where(slot % ALIBI_PIECES == r, piece, aug)
            own = q_t[hh * HEAD_DIM:(hh + 1) * HEAD_DIM]
            rows = [own, aug, pad] if _slot_base(h) > 0 else [aug, pad, own]
            q_aug.append(jnp.concatenate(rows, axis=0).astype(jnp.bfloat16))

    def block_of(u):
        return jnp.where(u == 0, qi, jnp.minimum(u - 1, n_blocks - 1))

    key_i = lax.broadcasted_iota(jnp.int32, (blk, blk), 0)
    qry_i = lax.broadcasted_iota(jnp.int32, (blk, blk), 1)
    causal = key_i <= qry_i
    for h in range(n_heads):
        s = jnp.dot(kaug_ref[h, qi], q_aug[h], preferred_element_type=jnp.float32)
        s_ref[0, h] = jnp.where(causal, s, NEG_INF)
        p_ref[1, h] = jnp.zeros((blk, blk), jnp.bfloat16)
        acc_ref[h] = jnp.zeros((PV_ROWS, blk), jnp.float32)

    blk_id = lax.broadcasted_iota(jnp.int32, (n_blocks, blk), 0)
    for h in range(n_heads):
        g3 = jnp.dot(km_ref[h], q_aug[h], preferred_element_type=jnp.float32)
        gate = g3[:n_blocks] + g3[n_blocks:2 * n_blocks] + g3[2 * n_blocks:]
        gate = jnp.where(blk_id < qi, gate, NEG_INF)
        gate_ref[h] = gate
        ranks = [jnp.zeros((n_blocks, blk), jnp.float32) for _ in range(4)]
        for m in range(n_blocks - 1):
            gm = jnp.broadcast_to(gate_ref[h, m:m + 1, :], (n_blocks, blk))
            ahead = (gm > gate) | ((gm == gate) & (blk_id > m))
            ranks[m % 4] = ranks[m % 4] + jnp.where(ahead, 1.0, 0.0)
        rank = (ranks[0] + ranks[1]) + (ranks[2] + ranks[3])
        neg = jnp.where((rank < MOBA_TOPK) & (blk_id < qi), 0.0, NEG_INF)
        neg_ref[h, 0] = jnp.zeros((1, blk), jnp.float32)
        for u in range(1, n_blocks):
            neg_ref[h, u] = neg[u - 1:u, :]

    def step(u, cur, m_runs):
        b_prev = block_of(jnp.maximum(u - 1, 0))
        k_next = jnp.minimum(u, n_blocks - 1)
        pv = [jnp.dot(vt_ref[h, b_prev], p_ref[1 - cur, h], preferred_element_type=jnp.float32)
              for h in range(n_heads)]
        for h in range(n_heads):
            s_ref[1 - cur, h] = jnp.dot(kaug_ref[h, k_next], q_aug[h],
                                        preferred_element_type=jnp.float32)
        m_news = []
        for h in range(n_heads):
            s = s_ref[cur, h]
            neg = neg_ref[h, u]
            m_new = jnp.maximum(m_runs[h], jnp.max(s, axis=0, keepdims=True) + neg)
            alpha = jnp.exp2(m_runs[h] - m_new)
            p = jnp.exp2(s - (m_new - neg))
            p_ref[cur, h] = p.astype(jnp.bfloat16)
            acc_ref[h] = alpha * (acc_ref[h] + pv[h])
            m_news.append(m_new)
        return tuple(m_news)

    def step_pair(i, m_runs):
        return step(2 * i + 1, 1, step(2 * i, 0, m_runs))

    m_init = jnp.full((1, blk), NEG_INF, jnp.float32)
    lax.fori_loop(0, (qi + 2) // 2, step_pair, (m_init,) * n_heads)
    last = 2 * ((qi + 2) // 2) - 1
    for t in range(n_tiles):
        outs = []
        for h in range(t * HEADS_PER_TILE, (t + 1) * HEADS_PER_TILE):
            acc = acc_ref[h] + jnp.dot(vt_ref[h, block_of(last)], p_ref[1, h],
                                       preferred_element_type=jnp.float32)
            outs.append(acc[:HEAD_DIM] / acc[HEAD_DIM:HEAD_DIM + 1])
        o_ref[q_rows, t * LANES:(t + 1) * LANES] = jnp.concatenate(outs, axis=0).T.astype(jnp.bfloat16)


MOBA_TILES_PER_STEP = 2


def _moba(q, k, v, slopes):
    bn, s, width = q.shape
    step_w = MOBA_TILES_PER_STEP * LANES
    assert s % MOBA_BLOCK == 0 and width % step_w == 0
    n_blocks = s // MOBA_BLOCK
    n_heads = MOBA_TILES_PER_STEP * HEADS_PER_TILE
    assert ALIBI_TERMS * ALIBI_PIECES <= ALIBI_ROWS <= HEAD_DIM and HEADS_PER_TILE == 2
    assert n_blocks <= 256 and MOBA_BLOCK <= 256
    seq = pl.BlockSpec((None, s, step_w), lambda b, hg: (b, 0, hg))
    return pl.pallas_call(
        functools.partial(_moba_kernel, n_blocks=n_blocks),
        grid=(bn, width // step_w),
        in_specs=[pl.BlockSpec(memory_space=pltpu.SMEM), seq, seq, seq],
        out_specs=seq,
        out_shape=jax.ShapeDtypeStruct((bn, s, width), jnp.bfloat16),
        scratch_shapes=[
            pltpu.VMEM((n_heads, GATE_PIECES * n_blocks, LANES), jnp.bfloat16),
            pltpu.VMEM((n_heads, n_blocks, MOBA_BLOCK, LANES), jnp.bfloat16),
            pltpu.VMEM((n_heads, n_blocks, PV_ROWS, MOBA_BLOCK), jnp.bfloat16),
            pltpu.VMEM((n_heads, n_blocks, 1, MOBA_BLOCK), jnp.float32),
            pltpu.VMEM((n_heads, n_blocks, MOBA_BLOCK), jnp.float32),
            pltpu.VMEM((2, n_heads, MOBA_BLOCK, MOBA_BLOCK), jnp.float32),
            pltpu.VMEM((2, n_heads, MOBA_BLOCK, MOBA_BLOCK), jnp.bfloat16),
            pltpu.VMEM((n_heads, PV_ROWS, MOBA_BLOCK), jnp.float32),
        ],
        compiler_params=pltpu.CompilerParams(
            dimension_semantics=("arbitrary", "arbitrary"),
            vmem_limit_bytes=VMEM_LIMIT),
        name="moba",
    )(slopes, q, k, v)


def _mix_out_kernel(x_ref, ya_ref, yb_ref, yc_ref, wout_ref, gpost_ref, gprex_ref, wq_ref,
                    kx_ref, vx_ref, wo_ref, gpostx_ref, o_ref):
    d = x_ref.shape[-1]
    dh = d // X_HEADS
    nt = (((1,), (1,)), ((), ()))
    rows = x_ref.shape[0] // ROW_GROUPS
    groups = [slice(r * rows, (r + 1) * rows) for r in range(ROW_GROUPS)]

    ys = [jnp.concatenate([ya_ref[g, :], yb_ref[g, :], yc_ref[g, :]], axis=-1) for g in groups]
    mixes = [jnp.dot(y, wout_ref[...], preferred_element_type=jnp.float32) for y in ys]
    x1s = [x_ref[g, :] + _rms(mix, gpost_ref[...]) for g, mix in zip(groups, mixes)]
    hs = [_rms(x1, gprex_ref[...]).astype(jnp.bfloat16) for x1 in x1s]
    qxs = [(jnp.dot(h, wq_ref[...], preferred_element_type=jnp.float32) * (dh ** -0.5)
            ).astype(jnp.bfloat16) for h in hs]
    outs = []
    for qx in qxs:
        heads = []
        for hh in range(X_HEADS):
            cols = slice(hh * dh, (hh + 1) * dh)
            s = lax.dot_general(qx[:, cols], kx_ref[:, cols], nt, preferred_element_type=jnp.float32)
            p = jnp.exp(s - jnp.max(s, axis=-1, keepdims=True))
            denom = jnp.sum(p, axis=-1, keepdims=True)
            oh = jnp.dot(p.astype(jnp.bfloat16), vx_ref[:, cols], preferred_element_type=jnp.float32)
            heads.append((oh / denom).astype(jnp.bfloat16))
        outs.append(jnp.concatenate(heads, axis=-1))
    atts = [jnp.dot(o, wo_ref[...], preferred_element_type=jnp.float32) for o in outs]
    for g, x1, att in zip(groups, x1s, atts):
        o_ref[g, :] = x1 + _rms(att, gpostx_ref[...])


def _mix_out(x, ya, yb, yc, l, w_out, g_post, g_prex, w_xq, kx, vx, w_xo, g_postx):
    bn, s, d = x.shape
    tm = min(ROW_GROUPS * TOKEN_TILE, s)
    m = kx.shape[2]
    lsel3 = lambda b, i: (l, 0, 0)
    tok = lambda w: pl.BlockSpec((None, tm, w), lambda b, i: (b, i, 0))
    gain = pl.BlockSpec((None, 1, d), lsel3)
    wsq = _const_spec((None, d, d), lsel3)
    mem = pl.BlockSpec((None, None, m, d), lambda b, i: (l, b, 0, 0))
    return pl.pallas_call(
        _mix_out_kernel,
        grid=(bn, s // tm),
        in_specs=[tok(d), tok(ya.shape[-1]), tok(yb.shape[-1]), tok(yc.shape[-1]),
                  wsq, gain, gain, wsq, mem, mem, wsq, gain],
        out_specs=tok(d),
        out_shape=jax.ShapeDtypeStruct((bn, s, d), jnp.float32),
        compiler_params=pltpu.CompilerParams(
            dimension_semantics=("arbitrary", "arbitrary"), vmem_limit_bytes=VMEM_LIMIT),
        name="mix_out",
    )(x, ya, yb, yc, w_out, g_post, g_prex, w_xq, kx, vx, w_xo, g_postx)


def _ffn_kernel(x_ref, gpre_ref, w1_ref, w2_ref, gpost_ref, o_ref):
    rows = x_ref.shape[0] // ROW_GROUPS
    for r in range(ROW_GROUPS):
        g = slice(r * rows, (r + 1) * rows)
        x = x_ref[g, :]
        h = _rms(x, gpre_ref[...]).astype(jnp.bfloat16)
        a = jnp.maximum(jnp.dot(h, w1_ref[...], preferred_element_type=jnp.float32), 0.0)
        f = jnp.dot((a * a).astype(jnp.bfloat16), w2_ref[...], preferred_element_type=jnp.float32)
        o_ref[g, :] = x + _rms(f, gpost_ref[...])


def _ffn(x, l, g_pre, w1, w2, g_post):
    bn, s, d = x.shape
    tm = min(ROW_GROUPS * TOKEN_TILE, s)
    dff = w1.shape[-1]
    lsel3 = lambda b, i: (l, 0, 0)
    tok = pl.BlockSpec((None, tm, d), lambda b, i: (b, i, 0))
    gain = pl.BlockSpec((None, 1, d), lsel3)
    return pl.pallas_call(
        _ffn_kernel,
        grid=(bn, s // tm),
        in_specs=[tok, gain, _const_spec((None, d, dff), lsel3),
                  _const_spec((None, dff, d), lsel3), gain],
        out_specs=tok,
        out_shape=jax.ShapeDtypeStruct((bn, s, d), jnp.float32),
        compiler_params=pltpu.CompilerParams(
            dimension_semantics=("arbitrary", "arbitrary"), vmem_limit_bytes=VMEM_LIMIT),
        name="ffn",
    )(x, g_pre, w1, w2, g_post)


def kernel(x, mem, pre_mix_g, w_in, gate_ln_g, gate_ln_b, w_s, b_s, w_dw, b_dw, conv_gn_g, conv_gn_b, w_out, post_mix_g, pre_x_g, mem_g, w_xq, w_xkv, w_xo, post_x_g, pre_ffn_g, w_ff1, w_ff2, post_ffn_g):
    depth, d, _ = w_in.shape
    a_width = gate_ln_g.shape[-1]
    c_width = w_dw.shape[-1]
    b_width = d - a_width - c_width
    a_heads = a_width // HEAD_DIM
    b_heads = b_width // HEAD_DIM
    assert w_in.shape[-1] == 2 * a_width + 3 * b_width + 2 * c_width
    assert w_s.shape[1:] == (a_heads, CHUNK, CHUNK) and w_dw.shape[1] == CONV_WIDTH
    assert x.shape[1] % MOBA_BLOCK == 0 and x.shape[1] % TOKEN_TILE == 0

    bf16 = lambda w: w.astype(jnp.bfloat16)
    row = lambda g: g[:, None, :]
    ws_cat = w_s.transpose(0, 2, 1, 3).reshape(depth, CHUNK, a_heads * CHUNK)
    bs_x = jnp.repeat(b_s.transpose(0, 2, 1), HEAD_DIM, axis=-1)
    slopes = 2.0 ** (-8.0 * jnp.arange(1, b_heads + 1, dtype=jnp.float32) / b_heads)
    w_in_b, w_out_b, w_xq_b, w_xkv_b, w_xo_b, w_ff1_b, w_ff2_b = map(
        bf16, (w_in, w_out, w_xq, w_xkv, w_xo, w_ff1, w_ff2))

    kx, vx = _mem_kv(mem, row(mem_g), w_xkv_b)
    for l in range(depth):
        ya, q, k, v, yc = _mix_in(
            x, l, row(pre_mix_g), w_in_b, row(gate_ln_g), row(gate_ln_b), ws_cat, bs_x, w_dw,
            row(b_dw), row(conv_gn_g), row(conv_gn_b),
            a_width=a_width, b_width=b_width, c_width=c_width)
        yb = _moba(q, k, v, slopes)
        x = _mix_out(x, ya, yb, yc, l, w_out_b, row(post_mix_g), row(pre_x_g), w_xq_b, kx, vx,
                     w_xo_b, row(post_x_g))
        x = _ffn(x, l, row(pre_ffn_g), w_ff1_b, w_ff2_b, row(post_ffn_g))
    return x
```

```python
import functools
import math

import jax
import jax.numpy as jnp
from jax import lax
from jax.experimental import pallas as pl
from jax.experimental.pallas import tpu as pltpu

HEAD_DIM = 64
CHUNK = 128
MOBA_BLOCK = 256
MOBA_TOPK = 3
CONV_WIDTH = 31
X_HEADS = 4
EPS = 1e-6
LOG2_E = math.log2(math.e)

LANES = 128
SUBLANES = 8
HALO = 32
CONV_ROWS = 128
TOKEN_TILE = 512
ROW_GROUPS = 2
VMEM_LIMIT = 56 * 1024 * 1024

NEG_INF = float("-inf")


def _rms(x, g):
    return x * lax.rsqrt(jnp.mean(x * x, axis=-1, keepdims=True) + EPS) * g


def _const_spec(shape, index_map):
    return pl.BlockSpec(shape, index_map, pipeline_mode=pl.Buffered(1))


def _mem_kv_kernel(mem_ref, g_ref, w_ref, k_ref, v_ref):
    d = mem_ref.shape[-1]
    m = _rms(mem_ref[...], g_ref[...]).astype(jnp.bfloat16)
    kv = jnp.dot(m, w_ref[...], preferred_element_type=jnp.float32)
    k_ref[...] = kv[:, :d].astype(jnp.bfloat16)
    v_ref[...] = kv[:, d:].astype(jnp.bfloat16)


def _mem_kv(mem, mem_g, w_xkv):
    bn, m, d = mem.shape
    depth = w_xkv.shape[0]
    out = jax.ShapeDtypeStruct((depth, bn, m, d), jnp.bfloat16)
    return pl.pallas_call(
        _mem_kv_kernel,
        grid=(depth, bn),
        in_specs=[
            pl.BlockSpec((None, m, d), lambda l, b: (b, 0, 0)),
            pl.BlockSpec((None, 1, d), lambda l, b: (l, 0, 0)),
            pl.BlockSpec((None, d, 2 * d), lambda l, b: (l, 0, 0)),
        ],
        out_specs=[
            pl.BlockSpec((None, None, m, d), lambda l, b: (l, b, 0, 0)),
            pl.BlockSpec((None, None, m, d), lambda l, b: (l, b, 0, 0)),
        ],
        out_shape=[out, out],
        compiler_params=pltpu.CompilerParams(
            dimension_semantics=("arbitrary", "arbitrary"), vmem_limit_bytes=VMEM_LIMIT),
        name="mem_kv",
    )(mem, mem_g, w_xkv)


def _mix_in_kernel(x_ref, g_ref, win_ref, lng_ref, lnb_ref, ws_ref, bs_ref, wdw_ref, bdw_ref,
                   gng_ref, gnb_ref, ya_ref, q_ref, k_ref, v_ref, yc_ref, hist_ref, conv_ref, shift_ref,
                   *, a_width, b_width, c_width):
    tm = x_ref.shape[0]

    @pl.when(pl.program_id(1) == 0)
    def _():
        hist_ref[0:HALO, :] = jnp.zeros((HALO, c_width), jnp.float32)

    h = _rms(x_ref[...], g_ref[...]).astype(jnp.bfloat16)

    def project(lo, hi):
        return jnp.dot(h, win_ref[:, lo:hi], preferred_element_type=jnp.float32)

    o = 2 * a_width + 3 * b_width
    zc = project(o, o + 2 * c_width)
    hist_ref[HALO:HALO + tm, :] = zc[:, :c_width] * (1.0 / (1.0 + jnp.exp(-zc[:, c_width:])))

    o = 2 * a_width
    q = (project(o, o + b_width) * (HEAD_DIM ** -0.5 * LOG2_E)).astype(jnp.bfloat16)
    k = project(o + b_width, o + 2 * b_width).astype(jnp.bfloat16)
    v = project(o + 2 * b_width, o + 3 * b_width).astype(jnp.bfloat16)
    za = project(0, 2 * a_width)

    first = HALO - (CONV_WIDTH - 1)
    for c in range(tm // CONV_ROWS):
        r0 = c * CONV_ROWS
        acc = jnp.zeros((CONV_ROWS, c_width), jnp.float32)
        for phase in range(SUBLANES):
            taps = [j for j in range(CONV_WIDTH) if (first + j) % SUBLANES == phase]
            span = first + taps[-1] - phase
            rows = CONV_ROWS + span
            shift_ref[phase, 0:rows, :] = hist_ref[r0 + phase:r0 + phase + rows, :]
            for j in taps:
                off = first + j - phase
                acc = acc + wdw_ref[j:j + 1, :] * shift_ref[phase, off:off + CONV_ROWS, :]
        conv_ref[r0:r0 + CONV_ROWS, :] = acc + bdw_ref[...]
    hist_ref[0:HALO, :] = hist_ref[tm:tm + HALO, :]
    q_ref[...] = q
    k_ref[...] = k
    v_ref[...] = v

    ga = 0.5 * za * (1.0 + lax.erf(za * (2.0 ** -0.5)))
    u = ga[:, :a_width]
    vv = ga[:, a_width:]
    mu = jnp.mean(vv, axis=-1, keepdims=True)
    vc = vv - mu
    var = jnp.mean(vc * vc, axis=-1, keepdims=True)
    vn = (vc * lax.rsqrt(var + EPS) * lng_ref[...] + lnb_ref[...]).astype(jnp.bfloat16)
    n_heads = a_width // HEAD_DIM
    t_idx = lax.broadcasted_iota(jnp.int32, (CHUNK, n_heads * CHUNK), 0)
    s_idx = lax.broadcasted_iota(jnp.int32, (CHUNK, n_heads * CHUNK), 1) % CHUNK
    w_tril = jnp.where(s_idx <= t_idx, ws_ref[...], 0.0).astype(jnp.bfloat16)
    r_head = lax.broadcasted_iota(jnp.int32, (n_heads * CHUNK, a_width), 0) // CHUNK
    c_head = lax.broadcasted_iota(jnp.int32, (n_heads * CHUNK, a_width), 1) // HEAD_DIM
    head_match = r_head == c_head
    for c in range(tm // CHUNK):
        rows = slice(c * CHUNK, (c + 1) * CHUNK)
        v_chunk = vn[rows]
        v_bd = jnp.where(head_match, jnp.concatenate([v_chunk] * n_heads, axis=0), 0.0)
        mixed = jnp.dot(w_tril, v_bd.astype(jnp.bfloat16), preferred_element_type=jnp.float32)
        ya_ref[rows, :] = (u[rows] * (mixed + bs_ref[...])).astype(jnp.bfloat16)

    gr = lax.broadcasted_iota(jnp.int32, (2 * c_width, c_width), 0) % c_width // HEAD_DIM
    gc = lax.broadcasted_iota(jnp.int32, (2 * c_width, c_width), 1) // HEAD_DIM
    gmat2 = jnp.where(gr == gc, 1.0 / HEAD_DIM, 0.0).astype(jnp.bfloat16)

    def group_mean(a):
        hi = a.astype(jnp.bfloat16)
        lo = (a - hi.astype(jnp.float32)).astype(jnp.bfloat16)
        return jnp.dot(jnp.concatenate([hi, lo], axis=-1), gmat2, preferred_element_type=jnp.float32)

    y = conv_ref[...]
    yd = y - group_mean(y)
    yn = yd * lax.rsqrt(group_mean(yd * yd) + EPS) * gng_ref[...] + gnb_ref[...]
    yc_ref[...] = (yn * (1.0 / (1.0 + jnp.exp(-yn)))).astype(jnp.bfloat16)


def _mix_in(x, l, pre_g, w_in, ln_g, ln_b, ws_cat, bs_x, w_dw, b_dw, gn_g, gn_b,
            *, a_width, b_width, c_width):
    bn, s, d = x.shape
    tm = min(2 * TOKEN_TILE, s)
    in_cols = w_in.shape[-1]
    n_heads = a_width // HEAD_DIM
    lsel3 = lambda b, i: (l, 0, 0)
    tok = lambda w: pl.BlockSpec((None, tm, w), lambda b, i: (b, i, 0))
    bf = lambda w: jax.ShapeDtypeStruct((bn, s, w), jnp.bfloat16)
    return pl.pallas_call(
        functools.partial(_mix_in_kernel, a_width=a_width, b_width=b_width, c_width=c_width),
        grid=(bn, s // tm),
        in_specs=[
            tok(d),
            pl.BlockSpec((None, 1, d), lsel3),
            _const_spec((None, d, in_cols), lsel3),
            pl.BlockSpec((None, 1, a_width), lsel3),
            pl.BlockSpec((None, 1, a_width), lsel3),
            pl.BlockSpec((None, CHUNK, n_heads * CHUNK), lsel3),
            pl.BlockSpec((None, CHUNK, a_width), lsel3),
            pl.BlockSpec((None, CONV_WIDTH, c_width), lsel3),
            pl.BlockSpec((None, 1, c_width), lsel3),
            pl.BlockSpec((None, 1, c_width), lsel3),
            pl.BlockSpec((None, 1, c_width), lsel3),
        ],
        out_specs=[tok(a_width), tok(b_width), tok(b_width), tok(b_width), tok(c_width)],
        out_shape=[bf(a_width), bf(b_width), bf(b_width), bf(b_width), bf(c_width)],
        scratch_shapes=[pltpu.VMEM((tm + HALO, c_width), jnp.float32),
                        pltpu.VMEM((tm, c_width), jnp.float32),
                        pltpu.VMEM((SUBLANES, CONV_ROWS + HALO, c_width), jnp.float32)],
        compiler_params=pltpu.CompilerParams(
            dimension_semantics=("arbitrary", "arbitrary"), vmem_limit_bytes=VMEM_LIMIT),
        name="mix_in",
    )(x, pre_g, w_in, ln_g, ln_b, ws_cat, bs_x, w_dw, b_dw, gn_g, gn_b)


ALIBI_TERMS = 4
ALIBI_PIECES = 3
ALIBI_ROWS = 16
HEADS_PER_TILE = LANES // HEAD_DIM
BF16_SUBLANES = 16
PV_ROWS = HEAD_DIM + BF16_SUBLANES
GATE_PIECES = 3
SELECT_TILES = 4


def _own_lanes(h, idx):
    hh = h % HEADS_PER_TILE
    return (idx >= hh * HEAD_DIM) & (idx < (hh + 1) * HEAD_DIM)


def _slot_base(h):
    return ((h + 1) % HEADS_PER_TILE) * HEAD_DIM


def _moba_kernel(slopes_ref, q_ref, k_ref, v_ref, o_ref, km_ref, kaug_ref, vt_ref, qaug_ref, neg_ref,
                 s_ref, p_ref, acc_ref, *, n_blocks):
    blk = MOBA_BLOCK
    n_tiles = q_ref.shape[-1] // LANES

    def prepare():
        lane = lax.broadcasted_iota(jnp.int32, (blk, LANES), 1)
        j_pos = lax.broadcasted_iota(jnp.int32, (blk, LANES), 0).astype(jnp.float32)
        ones_rows = (lax.broadcasted_iota(jnp.int32, (BF16_SUBLANES, blk), 0) == 0
                     ).astype(jnp.float32)
        km_row = lax.broadcasted_iota(jnp.int32, (n_blocks, LANES), 0)
        km_lane = lax.broadcasted_iota(jnp.int32, (n_blocks, LANES), 1)
        for t in range(n_tiles):
            cols = slice(t * LANES, (t + 1) * LANES)
            kmean = jnp.zeros((n_blocks, LANES), jnp.float32)
            for n in range(n_blocks):
                rows = slice(n * blk, (n + 1) * blk)
                kf = k_ref[rows, cols].astype(jnp.float32)
                kmean = jnp.where(km_row == n, jnp.sum(kf, axis=0, keepdims=True) * (1.0 / blk), kmean)
                v_t = v_ref[rows, cols].astype(jnp.float32).T
                for hh in range(HEADS_PER_TILE):
                    h = t * HEADS_PER_TILE + hh
                    term = (lane - _slot_base(h)) // ALIBI_PIECES
                    aug = jnp.where(term == 0, j_pos,
                                    jnp.where(term == 1, float(n),
                                              jnp.where((term == 2) | (term == 3), 1.0, 0.0)))
                    kaug_ref[h, n] = jnp.where(_own_lanes(h, lane), kf, aug).astype(jnp.bfloat16)
                    vt_ref[h, n] = jnp.concatenate(
                        [v_t[hh * HEAD_DIM:(hh + 1) * HEAD_DIM], ones_rows], axis=0).astype(jnp.bfloat16)
            for hh in range(HEADS_PER_TILE):
                h = t * HEADS_PER_TILE + hh
                rest = jnp.where(_own_lanes(h, km_lane), kmean, 0.0)
                pieces = []
                for _ in range(GATE_PIECES):
                    piece = rest.astype(jnp.bfloat16).astype(jnp.float32)
                    pieces.append(piece)
                    rest = rest - piece
                km_ref[h] = jnp.concatenate(pieces[::-1], axis=0).astype(jnp.bfloat16)

    prepare()

    def select(g, carry):
        _moba_select(g, slopes_ref, q_ref, km_ref, qaug_ref, neg_ref, n_blocks=n_blocks)
        return carry

    lax.fori_loop(0, n_blocks // SELECT_TILES, select, 0)

    def sweep(qi, carry):
        _moba_tile(qi, o_ref, kaug_ref, vt_ref, qaug_ref, neg_ref, s_ref, p_ref, acc_ref,
                   n_blocks=n_blocks)
        return carry

    lax.fori_loop(0, n_blocks, sweep, 0)


def _moba_select(g, slopes_ref, q_ref, km_ref, qaug_ref, neg_ref, *, n_blocks):
    blk = MOBA_BLOCK
    width = SELECT_TILES * blk
    n_tiles = q_ref.shape[-1] // LANES
    n_heads = n_tiles * HEADS_PER_TILE
    hg = pl.program_id(1)
    first_tile = jnp.asarray(g, jnp.int32) * SELECT_TILES
    q_rows = pl.ds(pl.multiple_of(first_tile * blk, width), width)

    def tile_of(shape):
        return first_tile + lax.broadcasted_iota(jnp.int32, shape, 1) // blk

    slot = lax.broadcasted_iota(jnp.int32, (ALIBI_ROWS, width), 0)
    i_pos = (lax.broadcasted_iota(jnp.int32, (ALIBI_ROWS, width), 1) % blk).astype(jnp.float32)
    tile_f = tile_of((ALIBI_ROWS, width)).astype(jnp.float32)
    pad = jnp.zeros((HEAD_DIM - ALIBI_ROWS, width), jnp.float32)
    blk_id = lax.broadcasted_iota(jnp.int32, (n_blocks, width), 0)
    past = blk_id < tile_of((n_blocks, width))
    for t in range(n_tiles):
        q_t = q_ref[q_rows, t * LANES:(t + 1) * LANES].astype(jnp.float32).T
        for hh in range(HEADS_PER_TILE):
            h = t * HEADS_PER_TILE + hh
            c = slopes_ref[hg * n_heads + h] * LOG2_E
            term = slot // ALIBI_PIECES
            full = jnp.where(term == 0, c,
                             jnp.where(term == 1, c * blk,
                                       jnp.where(term == 2, -(c * (blk * tile_f)),
                                                 jnp.where(term == 3, -(c * i_pos), 0.0))))
            piece = full.astype(jnp.bfloat16).astype(jnp.float32)
            aug = piece
            for r in range(1, ALIBI_PIECES):
                full = full - piece
                piece = full.astype(jnp.bfloat16).astype(jnp.float32)
                aug = jnp.where(slot % ALIBI_PIECES == r, piece, aug)
            own = q_t[hh * HEAD_DIM:(hh + 1) * HEAD_DIM]
            rows = [own, aug, pad] if _slot_base(h) > 0 else [aug, pad, own]
            q_aug = jnp.concatenate(rows, axis=0).astype(jnp.bfloat16)
            for j in range(SELECT_TILES):
                qaug_ref[h, first_tile + j] = q_aug[:, j * blk:(j + 1) * blk]

            g3 = jnp.dot(km_ref[h], q_aug, preferred_element_type=jnp.float32)
            gate = g3[:n_blocks] + g3[n_blocks:2 * n_blocks] + g3[2 * n_blocks:]
            gate = jnp.where(past, gate, NEG_INF)
            blk_f = blk_id.astype(jnp.float32)
            chosen = jnp.zeros((n_blocks, width), jnp.float32)
            for _ in range(MOBA_TOPK):
                best = jnp.max(gate, axis=0, keepdims=True)
                pick = jnp.min(jnp.where(gate == best, blk_f, float(n_blocks)), axis=0, keepdims=True)
                taken = blk_f == pick
                chosen = jnp.where(taken, 1.0, chosen)
                gate = jnp.where(taken, NEG_INF, gate)
            neg = jnp.where((chosen > 0.5) & past, 0.0, NEG_INF)
            for j in range(SELECT_TILES):
                neg_ref[h, first_tile + j, 0] = jnp.zeros((1, blk), jnp.float32)
                for u in range(1, n_blocks):
                    neg_ref[h, first_tile + j, u] = neg[u - 1:u, j * blk:(j + 1) * blk]


def _moba_tile(qi, o_ref, kaug_ref, vt_ref, qaug_ref, neg_ref, s_ref, p_ref, acc_ref, *, n_blocks):
    blk = MOBA_BLOCK
    n_heads = kaug_ref.shape[0]
    n_tiles = n_heads // HEADS_PER_TILE
    q_rows = pl.ds(pl.multiple_of(qi * blk, blk), blk)
    q_aug = [qaug_ref[h, qi] for h in range(n_heads)]

    def block_of(u):
        return jnp.where(u == 0, qi, jnp.minimum(u - 1, n_blocks - 1))

    key_i = lax.broadcasted_iota(jnp.int32, (blk, blk), 0)
    qry_i = lax.broadcasted_iota(jnp.int32, (blk, blk), 1)
    causal = key_i <= qry_i
    for h in range(n_heads):
        s = jnp.dot(kaug_ref[h, qi], q_aug[h], preferred_element_type=jnp.float32)
        s_ref[0, h] = jnp.where(causal, s, NEG_INF)
        p_ref[1, h] = jnp.zeros((blk, blk), jnp.bfloat16)
        acc_ref[h] = jnp.zeros((PV_ROWS, blk), jnp.float32)

    def step(u, cur, m_runs):
        b_prev = block_of(jnp.maximum(u - 1, 0))
        k_next = jnp.minimum(u, n_blocks - 1)
        pv = [jnp.dot(vt_ref[h, b_prev], p_ref[1 - cur, h], preferred_element_type=jnp.float32)
              for h in range(n_heads)]
        for h in range(n_heads):
            s_ref[1 - cur, h] = jnp.dot(kaug_ref[h, k_next], q_aug[h],
                                        preferred_element_type=jnp.float32)
        m_news = []
        for h in range(n_heads):
            s = s_ref[cur, h]
            neg = neg_ref[h, qi, u]
            m_new = jnp.maximum(m_runs[h], jnp.max(s, axis=0, keepdims=True) + neg)
            alpha = jnp.exp2(m_runs[h] - m_new)
            p = jnp.exp2(s - (m_new - neg))
            p_ref[cur, h] = p.astype(jnp.bfloat16)
            acc_ref[h] = alpha * (acc_ref[h] + pv[h])
            m_news.append(m_new)
        return tuple(m_news)

    def step_pair(i, m_runs):
        return step(2 * i + 1, 1, step(2 * i, 0, m_runs))

    m_init = jnp.full((1, blk), NEG_INF, jnp.float32)
    lax.fori_loop(0, (qi + 2) // 2, step_pair, (m_init,) * n_heads)
    last = 2 * ((qi + 2) // 2) - 1
    for t in range(n_tiles):
        outs = []
        for h in range(t * HEADS_PER_TILE, (t + 1) * HEADS_PER_TILE):
            acc = acc_ref[h] + jnp.dot(vt_ref[h, block_of(last)], p_ref[1, h],
                                       preferred_element_type=jnp.float32)
            outs.append(acc[:HEAD_DIM] / acc[HEAD_DIM:HEAD_DIM + 1])
        o_ref[q_rows, t * LANES:(t + 1) * LANES] = jnp.concatenate(outs, axis=0).T.astype(jnp.bfloat16)


MOBA_TILES_PER_STEP = 2


def _moba(q, k, v, slopes):
    bn, s, width = q.shape
    step_w = MOBA_TILES_PER_STEP * LANES
    assert s % MOBA_BLOCK == 0 and width % step_w == 0
    n_blocks = s // MOBA_BLOCK
    n_heads = MOBA_TILES_PER_STEP * HEADS_PER_TILE
    assert ALIBI_TERMS * ALIBI_PIECES <= ALIBI_ROWS <= HEAD_DIM and HEADS_PER_TILE == 2
    assert n_blocks <= 256 and MOBA_BLOCK <= 256
    seq = pl.BlockSpec((None, s, step_w), lambda b, hg: (b, 0, hg))
    return pl.pallas_call(
        functools.partial(_moba_kernel, n_blocks=n_blocks),
        grid=(bn, width // step_w),
        in_specs=[pl.BlockSpec(memory_space=pltpu.SMEM), seq, seq, seq],
        out_specs=seq,
        out_shape=jax.ShapeDtypeStruct((bn, s, width), jnp.bfloat16),
        scratch_shapes=[
            pltpu.VMEM((n_heads, GATE_PIECES * n_blocks, LANES), jnp.bfloat16),
            pltpu.VMEM((n_heads, n_blocks, MOBA_BLOCK, LANES), jnp.bfloat16),
            pltpu.VMEM((n_heads, n_blocks, PV_ROWS, MOBA_BLOCK), jnp.bfloat16),
            pltpu.VMEM((n_heads, n_blocks, LANES, MOBA_BLOCK), jnp.bfloat16),
            pltpu.VMEM((n_heads, n_blocks, n_blocks, 1, MOBA_BLOCK), jnp.float32),
            pltpu.VMEM((2, n_heads, MOBA_BLOCK, MOBA_BLOCK), jnp.float32),
            pltpu.VMEM((2, n_heads, MOBA_BLOCK, MOBA_BLOCK), jnp.bfloat16),
            pltpu.VMEM((n_heads, PV_ROWS, MOBA_BLOCK), jnp.float32),
        ],
        compiler_params=pltpu.CompilerParams(
            dimension_semantics=("arbitrary", "arbitrary"),
            vmem_limit_bytes=VMEM_LIMIT),
        name="moba",
    )(slopes, q, k, v)


def _mix_out_kernel(x_ref, ya_ref, yb_ref, yc_ref, wout_ref, gpost_ref, gprex_ref, wq_ref,
                    kx_ref, vx_ref, wo_ref, gpostx_ref, o_ref):
    d = x_ref.shape[-1]
    dh = d // X_HEADS
    nt = (((1,), (1,)), ((), ()))
    rows = x_ref.shape[0] // ROW_GROUPS
    groups = [slice(r * rows, (r + 1) * rows) for r in range(ROW_GROUPS)]

    ys = [jnp.concatenate([ya_ref[g, :], yb_ref[g, :], yc_ref[g, :]], axis=-1) for g in groups]
    mixes = [jnp.dot(y, wout_ref[...], preferred_element_type=jnp.float32) for y in ys]
    x1s = [x_ref[g, :] + _rms(mix, gpost_ref[...]) for g, mix in zip(groups, mixes)]
    hs = [_rms(x1, gprex_ref[...]).astype(jnp.bfloat16) for x1 in x1s]
    qxs = [(jnp.dot(h, wq_ref[...], preferred_element_type=jnp.float32) * (dh ** -0.5)
            ).astype(jnp.bfloat16) for h in hs]
    outs = []
    for qx in qxs:
        heads = []
        for hh in range(X_HEADS):
            cols = slice(hh * dh, (hh + 1) * dh)
            s = lax.dot_general(qx[:, cols], kx_ref[:, cols], nt, preferred_element_type=jnp.float32)
            p = jnp.exp(s - jnp.max(s, axis=-1, keepdims=True))
            denom = jnp.sum(p, axis=-1, keepdims=True)
            oh = jnp.dot(p.astype(jnp.bfloat16), vx_ref[:, cols], preferred_element_type=jnp.float32)
            heads.append((oh / denom).astype(jnp.bfloat16))
        outs.append(jnp.concatenate(heads, axis=-1))
    atts = [jnp.dot(o, wo_ref[...], preferred_element_type=jnp.float32) for o in outs]
    for g, x1, att in zip(groups, x1s, atts):
        o_ref[g, :] = x1 + _rms(att, gpostx_ref[...])


def _mix_out(x, ya, yb, yc, l, w_out, g_post, g_prex, w_xq, kx, vx, w_xo, g_postx):
    bn, s, d = x.shape
    tm = min(ROW_GROUPS * TOKEN_TILE, s)
    m = kx.shape[2]
    lsel3 = lambda b, i: (l, 0, 0)
    tok = lambda w: pl.BlockSpec((None, tm, w), lambda b, i: (b, i, 0))
    gain = pl.BlockSpec((None, 1, d), lsel3)
    wsq = _const_spec((None, d, d), lsel3)
    mem = pl.BlockSpec((None, None, m, d), lambda b, i: (l, b, 0, 0))
    return pl.pallas_call(
        _mix_out_kernel,
        grid=(bn, s // tm),
        in_specs=[tok(d), tok(ya.shape[-1]), tok(yb.shape[-1]), tok(yc.shape[-1]),
                  wsq, gain, gain, wsq, mem, mem, wsq, gain],
        out_specs=tok(d),
        out_shape=jax.ShapeDtypeStruct((bn, s, d), jnp.float32),
        compiler_params=pltpu.CompilerParams(
            dimension_semantics=("arbitrary", "arbitrary"), vmem_limit_bytes=VMEM_LIMIT),
        name="mix_out",
    )(x, ya, yb, yc, w_out, g_post, g_prex, w_xq, kx, vx, w_xo, g_postx)


def _ffn_kernel(x_ref, gpre_ref, w1_ref, w2_ref, gpost_ref, o_ref):
    rows = x_ref.shape[0] // ROW_GROUPS
    for r in range(ROW_GROUPS):
        g = slice(r * rows, (r + 1) * rows)
        x = x_ref[g, :]
        h = _rms(x, gpre_ref[...]).astype(jnp.bfloat16)
        a = jnp.maximum(jnp.dot(h, w1_ref[...], preferred_element_type=jnp.float32), 0.0)
        f = jnp.dot((a * a).astype(jnp.bfloat16), w2_ref[...], preferred_element_type=jnp.float32)
        o_ref[g, :] = x + _rms(f, gpost_ref[...])


def _ffn(x, l, g_pre, w1, w2, g_post):
    bn, s, d = x.shape
    tm = min(ROW_GROUPS * TOKEN_TILE, s)
    dff = w1.shape[-1]
    lsel3 = lambda b, i: (l, 0, 0)
    tok = pl.BlockSpec((None, tm, d), lambda b, i: (b, i, 0))
    gain = pl.BlockSpec((None, 1, d), lsel3)
    return pl.pallas_call(
        _ffn_kernel,
        grid=(bn, s // tm),
        in_specs=[tok, gain, _const_spec((None, d, dff), lsel3),
                  _const_spec((None, dff, d), lsel3), gain],
        out_specs=tok,
        out_shape=jax.ShapeDtypeStruct((bn, s, d), jnp.float32),
        compiler_params=pltpu.CompilerParams(
            dimension_semantics=("arbitrary", "arbitrary"), vmem_limit_bytes=VMEM_LIMIT),
        name="ffn",
    )(x, g_pre, w1, w2, g_post)


def kernel(x, mem, pre_mix_g, w_in, gate_ln_g, gate_ln_b, w_s, b_s, w_dw, b_dw, conv_gn_g, conv_gn_b, w_out, post_mix_g, pre_x_g, mem_g, w_xq, w_xkv, w_xo, post_x_g, pre_ffn_g, w_ff1, w_ff2, post_ffn_g):
    depth, d, _ = w_in.shape
    a_width = gate_ln_g.shape[-1]
    c_width = w_dw.shape[-1]
    b_width = d - a_width - c_width
    a_heads = a_width // HEAD_DIM
    b_heads = b_width // HEAD_DIM
    assert w_in.shape[-1] == 2 * a_width + 3 * b_width + 2 * c_width
    assert w_s.shape[1:] == (a_heads, CHUNK, CHUNK) and w_dw.shape[1] == CONV_WIDTH
    assert x.shape[1] % MOBA_BLOCK == 0 and x.shape[1] % TOKEN_TILE == 0

    bf16 = lambda w: w.astype(jnp.bfloat16)
    row = lambda g: g[:, None, :]
    ws_cat = w_s.transpose(0, 2, 1, 3).reshape(depth, CHUNK, a_heads * CHUNK)
    bs_x = jnp.repeat(b_s.transpose(0, 2, 1), HEAD_DIM, axis=-1)
    slopes = 2.0 ** (-8.0 * jnp.arange(1, b_heads + 1, dtype=jnp.float32) / b_heads)
    w_in_b, w_out_b, w_xq_b, w_xkv_b, w_xo_b, w_ff1_b, w_ff2_b = map(
        bf16, (w_in, w_out, w_xq, w_xkv, w_xo, w_ff1, w_ff2))

    kx, vx = _mem_kv(mem, row(mem_g), w_xkv_b)
    for l in range(depth):
        ya, q, k, v, yc = _mix_in(
            x, l, row(pre_mix_g), w_in_b, row(gate_ln_g), row(gate_ln_b), ws_cat, bs_x, w_dw,
            row(b_dw), row(conv_gn_g), row(conv_gn_b),
            a_width=a_width, b_width=b_width, c_width=c_width)
        yb = _moba(q, k, v, slopes)
        x = _mix_out(x, ya, yb, yc, l, w_out_b, row(post_mix_g), row(pre_x_g), w_xq_b, kx, vx,
                     w_xo_b, row(post_x_g))
        x = _ffn(x, l, row(pre_ffn_g), w_ff1_b, w_ff2_b, row(post_ffn_g))
    return x
```

```python
import functools
import math

import jax
import jax.numpy as jnp
from jax import lax
from jax.experimental import pallas as pl
from jax.experimental.pallas import tpu as pltpu

HEAD_DIM = 64
CHUNK = 128
MOBA_BLOCK = 256
MOBA_TOPK = 3
CONV_WIDTH = 31
X_HEADS = 4
EPS = 1e-6
LOG2_E = math.log2(math.e)

LANES = 128
SUBLANES = 8
HALO = 32
CONV_ROWS = 128
TOKEN_TILE = 512
ROW_GROUPS = 2
VMEM_LIMIT = 56 * 1024 * 1024

NEG_INF = float("-inf")


def _rms(x, g):
    return x * lax.rsqrt(jnp.mean(x * x, axis=-1, keepdims=True) + EPS) * g


def _const_spec(shape, index_map):
    return pl.BlockSpec(shape, index_map, pipeline_mode=pl.Buffered(1))


def _mem_kv_kernel(mem_ref, g_ref, w_ref, k_ref, v_ref):
    d = mem_ref.shape[-1]
    m = _rms(mem_ref[...], g_ref[...]).astype(jnp.bfloat16)
    kv = jnp.dot(m, w_ref[...], preferred_element_type=jnp.float32)
    k_ref[...] = kv[:, :d].astype(jnp.bfloat16)
    v_ref[...] = kv[:, d:].astype(jnp.bfloat16)


def _mem_kv(mem, mem_g, w_xkv):
    bn, m, d = mem.shape
    depth = w_xkv.shape[0]
    out = jax.ShapeDtypeStruct((depth, bn, m, d), jnp.bfloat16)
    return pl.pallas_call(
        _mem_kv_kernel,
        grid=(depth, bn),
        in_specs=[
            pl.BlockSpec((None, m, d), lambda l, b: (b, 0, 0)),
            pl.BlockSpec((None, 1, d), lambda l, b: (l, 0, 0)),
            pl.BlockSpec((None, d, 2 * d), lambda l, b: (l, 0, 0)),
        ],
        out_specs=[
            pl.BlockSpec((None, None, m, d), lambda l, b: (l, b, 0, 0)),
            pl.BlockSpec((None, None, m, d), lambda l, b: (l, b, 0, 0)),
        ],
        out_shape=[out, out],
        compiler_params=pltpu.CompilerParams(
            dimension_semantics=("arbitrary", "arbitrary"), vmem_limit_bytes=VMEM_LIMIT),
        name="mem_kv",
    )(mem, mem_g, w_xkv)


def _mix_in_kernel(x_ref, g_ref, win_ref, lng_ref, lnb_ref, ws_ref, bs_ref, wdw_ref, bdw_ref,
                   gng_ref, gnb_ref, ya_ref, q_ref, k_ref, v_ref, yc_ref, hist_ref, conv_ref, shift_ref,
                   *, a_width, b_width, c_width):
    tm = x_ref.shape[0]

    @pl.when(pl.program_id(1) == 0)
    def _():
        hist_ref[0:HALO, :] = jnp.zeros((HALO, c_width), jnp.float32)

    h = _rms(x_ref[...], g_ref[...]).astype(jnp.bfloat16)

    def project(lo, hi):
        return jnp.dot(h, win_ref[:, lo:hi], preferred_element_type=jnp.float32)

    o = 2 * a_width + 3 * b_width
    zc = project(o, o + 2 * c_width)
    hist_ref[HALO:HALO + tm, :] = zc[:, :c_width] * (1.0 / (1.0 + jnp.exp(-zc[:, c_width:])))

    o = 2 * a_width
    q = (project(o, o + b_width) * (HEAD_DIM ** -0.5 * LOG2_E)).astype(jnp.bfloat16)
    k = project(o + b_width, o + 2 * b_width).astype(jnp.bfloat16)
    v = project(o + 2 * b_width, o + 3 * b_width).astype(jnp.bfloat16)
    za = project(0, 2 * a_width)

    first = HALO - (CONV_WIDTH - 1)
    for c in range(tm // CONV_ROWS):
        r0 = c * CONV_ROWS
        acc = jnp.zeros((CONV_ROWS, c_width), jnp.float32)
        for phase in range(SUBLANES):
            taps = [j for j in range(CONV_WIDTH) if (first + j) % SUBLANES == phase]
            span = first + taps[-1] - phase
            rows = CONV_ROWS + span
            shift_ref[phase, 0:rows, :] = hist_ref[r0 + phase:r0 + phase + rows, :]
            for j in taps:
                off = first + j - phase
                acc = acc + wdw_ref[j:j + 1, :] * shift_ref[phase, off:off + CONV_ROWS, :]
        conv_ref[r0:r0 + CONV_ROWS, :] = acc + bdw_ref[...]
    hist_ref[0:HALO, :] = hist_ref[tm:tm + HALO, :]
    q_ref[...] = q
    k_ref[...] = k
    v_ref[...] = v

    ga = 0.5 * za * (1.0 + lax.erf(za * (2.0 ** -0.5)))
    u = ga[:, :a_width]
    vv = ga[:, a_width:]
    mu = jnp.mean(vv, axis=-1, keepdims=True)
    vc = vv - mu
    var = jnp.mean(vc * vc, axis=-1, keepdims=True)
    vn = (vc * lax.rsqrt(var + EPS) * lng_ref[...] + lnb_ref[...]).astype(jnp.bfloat16)
    n_heads = a_width // HEAD_DIM
    t_idx = lax.broadcasted_iota(jnp.int32, (CHUNK, n_heads * CHUNK), 0)
    s_idx = lax.broadcasted_iota(jnp.int32, (CHUNK, n_heads * CHUNK), 1) % CHUNK
    w_tril = jnp.where(s_idx <= t_idx, ws_ref[...], 0.0).astype(jnp.bfloat16)
    r_head = lax.broadcasted_iota(jnp.int32, (n_heads * CHUNK, a_width), 0) // CHUNK
    c_head = lax.broadcasted_iota(jnp.int32, (n_heads * CHUNK, a_width), 1) // HEAD_DIM
    head_match = r_head == c_head
    for c in range(tm // CHUNK):
        rows = slice(c * CHUNK, (c + 1) * CHUNK)
        v_chunk = vn[rows]
        v_bd = jnp.where(head_match, jnp.concatenate([v_chunk] * n_heads, axis=0), 0.0)
        mixed = jnp.dot(w_tril, v_bd.astype(jnp.bfloat16), preferred_element_type=jnp.float32)
        ya_ref[rows, :] = (u[rows] * (mixed + bs_ref[...])).astype(jnp.bfloat16)

    gr = lax.broadcasted_iota(jnp.int32, (2 * c_width, c_width), 0) % c_width // HEAD_DIM
    gc = lax.broadcasted_iota(jnp.int32, (2 * c_width, c_width), 1) // HEAD_DIM
    gmat2 = jnp.where(gr == gc, 1.0 / HEAD_DIM, 0.0).astype(jnp.bfloat16)

    def group_mean(a):
        hi = a.astype(jnp.bfloat16)
        lo = (a - hi.astype(jnp.float32)).astype(jnp.bfloat16)
        return jnp.dot(jnp.concatenate([hi, lo], axis=-1), gmat2, preferred_element_type=jnp.float32)

    y = conv_ref[...]
    yd = y - group_mean(y)
    yn = yd * lax.rsqrt(group_mean(yd * yd) + EPS) * gng_ref[...] + gnb_ref[...]
    yc_ref[...] = (yn * (1.0 / (1.0 + jnp.exp(-yn)))).astype(jnp.bfloat16)


def _mix_in(x, l, pre_g, w_in, ln_g, ln_b, ws_cat, bs_x, w_dw, b_dw, gn_g, gn_b,
            *, a_width, b_width, c_width):
    bn, s, d = x.shape
    tm = min(2 * TOKEN_TILE, s)
    in_cols = w_in.shape[-1]
    n_heads = a_width // HEAD_DIM
    lsel3 = lambda b, i: (l, 0, 0)
    tok = lambda w: pl.BlockSpec((None, tm, w), lambda b, i: (b, i, 0))
    bf = lambda w: jax.ShapeDtypeStruct((bn, s, w), jnp.bfloat16)
    return pl.pallas_call(
        functools.partial(_mix_in_kernel, a_width=a_width, b_width=b_width, c_width=c_width),
        grid=(bn, s // tm),
        in_specs=[
            tok(d),
            pl.BlockSpec((None, 1, d), lsel3),
            _const_spec((None, d, in_cols), lsel3),
            pl.BlockSpec((None, 1, a_width), lsel3),
            pl.BlockSpec((None, 1, a_width), lsel3),
            pl.BlockSpec((None, CHUNK, n_heads * CHUNK), lsel3),
            pl.BlockSpec((None, CHUNK, a_width), lsel3),
            pl.BlockSpec((None, CONV_WIDTH, c_width), lsel3),
            pl.BlockSpec((None, 1, c_width), lsel3),
            pl.BlockSpec((None, 1, c_width), lsel3),
            pl.BlockSpec((None, 1, c_width), lsel3),
        ],
        out_specs=[tok(a_width), tok(b_width), tok(b_width), tok(b_width), tok(c_width)],
        out_shape=[bf(a_width), bf(b_width), bf(b_width), bf(b_width), bf(c_width)],
        scratch_shapes=[pltpu.VMEM((tm + HALO, c_width), jnp.float32),
                        pltpu.VMEM((tm, c_width), jnp.float32),
                        pltpu.VMEM((SUBLANES, CONV_ROWS + HALO, c_width), jnp.float32)],
        compiler_params=pltpu.CompilerParams(
            dimension_semantics=("arbitrary", "arbitrary"), vmem_limit_bytes=VMEM_LIMIT),
        name="mix_in",
    )(x, pre_g, w_in, ln_g, ln_b, ws_cat, bs_x, w_dw, b_dw, gn_g, gn_b)


ALIBI_TERMS = 4
ALIBI_PIECES = 3
ALIBI_ROWS = 16
HEADS_PER_TILE = LANES // HEAD_DIM
BF16_SUBLANES = 16
PV_ROWS = HEAD_DIM + BF16_SUBLANES
GATE_PIECES = 3
SELECT_TILES = 4


def _own_lanes(h, idx):
    hh = h % HEADS_PER_TILE
    return (idx >= hh * HEAD_DIM) & (idx < (hh + 1) * HEAD_DIM)


def _slot_base(h):
    return ((h + 1) % HEADS_PER_TILE) * HEAD_DIM


def _moba_kernel(slopes_ref, q_ref, k_ref, v_ref, o_ref, km_ref, kaug_ref, vt_ref, qaug_ref, neg_ref,
                 s_ref, p_ref, acc_ref, *, n_blocks):
    blk = MOBA_BLOCK
    n_tiles = q_ref.shape[-1] // LANES

    def prepare():
        lane = lax.broadcasted_iota(jnp.int32, (blk, LANES), 1)
        j_pos = lax.broadcasted_iota(jnp.int32, (blk, LANES), 0).astype(jnp.float32)
        ones_rows = (lax.broadcasted_iota(jnp.int32, (BF16_SUBLANES, blk), 0) == 0
                     ).astype(jnp.float32)
        km_row = lax.broadcasted_iota(jnp.int32, (n_blocks, LANES), 0)
        km_lane = lax.broadcasted_iota(jnp.int32, (n_blocks, LANES), 1)
        for t in range(n_tiles):
            cols = slice(t * LANES, (t + 1) * LANES)
            kmean = jnp.zeros((n_blocks, LANES), jnp.float32)
            for n in range(n_blocks):
                rows = slice(n * blk, (n + 1) * blk)
                kf = k_ref[rows, cols].astype(jnp.float32)
                kmean = jnp.where(km_row == n, jnp.sum(kf, axis=0, keepdims=True) * (1.0 / blk), kmean)
                v_t = v_ref[rows, cols].astype(jnp.float32).T
                for hh in range(HEADS_PER_TILE):
                    h = t * HEADS_PER_TILE + hh
                    term = (lane - _slot_base(h)) // ALIBI_PIECES
                    aug = jnp.where(term == 0, j_pos,
                                    jnp.where(term == 1, float(n),
                                              jnp.where((term == 2) | (term == 3), 1.0, 0.0)))
                    kaug_ref[h, n] = jnp.where(_own_lanes(h, lane), kf, aug).astype(jnp.bfloat16)
                    vt_ref[h, n] = jnp.concatenate(
                        [v_t[hh * HEAD_DIM:(hh + 1) * HEAD_DIM], ones_rows], axis=0).astype(jnp.bfloat16)
            for hh in range(HEADS_PER_TILE):
                h = t * HEADS_PER_TILE + hh
                rest = jnp.where(_own_lanes(h, km_lane), kmean, 0.0)
                pieces = []
                for _ in range(GATE_PIECES):
                    piece = rest.astype(jnp.bfloat16).astype(jnp.float32)
                    pieces.append(piece)
                    rest = rest - piece
                km_ref[h] = jnp.concatenate(pieces[::-1], axis=0).astype(jnp.bfloat16)

    prepare()

    def select(g, carry):
        _moba_select(g, slopes_ref, q_ref, km_ref, qaug_ref, neg_ref, n_blocks=n_blocks)
        return carry

    lax.fori_loop(0, n_blocks // SELECT_TILES, select, 0)

    _moba_open(0, kaug_ref, qaug_ref, s_ref)

    def sweep(qi, carry):
        _moba_tile(qi, o_ref, kaug_ref, vt_ref, qaug_ref, neg_ref, s_ref, p_ref, acc_ref,
                   n_blocks=n_blocks)
        _moba_open(jnp.minimum(qi + 1, n_blocks - 1), kaug_ref, qaug_ref, s_ref)
        return carry

    lax.fori_loop(0, n_blocks, sweep, 0)


def _moba_select(g, slopes_ref, q_ref, km_ref, qaug_ref, neg_ref, *, n_blocks):
    blk = MOBA_BLOCK
    width = SELECT_TILES * blk
    n_tiles = q_ref.shape[-1] // LANES
    n_heads = n_tiles * HEADS_PER_TILE
    hg = pl.program_id(1)
    first_tile = jnp.asarray(g, jnp.int32) * SELECT_TILES
    q_rows = pl.ds(pl.multiple_of(first_tile * blk, width), width)

    def tile_of(shape):
        return first_tile + lax.broadcasted_iota(jnp.int32, shape, 1) // blk

    slot = lax.broadcasted_iota(jnp.int32, (ALIBI_ROWS, width), 0)
    i_pos = (lax.broadcasted_iota(jnp.int32, (ALIBI_ROWS, width), 1) % blk).astype(jnp.float32)
    tile_f = tile_of((ALIBI_ROWS, width)).astype(jnp.float32)
    pad = jnp.zeros((HEAD_DIM - ALIBI_ROWS, width), jnp.float32)
    blk_id = lax.broadcasted_iota(jnp.int32, (n_blocks, width), 0)
    past = blk_id < tile_of((n_blocks, width))
    for t in range(n_tiles):
        q_t = q_ref[q_rows, t * LANES:(t + 1) * LANES].astype(jnp.float32).T
        for hh in range(HEADS_PER_TILE):
            h = t * HEADS_PER_TILE + hh
            c = slopes_ref[hg * n_heads + h] * LOG2_E
            term = slot // ALIBI_PIECES
            full = jnp.where(term == 0, c,
                             jnp.where(term == 1, c * blk,
                                       jnp.where(term == 2, -(c * (blk * tile_f)),
                                                 jnp.where(term == 3, -(c * i_pos), 0.0))))
            piece = full.astype(jnp.bfloat16).astype(jnp.float32)
            aug = piece
            for r in range(1, ALIBI_PIECES):
                full = full - piece
                piece = full.astype(jnp.bfloat16).astype(jnp.float32)
                aug = jnp.where(slot % ALIBI_PIECES == r, piece, aug)
            own = q_t[hh * HEAD_DIM:(hh + 1) * HEAD_DIM]
            rows = [own, aug, pad] if _slot_base(h) > 0 else [aug, pad, own]
            q_aug = jnp.concatenate(rows, axis=0).astype(jnp.bfloat16)
            for j in range(SELECT_TILES):
                qaug_ref[h, first_tile + j] = q_aug[:, j * blk:(j + 1) * blk]

            g3 = jnp.dot(km_ref[h], q_aug, preferred_element_type=jnp.float32)
            gate = g3[:n_blocks] + g3[n_blocks:2 * n_blocks] + g3[2 * n_blocks:]
            gate = jnp.where(past, gate, NEG_INF)
            blk_f = blk_id.astype(jnp.float32)
            chosen = jnp.zeros((n_blocks, width), jnp.float32)
            for _ in range(MOBA_TOPK):
                best = jnp.max(gate, axis=0, keepdims=True)
                pick = jnp.min(jnp.where(gate == best, blk_f, float(n_blocks)), axis=0, keepdims=True)
                taken = blk_f == pick
                chosen = jnp.where(taken, 1.0, chosen)
                gate = jnp.where(taken, NEG_INF, gate)
            neg = jnp.where((chosen > 0.5) & past, 0.0, NEG_INF)
            for j in range(SELECT_TILES):
                neg_ref[h, first_tile + j, 0] = jnp.zeros((1, blk), jnp.float32)
                for u in range(1, n_blocks):
                    neg_ref[h, first_tile + j, u] = neg[u - 1:u, j * blk:(j + 1) * blk]


def _moba_open(qi, kaug_ref, qaug_ref, s_ref):
    blk = MOBA_BLOCK
    key_i = lax.broadcasted_iota(jnp.int32, (blk, blk), 0)
    qry_i = lax.broadcasted_iota(jnp.int32, (blk, blk), 1)
    causal = key_i <= qry_i
    for h in range(kaug_ref.shape[0]):
        s = jnp.dot(kaug_ref[h, qi], qaug_ref[h, qi], preferred_element_type=jnp.float32)
        s_ref[0, h] = jnp.where(causal, s, NEG_INF)


def _moba_tile(qi, o_ref, kaug_ref, vt_ref, qaug_ref, neg_ref, s_ref, p_ref, acc_ref, *, n_blocks):
    blk = MOBA_BLOCK
    n_heads = kaug_ref.shape[0]
    n_tiles = n_heads // HEADS_PER_TILE
    q_rows = pl.ds(pl.multiple_of(qi * blk, blk), blk)
    q_aug = [qaug_ref[h, qi] for h in range(n_heads)]
    for h in range(n_heads):
        p_ref[1, h] = jnp.zeros((blk, blk), jnp.bfloat16)
        acc_ref[h] = jnp.zeros((PV_ROWS, blk), jnp.float32)

    def block_of(u):
        return jnp.where(u == 0, qi, jnp.minimum(u - 1, n_blocks - 1))

    def step(u, cur, m_runs):
        b_prev = block_of(jnp.maximum(u - 1, 0))
        k_next = jnp.minimum(u, n_blocks - 1)
        pv = [jnp.dot(vt_ref[h, b_prev], p_ref[1 - cur, h], preferred_element_type=jnp.float32)
              for h in range(n_heads)]
        for h in range(n_heads):
            s_ref[1 - cur, h] = jnp.dot(kaug_ref[h, k_next], q_aug[h],
                                        preferred_element_type=jnp.float32)
        m_news = []
        for h in range(n_heads):
            s = s_ref[cur, h]
            neg = neg_ref[h, qi, u]
            m_new = jnp.maximum(m_runs[h], jnp.max(s, axis=0, keepdims=True) + neg)
            alpha = jnp.exp2(m_runs[h] - m_new)
            p = jnp.exp2(s - (m_new - neg))
            p_ref[cur, h] = p.astype(jnp.bfloat16)
            acc_ref[h] = alpha * (acc_ref[h] + pv[h])
            m_news.append(m_new)
        return tuple(m_news)

    def step_pair(i, m_runs):
        return step(2 * i + 1, 1, step(2 * i, 0, m_runs))

    m_init = jnp.full((1, blk), NEG_INF, jnp.float32)
    lax.fori_loop(0, (qi + 2) // 2, step_pair, (m_init,) * n_heads)
    last = 2 * ((qi + 2) // 2) - 1
    for t in range(n_tiles):
        outs = []
        for h in range(t * HEADS_PER_TILE, (t + 1) * HEADS_PER_TILE):
            acc = acc_ref[h] + jnp.dot(vt_ref[h, block_of(last)], p_ref[1, h],
                                       preferred_element_type=jnp.float32)
            outs.append(acc[:HEAD_DIM] / acc[HEAD_DIM:HEAD_DIM + 1])
        o_ref[q_rows, t * LANES:(t + 1) * LANES] = jnp.concatenate(outs, axis=0).T.astype(jnp.bfloat16)


MOBA_TILES_PER_STEP = 2


def _moba(q, k, v, slopes):
    bn, s, width = q.shape
    step_w = MOBA_TILES_PER_STEP * LANES
    assert s % MOBA_BLOCK == 0 and width % step_w == 0
    n_blocks = s // MOBA_BLOCK
    n_heads = MOBA_TILES_PER_STEP * HEADS_PER_TILE
    assert ALIBI_TERMS * ALIBI_PIECES <= ALIBI_ROWS <= HEAD_DIM and HEADS_PER_TILE == 2
    assert n_blocks <= 256 and MOBA_BLOCK <= 256
    seq = pl.BlockSpec((None, s, step_w), lambda b, hg: (b, 0, hg))
    return pl.pallas_call(
        functools.partial(_moba_kernel, n_blocks=n_blocks),
        grid=(bn, width // step_w),
        in_specs=[pl.BlockSpec(memory_space=pltpu.SMEM), seq, seq, seq],
        out_specs=seq,
        out_shape=jax.ShapeDtypeStruct((bn, s, width), jnp.bfloat16),
        scratch_shapes=[
            pltpu.VMEM((n_heads, GATE_PIECES * n_blocks, LANES), jnp.bfloat16),
            pltpu.VMEM((n_heads, n_blocks, MOBA_BLOCK, LANES), jnp.bfloat16),
            pltpu.VMEM((n_heads, n_blocks, PV_ROWS, MOBA_BLOCK), jnp.bfloat16),
            pltpu.VMEM((n_heads, n_blocks, LANES, MOBA_BLOCK), jnp.bfloat16),
            pltpu.VMEM((n_heads, n_blocks, n_blocks, 1, MOBA_BLOCK), jnp.float32),
            pltpu.VMEM((2, n_heads, MOBA_BLOCK, MOBA_BLOCK), jnp.float32),
            pltpu.VMEM((2, n_heads, MOBA_BLOCK, MOBA_BLOCK), jnp.bfloat16),
            pltpu.VMEM((n_heads, PV_ROWS, MOBA_BLOCK), jnp.float32),
        ],
        compiler_params=pltpu.CompilerParams(
            dimension_semantics=("arbitrary", "arbitrary"),
            vmem_limit_bytes=VMEM_LIMIT),
        name="moba",
    )(slopes, q, k, v)


def _mix_out_kernel(x_ref, ya_ref, yb_ref, yc_ref, wout_ref, gpost_ref, gprex_ref, wq_ref,
                    kx_ref, vx_ref, wo_ref, gpostx_ref, o_ref):
    d = x_ref.shape[-1]
    dh = d // X_HEADS
    nt = (((1,), (1,)), ((), ()))
    rows = x_ref.shape[0] // ROW_GROUPS
    groups = [slice(r * rows, (r + 1) * rows) for r in range(ROW_GROUPS)]

    ys = [jnp.concatenate([ya_ref[g, :], yb_ref[g, :], yc_ref[g, :]], axis=-1) for g in groups]
    mixes = [jnp.dot(y, wout_ref[...], preferred_element_type=jnp.float32) for y in ys]
    x1s = [x_ref[g, :] + _rms(mix, gpost_ref[...]) for g, mix in zip(groups, mixes)]
    hs = [_rms(x1, gprex_ref[...]).astype(jnp.bfloat16) for x1 in x1s]
    qxs = [(jnp.dot(h, wq_ref[...], preferred_element_type=jnp.float32) * (dh ** -0.5)
            ).astype(jnp.bfloat16) for h in hs]
    outs = []
    for qx in qxs:
        heads = []
        for hh in range(X_HEADS):
            cols = slice(hh * dh, (hh + 1) * dh)
            s = lax.dot_general(qx[:, cols], kx_ref[:, cols], nt, preferred_element_type=jnp.float32)
            p = jnp.exp(s - jnp.max(s, axis=-1, keepdims=True))
            denom = jnp.sum(p, axis=-1, keepdims=True)
            oh = jnp.dot(p.astype(jnp.bfloat16), vx_ref[:, cols], preferred_element_type=jnp.float32)
            heads.append((oh / denom).astype(jnp.bfloat16))
        outs.append(jnp.concatenate(heads, axis=-1))
    atts = [jnp.dot(o, wo_ref[...], preferred_element_type=jnp.float32) for o in outs]
    for g, x1, att in zip(groups, x1s, atts):
        o_ref[g, :] = x1 + _rms(att, gpostx_ref[...])


def _mix_out(x, ya, yb, yc, l, w_out, g_post, g_prex, w_xq, kx, vx, w_xo, g_postx):
    bn, s, d = x.shape
    tm = min(ROW_GROUPS * TOKEN_TILE, s)
    m = kx.shape[2]
    lsel3 = lambda b, i: (l, 0, 0)
    tok = lambda w: pl.BlockSpec((None, tm, w), lambda b, i: (b, i, 0))
    gain = pl.BlockSpec((None, 1, d), lsel3)
    wsq = _const_spec((None, d, d), lsel3)
    mem = pl.BlockSpec((None, None, m, d), lambda b, i: (l, b, 0, 0))
    return pl.pallas_call(
        _mix_out_kernel,
        grid=(bn, s // tm),
        in_specs=[tok(d), tok(ya.shape[-1]), tok(yb.shape[-1]), tok(yc.shape[-1]),
                  wsq, gain, gain, wsq, mem, mem, wsq, gain],
        out_specs=tok(d),
        out_shape=jax.ShapeDtypeStruct((bn, s, d), jnp.float32),
        compiler_params=pltpu.CompilerParams(
            dimension_semantics=("arbitrary", "arbitrary"), vmem_limit_bytes=VMEM_LIMIT),
        name="mix_out",
    )(x, ya, yb, yc, w_out, g_post, g_prex, w_xq, kx, vx, w_xo, g_postx)


def _ffn_kernel(x_ref, gpre_ref, w1_ref, w2_ref, gpost_ref, o_ref):
    rows = x_ref.shape[0] // ROW_GROUPS
    for r in range(ROW_GROUPS):
        g = slice(r * rows, (r + 1) * rows)
        x = x_ref[g, :]
        h = _rms(x, gpre_ref[...]).astype(jnp.bfloat16)
        a = jnp.maximum(jnp.dot(h, w1_ref[...], preferred_element_type=jnp.float32), 0.0)
        f = jnp.dot((a * a).astype(jnp.bfloat16), w2_ref[...], preferred_element_type=jnp.float32)
        o_ref[g, :] = x + _rms(f, gpost_ref[...])


def _ffn(x, l, g_pre, w1, w2, g_post):
    bn, s, d = x.shape
    tm = min(ROW_GROUPS * TOKEN_TILE, s)
    dff = w1.shape[-1]
    lsel3 = lambda b, i: (l, 0, 0)
    tok = pl.BlockSpec((None, tm, d), lambda b, i: (b, i, 0))
    gain = pl.BlockSpec((None, 1, d), lsel3)
    return pl.pallas_call(
        _ffn_kernel,
        grid=(bn, s // tm),
        in_specs=[tok, gain, _const_spec((None, d, dff), lsel3),
                  _const_spec((None, dff, d), lsel3), gain],
        out_specs=tok,
        out_shape=jax.ShapeDtypeStruct((bn, s, d), jnp.float32),
        compiler_params=pltpu.CompilerParams(
            dimension_semantics=("arbitrary", "arbitrary"), vmem_limit_bytes=VMEM_LIMIT),
        name="ffn",
    )(x, g_pre, w1, w2, g_post)


def kernel(x, mem, pre_mix_g, w_in, gate_ln_g, gate_ln_b, w_s, b_s, w_dw, b_dw, conv_gn_g, conv_gn_b, w_out, post_mix_g, pre_x_g, mem_g, w_xq, w_xkv, w_xo, post_x_g, pre_ffn_g, w_ff1, w_ff2, post_ffn_g):
    depth, d, _ = w_in.shape
    a_width = gate_ln_g.shape[-1]
    c_width = w_dw.shape[-1]
    b_width = d - a_width - c_width
    a_heads = a_width // HEAD_DIM
    b_heads = b_width // HEAD_DIM
    assert w_in.shape[-1] == 2 * a_width + 3 * b_width + 2 * c_width
    assert w_s.shape[1:] == (a_heads, CHUNK, CHUNK) and w_dw.shape[1] == CONV_WIDTH
    assert x.shape[1] % MOBA_BLOCK == 0 and x.shape[1] % TOKEN_TILE == 0

    bf16 = lambda w: w.astype(jnp.bfloat16)
    row = lambda g: g[:, None, :]
    ws_cat = w_s.transpose(0, 2, 1, 3).reshape(depth, CHUNK, a_heads * CHUNK)
    bs_x = jnp.repeat(b_s.transpose(0, 2, 1), HEAD_DIM, axis=-1)
    slopes = 2.0 ** (-8.0 * jnp.arange(1, b_heads + 1, dtype=jnp.float32) / b_heads)
    w_in_b, w_out_b, w_xq_b, w_xkv_b, w_xo_b, w_ff1_b, w_ff2_b = map(
        bf16, (w_in, w_out, w_xq, w_xkv, w_xo, w_ff1, w_ff2))

    kx, vx = _mem_kv(mem, row(mem_g), w_xkv_b)
    for l in range(depth):
        ya, q, k, v, yc = _mix_in(
            x, l, row(pre_mix_g), w_in_b, row(gate_ln_g), row(gate_ln_b), ws_cat, bs_x, w_dw,
            row(b_dw), row(conv_gn_g), row(conv_gn_b),
            a_width=a_width, b_width=b_width, c_width=c_width)
        yb = _moba(q, k, v, slopes)
        x = _mix_out(x, ya, yb, yc, l, w_out_b, row(post_mix_g), row(pre_x_g), w_xq_b, kx, vx,
                     w_xo_b, row(post_x_g))
        x = _ffn(x, l, row(pre_ffn_g), w_ff1_b, w_ff2_b, row(post_ffn_g))
    return x
```

```python
import functools
import math

import jax
import jax.numpy as jnp
from jax import lax
from jax.experimental import pallas as pl
from jax.experimental.pallas import tpu as pltpu

HEAD_DIM = 64
CHUNK = 128
MOBA_BLOCK = 256
MOBA_TOPK = 3
CONV_WIDTH = 31
X_HEADS = 4
EPS = 1e-6
LOG2_E = math.log2(math.e)

LANES = 128
SUBLANES = 8
HALO = 32
CONV_ROWS = 128
TOKEN_TILE = 512
MIX_IN_TILE = 1024
ROW_GROUPS = 2
VMEM_LIMIT = 56 * 1024 * 1024

NEG_INF = float("-inf")


def _rms(x, g):
    return x * lax.rsqrt(jnp.mean(x * x, axis=-1, keepdims=True) + EPS) * g


def _const_spec(shape, index_map):
    return pl.BlockSpec(shape, index_map, pipeline_mode=pl.Buffered(1))


def _mem_kv_kernel(mem_ref, g_ref, w_ref, k_ref, v_ref):
    d = mem_ref.shape[-1]
    m = _rms(mem_ref[...], g_ref[...]).astype(jnp.bfloat16)
    kv = jnp.dot(m, w_ref[...], preferred_element_type=jnp.float32)
    k_ref[...] = kv[:, :d].astype(jnp.bfloat16)
    v_ref[...] = kv[:, d:].astype(jnp.bfloat16)


def _mem_kv(mem, mem_g, w_xkv):
    bn, m, d = mem.shape
    depth = w_xkv.shape[0]
    out = jax.ShapeDtypeStruct((depth, bn, m, d), jnp.bfloat16)
    return pl.pallas_call(
        _mem_kv_kernel,
        grid=(depth, bn),
        in_specs=[
            pl.BlockSpec((None, m, d), lambda l, b: (b, 0, 0)),
            pl.BlockSpec((None, 1, d), lambda l, b: (l, 0, 0)),
            pl.BlockSpec((None, d, 2 * d), lambda l, b: (l, 0, 0)),
        ],
        out_specs=[
            pl.BlockSpec((None, None, m, d), lambda l, b: (l, b, 0, 0)),
            pl.BlockSpec((None, None, m, d), lambda l, b: (l, b, 0, 0)),
        ],
        out_shape=[out, out],
        compiler_params=pltpu.CompilerParams(
            dimension_semantics=("arbitrary", "arbitrary"), vmem_limit_bytes=VMEM_LIMIT),
        name="mem_kv",
    )(mem, mem_g, w_xkv)


def _mix_in_kernel(x_ref, g_ref, win_ref, lng_ref, lnb_ref, ws_ref, bs_ref, wdw_ref, bdw_ref,
                   gng_ref, gnb_ref, ya_ref, q_ref, k_ref, v_ref, yc_ref, hist_ref, conv_ref, shift_ref,
                   *, a_width, b_width, c_width):
    tm = x_ref.shape[0]

    @pl.when(pl.program_id(1) == 0)
    def _():
        hist_ref[0:HALO, :] = jnp.zeros((HALO, c_width), jnp.float32)

    h = _rms(x_ref[...], g_ref[...]).astype(jnp.bfloat16)

    def project(lo, hi):
        return jnp.dot(h, win_ref[:, lo:hi], preferred_element_type=jnp.float32)

    o = 2 * a_width + 3 * b_width
    zc = project(o, o + 2 * c_width)
    hist_ref[HALO:HALO + tm, :] = zc[:, :c_width] * (1.0 / (1.0 + jnp.exp(-zc[:, c_width:])))

    o = 2 * a_width
    q = (project(o, o + b_width) * (HEAD_DIM ** -0.5 * LOG2_E)).astype(jnp.bfloat16)
    k = project(o + b_width, o + 2 * b_width).astype(jnp.bfloat16)
    v = project(o + 2 * b_width, o + 3 * b_width).astype(jnp.bfloat16)
    za = project(0, 2 * a_width)

    first = HALO - (CONV_WIDTH - 1)
    for c in range(tm // CONV_ROWS):
        r0 = c * CONV_ROWS
        acc = jnp.zeros((CONV_ROWS, c_width), jnp.float32)
        for phase in range(SUBLANES):
            taps = [j for j in range(CONV_WIDTH) if (first + j) % SUBLANES == phase]
            span = first + taps[-1] - phase
            rows = CONV_ROWS + span
            shift_ref[phase, 0:rows, :] = hist_ref[r0 + phase:r0 + phase + rows, :]
            for j in taps:
                off = first + j - phase
                acc = acc + wdw_ref[j:j + 1, :] * shift_ref[phase, off:off + CONV_ROWS, :]
        conv_ref[r0:r0 + CONV_ROWS, :] = acc + bdw_ref[...]
    hist_ref[0:HALO, :] = hist_ref[tm:tm + HALO, :]
    q_ref[...] = q
    k_ref[...] = k
    v_ref[...] = v

    ga = 0.5 * za * (1.0 + lax.erf(za * (2.0 ** -0.5)))
    u = ga[:, :a_width]
    vv = ga[:, a_width:]
    mu = jnp.mean(vv, axis=-1, keepdims=True)
    vc = vv - mu
    var = jnp.mean(vc * vc, axis=-1, keepdims=True)
    vn = (vc * lax.rsqrt(var + EPS) * lng_ref[...] + lnb_ref[...]).astype(jnp.bfloat16)
    n_heads = a_width // HEAD_DIM
    t_idx = lax.broadcasted_iota(jnp.int32, (CHUNK, n_heads * CHUNK), 0)
    s_idx = lax.broadcasted_iota(jnp.int32, (CHUNK, n_heads * CHUNK), 1) % CHUNK
    w_tril = jnp.where(s_idx <= t_idx, ws_ref[...], 0.0).astype(jnp.bfloat16)
    r_head = lax.broadcasted_iota(jnp.int32, (n_heads * CHUNK, a_width), 0) // CHUNK
    c_head = lax.broadcasted_iota(jnp.int32, (n_heads * CHUNK, a_width), 1) // HEAD_DIM
    head_match = r_head == c_head
    for c in range(tm // CHUNK):
        rows = slice(c * CHUNK, (c + 1) * CHUNK)
        v_chunk = vn[rows]
        v_bd = jnp.where(head_match, jnp.concatenate([v_chunk] * n_heads, axis=0), 0.0)
        mixed = jnp.dot(w_tril, v_bd.astype(jnp.bfloat16), preferred_element_type=jnp.float32)
        ya_ref[rows, :] = (u[rows] * (mixed + bs_ref[...])).astype(jnp.bfloat16)

    gr = lax.broadcasted_iota(jnp.int32, (2 * c_width, c_width), 0) % c_width // HEAD_DIM
    gc = lax.broadcasted_iota(jnp.int32, (2 * c_width, c_width), 1) // HEAD_DIM
    gmat2 = jnp.where(gr == gc, 1.0 / HEAD_DIM, 0.0).astype(jnp.bfloat16)

    def group_mean(a):
        hi = a.astype(jnp.bfloat16)
        lo = (a - hi.astype(jnp.float32)).astype(jnp.bfloat16)
        return jnp.dot(jnp.concatenate([hi, lo], axis=-1), gmat2, preferred_element_type=jnp.float32)

    y = conv_ref[...]
    yd = y - group_mean(y)
    yn = yd * lax.rsqrt(group_mean(yd * yd) + EPS) * gng_ref[...] + gnb_ref[...]
    yc_ref[...] = (yn * (1.0 / (1.0 + jnp.exp(-yn)))).astype(jnp.bfloat16)


def _mix_in(x, l, pre_g, w_in, ln_g, ln_b, ws_cat, bs_x, w_dw, b_dw, gn_g, gn_b,
            *, a_width, b_width, c_width):
    bn, s, d = x.shape
    tm = min(MIX_IN_TILE, s)
    in_cols = w_in.shape[-1]
    n_heads = a_width // HEAD_DIM
    lsel3 = lambda b, i: (l, 0, 0)
    tok = lambda w: pl.BlockSpec((None, tm, w), lambda b, i: (b, i, 0))
    bf = lambda w: jax.ShapeDtypeStruct((bn, s, w), jnp.bfloat16)
    return pl.pallas_call(
        functools.partial(_mix_in_kernel, a_width=a_width, b_width=b_width, c_width=c_width),
        grid=(bn, s // tm),
        in_specs=[
            tok(d),
            pl.BlockSpec((None, 1, d), lsel3),
            _const_spec((None, d, in_cols), lambda b, i: (0, 0, 0)),
            pl.BlockSpec((None, 1, a_width), lsel3),
            pl.BlockSpec((None, 1, a_width), lsel3),
            pl.BlockSpec((None, CHUNK, n_heads * CHUNK), lsel3),
            pl.BlockSpec((None, CHUNK, a_width), lsel3),
            pl.BlockSpec((None, CONV_WIDTH, c_width), lsel3),
            pl.BlockSpec((None, 1, c_width), lsel3),
            pl.BlockSpec((None, 1, c_width), lsel3),
            pl.BlockSpec((None, 1, c_width), lsel3),
        ],
        out_specs=[tok(a_width), tok(b_width), tok(b_width), tok(b_width), tok(c_width)],
        out_shape=[bf(a_width), bf(b_width), bf(b_width), bf(b_width), bf(c_width)],
        scratch_shapes=[pltpu.VMEM((tm + HALO, c_width), jnp.float32),
                        pltpu.VMEM((tm, c_width), jnp.float32),
                        pltpu.VMEM((SUBLANES, CONV_ROWS + HALO, c_width), jnp.float32)],
        compiler_params=pltpu.CompilerParams(
            dimension_semantics=("arbitrary", "arbitrary"), vmem_limit_bytes=VMEM_LIMIT),
        name="mix_in",
    )(x, pre_g, w_in, ln_g, ln_b, ws_cat, bs_x, w_dw, b_dw, gn_g, gn_b)


ALIBI_TERMS = 4
ALIBI_PIECES = 3
ALIBI_ROWS = 16
HEADS_PER_TILE = LANES // HEAD_DIM
BF16_SUBLANES = 16
PV_ROWS = HEAD_DIM + BF16_SUBLANES
GATE_PIECES = 3
SELECT_TILES = 4


def _own_lanes(h, idx):
    hh = h % HEADS_PER_TILE
    return (idx >= hh * HEAD_DIM) & (idx < (hh + 1) * HEAD_DIM)


def _slot_base(h):
    return ((h + 1) % HEADS_PER_TILE) * HEAD_DIM


def _moba_kernel(slopes_ref, q_ref, k_ref, v_ref, o_ref, km_ref, kaug_ref, vt_ref, qaug_ref, neg_ref,
                 s_ref, p_ref, acc_ref, *, n_blocks):
    blk = MOBA_BLOCK
    n_tiles = q_ref.shape[-1] // LANES

    def prepare():
        lane = lax.broadcasted_iota(jnp.int32, (blk, LANES), 1)
        j_pos = lax.broadcasted_iota(jnp.int32, (blk, LANES), 0).astype(jnp.float32)
        ones_rows = (lax.broadcasted_iota(jnp.int32, (BF16_SUBLANES, blk), 0) == 0
                     ).astype(jnp.float32)
        km_row = lax.broadcasted_iota(jnp.int32, (n_blocks, LANES), 0)
        km_lane = lax.broadcasted_iota(jnp.int32, (n_blocks, LANES), 1)
        for t in range(n_tiles):
            cols = slice(t * LANES, (t + 1) * LANES)
            kmean = jnp.zeros((n_blocks, LANES), jnp.float32)
            for n in range(n_blocks):
                rows = slice(n * blk, (n + 1) * blk)
                kf = k_ref[rows, cols].astype(jnp.float32)
                kmean = jnp.where(km_row == n, jnp.sum(kf, axis=0, keepdims=True) * (1.0 / blk), kmean)
                v_t = v_ref[rows, cols].astype(jnp.float32).T
                for hh in range(HEADS_PER_TILE):
                    h = t * HEADS_PER_TILE + hh
                    term = (lane - _slot_base(h)) // ALIBI_PIECES
                    aug = jnp.where(term == 0, j_pos,
                                    jnp.where(term == 1, float(n),
                                              jnp.where((term == 2) | (term == 3), 1.0, 0.0)))
                    kaug_ref[h, n] = jnp.where(_own_lanes(h, lane), kf, aug).astype(jnp.bfloat16)
                    vt_ref[h, n] = jnp.concatenate(
                        [v_t[hh * HEAD_DIM:(hh + 1) * HEAD_DIM], ones_rows], axis=0).astype(jnp.bfloat16)
            for hh in range(HEADS_PER_TILE):
                h = t * HEADS_PER_TILE + hh
                rest = jnp.where(_own_lanes(h, km_lane), kmean, 0.0)
                pieces = []
                for _ in range(GATE_PIECES):
                    piece = rest.astype(jnp.bfloat16).astype(jnp.float32)
                    pieces.append(piece)
                    rest = rest - piece
                km_ref[h] = jnp.concatenate(pieces[::-1], axis=0).astype(jnp.bfloat16)

    prepare()

    def select(g, carry):
        _moba_select(g, slopes_ref, q_ref, km_ref, qaug_ref, neg_ref, n_blocks=n_blocks)
        return carry

    lax.fori_loop(0, n_blocks // SELECT_TILES, select, 0)

    _moba_open(0, kaug_ref, qaug_ref, s_ref)

    def sweep(qi, carry):
        _moba_tile(qi, o_ref, kaug_ref, vt_ref, qaug_ref, neg_ref, s_ref, p_ref, acc_ref,
                   n_blocks=n_blocks)
        _moba_open(jnp.minimum(qi + 1, n_blocks - 1), kaug_ref, qaug_ref, s_ref)
        return carry

    lax.fori_loop(0, n_blocks, sweep, 0)


def _moba_select(g, slopes_ref, q_ref, km_ref, qaug_ref, neg_ref, *, n_blocks):
    blk = MOBA_BLOCK
    width = SELECT_TILES * blk
    n_tiles = q_ref.shape[-1] // LANES
    n_heads = n_tiles * HEADS_PER_TILE
    hg = pl.program_id(1)
    first_tile = jnp.asarray(g, jnp.int32) * SELECT_TILES
    q_rows = pl.ds(pl.multiple_of(first_tile * blk, width), width)

    def tile_of(shape):
        return first_tile + lax.broadcasted_iota(jnp.int32, shape, 1) // blk

    slot = lax.broadcasted_iota(jnp.int32, (ALIBI_ROWS, width), 0)
    i_pos = (lax.broadcasted_iota(jnp.int32, (ALIBI_ROWS, width), 1) % blk).astype(jnp.float32)
    tile_f = tile_of((ALIBI_ROWS, width)).astype(jnp.float32)
    pad = jnp.zeros((HEAD_DIM - ALIBI_ROWS, width), jnp.float32)
    blk_id = lax.broadcasted_iota(jnp.int32, (n_blocks, width), 0)
    past = blk_id < tile_of((n_blocks, width))
    for t in range(n_tiles):
        q_t = q_ref[q_rows, t * LANES:(t + 1) * LANES].astype(jnp.float32).T
        for hh in range(HEADS_PER_TILE):
            h = t * HEADS_PER_TILE + hh
            c = slopes_ref[hg * n_heads + h] * LOG2_E
            term = slot // ALIBI_PIECES
            full = jnp.where(term == 0, c,
                             jnp.where(term == 1, c * blk,
                                       jnp.where(term == 2, -(c * (blk * tile_f)),
                                                 jnp.where(term == 3, -(c * i_pos), 0.0))))
            piece = full.astype(jnp.bfloat16).astype(jnp.float32)
            aug = piece
            for r in range(1, ALIBI_PIECES):
                full = full - piece
                piece = full.astype(jnp.bfloat16).astype(jnp.float32)
                aug = jnp.where(slot % ALIBI_PIECES == r, piece, aug)
            own = q_t[hh * HEAD_DIM:(hh + 1) * HEAD_DIM]
            rows = [own, aug, pad] if _slot_base(h) > 0 else [aug, pad, own]
            q_aug = jnp.concatenate(rows, axis=0).astype(jnp.bfloat16)
            for j in range(SELECT_TILES):
                qaug_ref[h, first_tile + j] = q_aug[:, j * blk:(j + 1) * blk]

            g3 = jnp.dot(km_ref[h], q_aug, preferred_element_type=jnp.float32)
            gate = g3[:n_blocks] + g3[n_blocks:2 * n_blocks] + g3[2 * n_blocks:]
            gate = jnp.where(past, gate, NEG_INF)
            blk_f = blk_id.astype(jnp.float32)
            chosen = jnp.zeros((n_blocks, width), jnp.float32)
            for _ in range(MOBA_TOPK):
                best = jnp.max(gate, axis=0, keepdims=True)
                pick = jnp.min(jnp.where(gate == best, blk_f, float(n_blocks)), axis=0, keepdims=True)
                taken = blk_f == pick
                chosen = jnp.where(taken, 1.0, chosen)
                gate = jnp.where(taken, NEG_INF, gate)
            neg = jnp.where((chosen > 0.5) & past, 0.0, NEG_INF)
            for j in range(SELECT_TILES):
                neg_ref[h, first_tile + j, 0] = jnp.zeros((1, blk), jnp.float32)
                for u in range(1, n_blocks):
                    neg_ref[h, first_tile + j, u] = neg[u - 1:u, j * blk:(j + 1) * blk]


def _moba_open(qi, kaug_ref, qaug_ref, s_ref):
    blk = MOBA_BLOCK
    key_i = lax.broadcasted_iota(jnp.int32, (blk, blk), 0)
    qry_i = lax.broadcasted_iota(jnp.int32, (blk, blk), 1)
    causal = key_i <= qry_i
    for h in range(kaug_ref.shape[0]):
        s = jnp.dot(kaug_ref[h, qi], qaug_ref[h, qi], preferred_element_type=jnp.float32)
        s_ref[0, h] = jnp.where(causal, s, NEG_INF)


def _moba_tile(qi, o_ref, kaug_ref, vt_ref, qaug_ref, neg_ref, s_ref, p_ref, acc_ref, *, n_blocks):
    blk = MOBA_BLOCK
    n_heads = kaug_ref.shape[0]
    n_tiles = n_heads // HEADS_PER_TILE
    q_rows = pl.ds(pl.multiple_of(qi * blk, blk), blk)
    q_aug = [qaug_ref[h, qi] for h in range(n_heads)]
    for h in range(n_heads):
        p_ref[1, h] = jnp.zeros((blk, blk), jnp.bfloat16)
        acc_ref[h] = jnp.zeros((PV_ROWS, blk), jnp.float32)

    def block_of(u):
        return jnp.where(u == 0, qi, jnp.minimum(u - 1, n_blocks - 1))

    def step(u, cur, m_runs):
        b_prev = block_of(jnp.maximum(u - 1, 0))
        k_next = jnp.minimum(u, n_blocks - 1)
        pv = [jnp.dot(vt_ref[h, b_prev], p_ref[1 - cur, h], preferred_element_type=jnp.float32)
              for h in range(n_heads)]
        for h in range(n_heads):
            s_ref[1 - cur, h] = jnp.dot(kaug_ref[h, k_next], q_aug[h],
                                        preferred_element_type=jnp.float32)
        m_news = []
        for h in range(n_heads):
            s = s_ref[cur, h]
            neg = neg_ref[h, qi, u]
            m_new = jnp.maximum(m_runs[h], jnp.max(s, axis=0, keepdims=True) + neg)
            alpha = jnp.exp2(m_runs[h] - m_new)
            p = jnp.exp2(s - (m_new - neg))
            p_ref[cur, h] = p.astype(jnp.bfloat16)
            acc_ref[h] = alpha * (acc_ref[h] + pv[h])
            m_news.append(m_new)
        return tuple(m_news)

    def step_pair(i, m_runs):
        return step(2 * i + 1, 1, step(2 * i, 0, m_runs))

    m_init = jnp.full((1, blk), NEG_INF, jnp.float32)
    lax.fori_loop(0, (qi + 2) // 2, step_pair, (m_init,) * n_heads)
    last = 2 * ((qi + 2) // 2) - 1
    for t in range(n_tiles):
        outs = []
        for h in range(t * HEADS_PER_TILE, (t + 1) * HEADS_PER_TILE):
            acc = acc_ref[h] + jnp.dot(vt_ref[h, block_of(last)], p_ref[1, h],
                                       preferred_element_type=jnp.float32)
            outs.append(acc[:HEAD_DIM] / acc[HEAD_DIM:HEAD_DIM + 1])
        o_ref[q_rows, t * LANES:(t + 1) * LANES] = jnp.concatenate(outs, axis=0).T.astype(jnp.bfloat16)


MOBA_TILES_PER_STEP = 2


def _moba(q, k, v, slopes):
    bn, s, width = q.shape
    step_w = MOBA_TILES_PER_STEP * LANES
    assert s % MOBA_BLOCK == 0 and width % step_w == 0
    n_blocks = s // MOBA_BLOCK
    n_heads = MOBA_TILES_PER_STEP * HEADS_PER_TILE
    assert ALIBI_TERMS * ALIBI_PIECES <= ALIBI_ROWS <= HEAD_DIM and HEADS_PER_TILE == 2
    assert n_blocks <= 256 and MOBA_BLOCK <= 256
    seq = pl.BlockSpec((None, s, step_w), lambda b, hg: (b, 0, hg))
    return pl.pallas_call(
        functools.partial(_moba_kernel, n_blocks=n_blocks),
        grid=(bn, width // step_w),
        in_specs=[pl.BlockSpec(memory_space=pltpu.SMEM), seq, seq, seq],
        out_specs=seq,
        out_shape=jax.ShapeDtypeStruct((bn, s, width), jnp.bfloat16),
        scratch_shapes=[
            pltpu.VMEM((n_heads, GATE_PIECES * n_blocks, LANES), jnp.bfloat16),
            pltpu.VMEM((n_heads, n_blocks, MOBA_BLOCK, LANES), jnp.bfloat16),
            pltpu.VMEM((n_heads, n_blocks, PV_ROWS, MOBA_BLOCK), jnp.bfloat16),
            pltpu.VMEM((n_heads, n_blocks, LANES, MOBA_BLOCK), jnp.bfloat16),
            pltpu.VMEM((n_heads, n_blocks, n_blocks, 1, MOBA_BLOCK), jnp.float32),
            pltpu.VMEM((2, n_heads, MOBA_BLOCK, MOBA_BLOCK), jnp.float32),
            pltpu.VMEM((2, n_heads, MOBA_BLOCK, MOBA_BLOCK), jnp.bfloat16),
            pltpu.VMEM((n_heads, PV_ROWS, MOBA_BLOCK), jnp.float32),
        ],
        compiler_params=pltpu.CompilerParams(
            dimension_semantics=("arbitrary", "arbitrary"),
            vmem_limit_bytes=VMEM_LIMIT),
        name="moba",
    )(slopes, q, k, v)


def _mix_out_kernel(x_ref, ya_ref, yb_ref, yc_ref, wout_ref, gpost_ref, gprex_ref, wq_ref,
                    kx_ref, vx_ref, wo_ref, gpostx_ref, *rest):
    n_cast = len(rest) // 2
    o_ref = rest[n_cast]
    for src_ref, dst_ref in zip(rest[:n_cast], rest[n_cast + 1:]):
        dst_ref[...] = src_ref[...].astype(jnp.bfloat16)

    d = x_ref.shape[-1]
    dh = d // X_HEADS
    nt = (((1,), (1,)), ((), ()))
    rows = x_ref.shape[0] // ROW_GROUPS
    groups = [slice(r * rows, (r + 1) * rows) for r in range(ROW_GROUPS)]

    ys = [jnp.concatenate([ya_ref[g, :], yb_ref[g, :], yc_ref[g, :]], axis=-1) for g in groups]
    mixes = [jnp.dot(y, wout_ref[...], preferred_element_type=jnp.float32) for y in ys]
    x1s = [x_ref[g, :] + _rms(mix, gpost_ref[...]) for g, mix in zip(groups, mixes)]
    hs = [_rms(x1, gprex_ref[...]).astype(jnp.bfloat16) for x1 in x1s]
    qxs = [(jnp.dot(h, wq_ref[...], preferred_element_type=jnp.float32) * (dh ** -0.5)
            ).astype(jnp.bfloat16) for h in hs]
    outs = []
    for qx in qxs:
        heads = []
        for hh in range(X_HEADS):
            cols = slice(hh * dh, (hh + 1) * dh)
            s = lax.dot_general(qx[:, cols], kx_ref[:, cols], nt, preferred_element_type=jnp.float32)
            p = jnp.exp(s - jnp.max(s, axis=-1, keepdims=True))
            denom = jnp.sum(p, axis=-1, keepdims=True)
            oh = jnp.dot(p.astype(jnp.bfloat16), vx_ref[:, cols], preferred_element_type=jnp.float32)
            heads.append((oh / denom).astype(jnp.bfloat16))
        outs.append(jnp.concatenate(heads, axis=-1))
    atts = [jnp.dot(o, wo_ref[...], preferred_element_type=jnp.float32) for o in outs]
    for g, x1, att in zip(groups, x1s, atts):
        o_ref[g, :] = x1 + _rms(att, gpostx_ref[...])


def _mix_out(x, ya, yb, yc, l, w_out, g_post, g_prex, w_xq, kx, vx, w_xo, g_postx, next_weights):
    bn, s, d = x.shape
    tm = min(ROW_GROUPS * TOKEN_TILE, s)
    m = kx.shape[2]
    n_i = s // tm
    n_steps = bn * n_i
    lsel3 = lambda b, i: (l, 0, 0)
    first3 = lambda b, i: (0, 0, 0)
    tok = lambda w: pl.BlockSpec((None, tm, w), lambda b, i: (b, i, 0))
    gain = pl.BlockSpec((None, 1, d), lsel3)
    wsq = _const_spec((None, d, d), first3)
    mem = pl.BlockSpec((None, None, m, d), lambda b, i: (l, b, 0, 0))
    cast_in, cast_out, cast_shape = [], [], []
    for w in next_weights:
        _, rows, cols = w.shape
        assert rows % (n_steps * BF16_SUBLANES) == 0
        cast_in.append(pl.BlockSpec((None, rows // n_steps, cols),
                                    lambda b, i: (l + 1, b * n_i + i, 0)))
        cast_out.append(pl.BlockSpec((None, rows // n_steps, cols), lambda b, i: (0, b * n_i + i, 0)))
        cast_shape.append(jax.ShapeDtypeStruct((1, rows, cols), jnp.bfloat16))
    out = pl.pallas_call(
        _mix_out_kernel,
        grid=(bn, n_i),
        in_specs=[tok(d), tok(ya.shape[-1]), tok(yb.shape[-1]), tok(yc.shape[-1]),
                  wsq, gain, gain, wsq, mem, mem, wsq, gain] + cast_in,
        out_specs=[tok(d)] + cast_out,
        out_shape=[jax.ShapeDtypeStruct((bn, s, d), jnp.float32)] + cast_shape,
        compiler_params=pltpu.CompilerParams(
            dimension_semantics=("arbitrary", "arbitrary"), vmem_limit_bytes=VMEM_LIMIT),
        name="mix_out",
    )(x, ya, yb, yc, w_out, g_post, g_prex, w_xq, kx, vx, w_xo, g_postx, *next_weights)
    return out[0], tuple(out[1:])


def _ffn_kernel(x_ref, gpre_ref, w1_ref, w2_ref, gpost_ref, o_ref):
    rows = x_ref.shape[0] // ROW_GROUPS
    for r in range(ROW_GROUPS):
        g = slice(r * rows, (r + 1) * rows)
        x = x_ref[g, :]
        h = _rms(x, gpre_ref[...]).astype(jnp.bfloat16)
        a = jnp.maximum(jnp.dot(h, w1_ref[...], preferred_element_type=jnp.float32), 0.0)
        f = jnp.dot((a * a).astype(jnp.bfloat16), w2_ref[...], preferred_element_type=jnp.float32)
        o_ref[g, :] = x + _rms(f, gpost_ref[...])


def _ffn(x, l, g_pre, w1, w2, g_post):
    bn, s, d = x.shape
    tm = min(ROW_GROUPS * TOKEN_TILE, s)
    dff = w1.shape[-1]
    lsel3 = lambda b, i: (l, 0, 0)
    first3 = lambda b, i: (0, 0, 0)
    tok = pl.BlockSpec((None, tm, d), lambda b, i: (b, i, 0))
    gain = pl.BlockSpec((None, 1, d), lsel3)
    return pl.pallas_call(
        _ffn_kernel,
        grid=(bn, s // tm),
        in_specs=[tok, gain, _const_spec((None, d, dff), first3),
                  _const_spec((None, dff, d), first3), gain],
        out_specs=tok,
        out_shape=jax.ShapeDtypeStruct((bn, s, d), jnp.float32),
        compiler_params=pltpu.CompilerParams(
            dimension_semantics=("arbitrary", "arbitrary"), vmem_limit_bytes=VMEM_LIMIT),
        name="ffn",
    )(x, g_pre, w1, w2, g_post)


def kernel(x, mem, pre_mix_g, w_in, gate_ln_g, gate_ln_b, w_s, b_s, w_dw, b_dw, conv_gn_g, conv_gn_b, w_out, post_mix_g, pre_x_g, mem_g, w_xq, w_xkv, w_xo, post_x_g, pre_ffn_g, w_ff1, w_ff2, post_ffn_g):
    depth, d, _ = w_in.shape
    a_width = gate_ln_g.shape[-1]
    c_width = w_dw.shape[-1]
    b_width = d - a_width - c_width
    a_heads = a_width // HEAD_DIM
    b_heads = b_width // HEAD_DIM
    assert w_in.shape[-1] == 2 * a_width + 3 * b_width + 2 * c_width
    assert w_s.shape[1:] == (a_heads, CHUNK, CHUNK) and w_dw.shape[1] == CONV_WIDTH
    assert x.shape[1] % MOBA_BLOCK == 0 and x.shape[1] % TOKEN_TILE == 0

    bf16 = lambda w: w.astype(jnp.bfloat16)
    row = lambda g: g[:, None, :]
    ws_cat = w_s.transpose(0, 2, 1, 3).reshape(depth, CHUNK, a_heads * CHUNK)
    bs_x = jnp.repeat(b_s.transpose(0, 2, 1), HEAD_DIM, axis=-1)
    slopes = 2.0 ** (-8.0 * jnp.arange(1, b_heads + 1, dtype=jnp.float32) / b_heads)
    layer_weights = (w_in, w_out, w_xq, w_xo, w_ff1, w_ff2)
    w_in_l, w_out_l, w_xq_l, w_xo_l, w_ff1_l, w_ff2_l = (bf16(w[:1]) for w in layer_weights)

    kx, vx = _mem_kv(mem, row(mem_g), bf16(w_xkv))
    for l in range(depth):
        ya, q, k, v, yc = _mix_in(
            x, l, row(pre_mix_g), w_in_l, row(gate_ln_g), row(gate_ln_b), ws_cat, bs_x, w_dw,
            row(b_dw), row(conv_gn_g), row(conv_gn_b),
            a_width=a_width, b_width=b_width, c_width=c_width)
        yb = _moba(q, k, v, slopes)
        x, next_l = _mix_out(x, ya, yb, yc, l, w_out_l, row(post_mix_g), row(pre_x_g), w_xq_l, kx, vx,
                             w_xo_l, row(post_x_g), layer_weights if l + 1 < depth else ())
        x = _ffn(x, l, row(pre_ffn_g), w_ff1_l, w_ff2_l, row(post_ffn_g))
        if next_l:
            w_in_l, w_out_l, w_xq_l, w_xo_l, w_ff1_l, w_ff2_l = next_l
    return x
```

```python
import functools
import math

import jax
import jax.numpy as jnp
from jax import lax
from jax.experimental import pallas as pl
from jax.experimental.pallas import tpu as pltpu

HEAD_DIM = 64
CHUNK = 128
MOBA_BLOCK = 256
MOBA_TOPK = 3
CONV_WIDTH = 31
X_HEADS = 4
EPS = 1e-6
LOG2_E = math.log2(math.e)

LANES = 128
SUBLANES = 8
HALO = 32
CONV_ROWS = 128
TOKEN_TILE = 512
MIX_IN_TILE = 1024
ROW_GROUPS = 2
VMEM_LIMIT = 56 * 1024 * 1024

NEG_INF = float("-inf")


def _rms(x, g):
    return x * lax.rsqrt(jnp.mean(x * x, axis=-1, keepdims=True) + EPS) * g


def _const_spec(shape, index_map):
    return pl.BlockSpec(shape, index_map, pipeline_mode=pl.Buffered(1))


def _cast_rider_specs(weights, layer, n_steps, step_of):
    in_specs, out_specs, out_shapes = [], [], []
    for w in weights:
        _, rows, cols = w.shape
        assert rows % (n_steps * BF16_SUBLANES) == 0
        chunk = (None, rows // n_steps, cols)
        in_specs.append(pl.BlockSpec(chunk, lambda *g: (layer, step_of(*g), 0)))
        out_specs.append(pl.BlockSpec(chunk, lambda *g: (0, step_of(*g), 0)))
        out_shapes.append(jax.ShapeDtypeStruct((1, rows, cols), jnp.bfloat16))
    return in_specs, out_specs, out_shapes


def _cast_riders(src_refs, dst_refs):
    for src_ref, dst_ref in zip(src_refs, dst_refs):
        dst_ref[...] = src_ref[...].astype(jnp.bfloat16)


def _mem_kv_kernel(mem_ref, g_ref, w_ref, *rest):
    n_cast = (len(rest) - 2) // 2
    k_ref, v_ref = rest[n_cast:n_cast + 2]
    _cast_riders(rest[:n_cast], rest[n_cast + 2:])
    d = mem_ref.shape[-1]
    m = _rms(mem_ref[...], g_ref[...]).astype(jnp.bfloat16)
    kv = jnp.dot(m, w_ref[...].astype(jnp.bfloat16), preferred_element_type=jnp.float32)
    k_ref[...] = kv[:, :d].astype(jnp.bfloat16)
    v_ref[...] = kv[:, d:].astype(jnp.bfloat16)


def _mem_kv(mem, mem_g, w_xkv, layer_weights):
    bn, m, d = mem.shape
    depth = w_xkv.shape[0]
    out = jax.ShapeDtypeStruct((depth, bn, m, d), jnp.bfloat16)
    cast_in, cast_out, cast_shape = _cast_rider_specs(
        layer_weights, 0, depth * bn, lambda l, b: l * bn + b)
    res = pl.pallas_call(
        _mem_kv_kernel,
        grid=(depth, bn),
        in_specs=[
            pl.BlockSpec((None, m, d), lambda l, b: (b, 0, 0)),
            pl.BlockSpec((None, 1, d), lambda l, b: (l, 0, 0)),
            pl.BlockSpec((None, d, 2 * d), lambda l, b: (l, 0, 0)),
        ] + cast_in,
        out_specs=[
            pl.BlockSpec((None, None, m, d), lambda l, b: (l, b, 0, 0)),
            pl.BlockSpec((None, None, m, d), lambda l, b: (l, b, 0, 0)),
        ] + cast_out,
        out_shape=[out, out] + cast_shape,
        compiler_params=pltpu.CompilerParams(
            dimension_semantics=("arbitrary", "arbitrary"), vmem_limit_bytes=VMEM_LIMIT),
        name="mem_kv",
    )(mem, mem_g, w_xkv, *layer_weights)
    return res[0], res[1], tuple(res[2:])


def _mix_in_kernel(x_ref, g_ref, win_ref, lng_ref, lnb_ref, ws_ref, bs_ref, wdw_ref, bdw_ref,
                   gng_ref, gnb_ref, ya_ref, q_ref, k_ref, v_ref, yc_ref, hist_ref, conv_ref, shift_ref,
                   *, a_width, b_width, c_width):
    tm = x_ref.shape[0]

    @pl.when(pl.program_id(1) == 0)
    def _():
        hist_ref[0:HALO, :] = jnp.zeros((HALO, c_width), jnp.float32)

    h = _rms(x_ref[...], g_ref[...]).astype(jnp.bfloat16)

    def project(lo, hi):
        return jnp.dot(h, win_ref[:, lo:hi], preferred_element_type=jnp.float32)

    o = 2 * a_width + 3 * b_width
    zc = project(o, o + 2 * c_width)
    hist_ref[HALO:HALO + tm, :] = zc[:, :c_width] * (1.0 / (1.0 + jnp.exp(-zc[:, c_width:])))

    o = 2 * a_width
    q = (project(o, o + b_width) * (HEAD_DIM ** -0.5 * LOG2_E)).astype(jnp.bfloat16)
    k = project(o + b_width, o + 2 * b_width).astype(jnp.bfloat16)
    v = project(o + 2 * b_width, o + 3 * b_width).astype(jnp.bfloat16)
    za = project(0, 2 * a_width)

    first = HALO - (CONV_WIDTH - 1)
    for c in range(tm // CONV_ROWS):
        r0 = c * CONV_ROWS
        acc = jnp.zeros((CONV_ROWS, c_width), jnp.float32)
        for phase in range(SUBLANES):
            taps = [j for j in range(CONV_WIDTH) if (first + j) % SUBLANES == phase]
            span = first + taps[-1] - phase
            rows = CONV_ROWS + span
            shift_ref[phase, 0:rows, :] = hist_ref[r0 + phase:r0 + phase + rows, :]
            for j in taps:
                off = first + j - phase
                acc = acc + wdw_ref[j:j + 1, :] * shift_ref[phase, off:off + CONV_ROWS, :]
        conv_ref[r0:r0 + CONV_ROWS, :] = acc + bdw_ref[...]
    hist_ref[0:HALO, :] = hist_ref[tm:tm + HALO, :]
    q_ref[...] = q
    k_ref[...] = k
    v_ref[...] = v

    ga = 0.5 * za * (1.0 + lax.erf(za * (2.0 ** -0.5)))
    u = ga[:, :a_width]
    vv = ga[:, a_width:]
    mu = jnp.mean(vv, axis=-1, keepdims=True)
    vc = vv - mu
    var = jnp.mean(vc * vc, axis=-1, keepdims=True)
    vn = (vc * lax.rsqrt(var + EPS) * lng_ref[...] + lnb_ref[...]).astype(jnp.bfloat16)
    n_heads = a_width // HEAD_DIM
    t_idx = lax.broadcasted_iota(jnp.int32, (CHUNK, n_heads * CHUNK), 0)
    s_idx = lax.broadcasted_iota(jnp.int32, (CHUNK, n_heads * CHUNK), 1) % CHUNK
    w_tril = jnp.where(s_idx <= t_idx, ws_ref[...], 0.0).astype(jnp.bfloat16)
    r_head = lax.broadcasted_iota(jnp.int32, (n_heads * CHUNK, a_width), 0) // CHUNK
    c_head = lax.broadcasted_iota(jnp.int32, (n_heads * CHUNK, a_width), 1) // HEAD_DIM
    head_match = r_head == c_head
    for c in range(tm // CHUNK):
        rows = slice(c * CHUNK, (c + 1) * CHUNK)
        v_chunk = vn[rows]
        v_bd = jnp.where(head_match, jnp.concatenate([v_chunk] * n_heads, axis=0), 0.0)
        mixed = jnp.dot(w_tril, v_bd.astype(jnp.bfloat16), preferred_element_type=jnp.float32)
        ya_ref[rows, :] = (u[rows] * (mixed + bs_ref[...])).astype(jnp.bfloat16)

    gr = lax.broadcasted_iota(jnp.int32, (2 * c_width, c_width), 0) % c_width // HEAD_DIM
    gc = lax.broadcasted_iota(jnp.int32, (2 * c_width, c_width), 1) // HEAD_DIM
    gmat2 = jnp.where(gr == gc, 1.0 / HEAD_DIM, 0.0).astype(jnp.bfloat16)

    def group_mean(a):
        hi = a.astype(jnp.bfloat16)
        lo = (a - hi.astype(jnp.float32)).astype(jnp.bfloat16)
        return jnp.dot(jnp.concatenate([hi, lo], axis=-1), gmat2, preferred_element_type=jnp.float32)

    y = conv_ref[...]
    yd = y - group_mean(y)
    yn = yd * lax.rsqrt(group_mean(yd * yd) + EPS) * gng_ref[...] + gnb_ref[...]
    yc_ref[...] = (yn * (1.0 / (1.0 + jnp.exp(-yn)))).astype(jnp.bfloat16)


def _mix_in(x, l, pre_g, w_in, ln_g, ln_b, ws_cat, bs_x, w_dw, b_dw, gn_g, gn_b,
            *, a_width, b_width, c_width):
    bn, s, d = x.shape
    tm = min(MIX_IN_TILE, s)
    in_cols = w_in.shape[-1]
    n_heads = a_width // HEAD_DIM
    lsel3 = lambda b, i: (l, 0, 0)
    tok = lambda w: pl.BlockSpec((None, tm, w), lambda b, i: (b, i, 0))
    bf = lambda w: jax.ShapeDtypeStruct((bn, s, w), jnp.bfloat16)
    return pl.pallas_call(
        functools.partial(_mix_in_kernel, a_width=a_width, b_width=b_width, c_width=c_width),
        grid=(bn, s // tm),
        in_specs=[
            tok(d),
            pl.BlockSpec((None, 1, d), lsel3),
            _const_spec((None, d, in_cols), lambda b, i: (0, 0, 0)),
            pl.BlockSpec((None, 1, a_width), lsel3),
            pl.BlockSpec((None, 1, a_width), lsel3),
            pl.BlockSpec((None, CHUNK, n_heads * CHUNK), lsel3),
            pl.BlockSpec((None, CHUNK, a_width), lsel3),
            pl.BlockSpec((None, CONV_WIDTH, c_width), lsel3),
            pl.BlockSpec((None, 1, c_width), lsel3),
            pl.BlockSpec((None, 1, c_width), lsel3),
            pl.BlockSpec((None, 1, c_width), lsel3),
        ],
        out_specs=[tok(a_width), tok(b_width), tok(b_width), tok(b_width), tok(c_width)],
        out_shape=[bf(a_width), bf(b_width), bf(b_width), bf(b_width), bf(c_width)],
        scratch_shapes=[pltpu.VMEM((tm + HALO, c_width), jnp.float32),
                        pltpu.VMEM((tm, c_width), jnp.float32),
                        pltpu.VMEM((SUBLANES, CONV_ROWS + HALO, c_width), jnp.float32)],
        compiler_params=pltpu.CompilerParams(
            dimension_semantics=("arbitrary", "arbitrary"), vmem_limit_bytes=VMEM_LIMIT),
        name="mix_in",
    )(x, pre_g, w_in, ln_g, ln_b, ws_cat, bs_x, w_dw, b_dw, gn_g, gn_b)


ALIBI_TERMS = 4
ALIBI_PIECES = 3
ALIBI_ROWS = 16
HEADS_PER_TILE = LANES // HEAD_DIM
BF16_SUBLANES = 16
PV_ROWS = HEAD_DIM + BF16_SUBLANES
GATE_PIECES = 3
SELECT_TILES = 4


def _own_lanes(h, idx):
    hh = h % HEADS_PER_TILE
    return (idx >= hh * HEAD_DIM) & (idx < (hh + 1) * HEAD_DIM)


def _slot_base(h):
    return ((h + 1) % HEADS_PER_TILE) * HEAD_DIM


def _moba_kernel(slopes_ref, q_ref, k_ref, v_ref, o_ref, km_ref, kaug_ref, vt_ref, qaug_ref, neg_ref,
                 s_ref, p_ref, acc_ref, *, n_blocks):
    blk = MOBA_BLOCK
    n_tiles = q_ref.shape[-1] // LANES

    def prepare():
        lane = lax.broadcasted_iota(jnp.int32, (blk, LANES), 1)
        j_pos = lax.broadcasted_iota(jnp.int32, (blk, LANES), 0).astype(jnp.float32)
        ones_rows = (lax.broadcasted_iota(jnp.int32, (BF16_SUBLANES, blk), 0) == 0
                     ).astype(jnp.float32)
        km_row = lax.broadcasted_iota(jnp.int32, (n_blocks, LANES), 0)
        km_lane = lax.broadcasted_iota(jnp.int32, (n_blocks, LANES), 1)
        for t in range(n_tiles):
            cols = slice(t * LANES, (t + 1) * LANES)
            kmean = jnp.zeros((n_blocks, LANES), jnp.float32)
            for n in range(n_blocks):
                rows = slice(n * blk, (n + 1) * blk)
                kf = k_ref[rows, cols].astype(jnp.float32)
                kmean = jnp.where(km_row == n, jnp.sum(kf, axis=0, keepdims=True) * (1.0 / blk), kmean)
                v_t = v_ref[rows, cols].astype(jnp.float32).T
                for hh in range(HEADS_PER_TILE):
                    h = t * HEADS_PER_TILE + hh
                    term = (lane - _slot_base(h)) // ALIBI_PIECES
                    aug = jnp.where(term == 0, j_pos,
                                    jnp.where(term == 1, float(n),
                                              jnp.where((term == 2) | (term == 3), 1.0, 0.0)))
                    kaug_ref[h, n] = jnp.where(_own_lanes(h, lane), kf, aug).astype(jnp.bfloat16)
                    vt_ref[h, n] = jnp.concatenate(
                        [v_t[hh * HEAD_DIM:(hh + 1) * HEAD_DIM], ones_rows], axis=0).astype(jnp.bfloat16)
            for hh in range(HEADS_PER_TILE):
                h = t * HEADS_PER_TILE + hh
                rest = jnp.where(_own_lanes(h, km_lane), kmean, 0.0)
                pieces = []
                for _ in range(GATE_PIECES):
                    piece = rest.astype(jnp.bfloat16).astype(jnp.float32)
                    pieces.append(piece)
                    rest = rest - piece
                km_ref[h] = jnp.concatenate(pieces[::-1], axis=0).astype(jnp.bfloat16)

    prepare()

    def select(g, carry):
        _moba_select(g, slopes_ref, q_ref, km_ref, qaug_ref, neg_ref, n_blocks=n_blocks)
        return carry

    lax.fori_loop(0, n_blocks // SELECT_TILES, select, 0)

    _moba_open(0, kaug_ref, qaug_ref, s_ref)

    def sweep(qi, carry):
        _moba_tile(qi, o_ref, kaug_ref, vt_ref, qaug_ref, neg_ref, s_ref, p_ref, acc_ref,
                   n_blocks=n_blocks)
        _moba_open(jnp.minimum(qi + 1, n_blocks - 1), kaug_ref, qaug_ref, s_ref)
        return carry

    lax.fori_loop(0, n_blocks, sweep, 0)


def _moba_select(g, slopes_ref, q_ref, km_ref, qaug_ref, neg_ref, *, n_blocks):
    blk = MOBA_BLOCK
    width = SELECT_TILES * blk
    n_tiles = q_ref.shape[-1] // LANES
    n_heads = n_tiles * HEADS_PER_TILE
    hg = pl.program_id(1)
    first_tile = jnp.asarray(g, jnp.int32) * SELECT_TILES
    q_rows = pl.ds(pl.multiple_of(first_tile * blk, width), width)

    def tile_of(shape):
        return first_tile + lax.broadcasted_iota(jnp.int32, shape, 1) // blk

    slot = lax.broadcasted_iota(jnp.int32, (ALIBI_ROWS, width), 0)
    i_pos = (lax.broadcasted_iota(jnp.int32, (ALIBI_ROWS, width), 1) % blk).astype(jnp.float32)
    tile_f = tile_of((ALIBI_ROWS, width)).astype(jnp.float32)
    pad = jnp.zeros((HEAD_DIM - ALIBI_ROWS, width), jnp.float32)
    blk_id = lax.broadcasted_iota(jnp.int32, (n_blocks, width), 0)
    past = blk_id < tile_of((n_blocks, width))
    for t in range(n_tiles):
        q_t = q_ref[q_rows, t * LANES:(t + 1) * LANES].astype(jnp.float32).T
        for hh in range(HEADS_PER_TILE):
            h = t * HEADS_PER_TILE + hh
            c = slopes_ref[hg * n_heads + h] * LOG2_E
            term = slot // ALIBI_PIECES
            full = jnp.where(term == 0, c,
                             jnp.where(term == 1, c * blk,
                                       jnp.where(term == 2, -(c * (blk * tile_f)),
                                                 jnp.where(term == 3, -(c * i_pos), 0.0))))
            piece = full.astype(jnp.bfloat16).astype(jnp.float32)
            aug = piece
            for r in range(1, ALIBI_PIECES):
                full = full - piece
                piece = full.astype(jnp.bfloat16).astype(jnp.float32)
                aug = jnp.where(slot % ALIBI_PIECES == r, piece, aug)
            own = q_t[hh * HEAD_DIM:(hh + 1) * HEAD_DIM]
            rows = [own, aug, pad] if _slot_base(h) > 0 else [aug, pad, own]
            q_aug = jnp.concatenate(rows, axis=0).astype(jnp.bfloat16)
            for j in range(SELECT_TILES):
                qaug_ref[h, first_tile + j] = q_aug[:, j * blk:(j + 1) * blk]

            g3 = jnp.dot(km_ref[h], q_aug, preferred_element_type=jnp.float32)
            gate = g3[:n_blocks] + g3[n_blocks:2 * n_blocks] + g3[2 * n_blocks:]
            gate = jnp.where(past, gate, NEG_INF)
            blk_f = blk_id.astype(jnp.float32)
            chosen = jnp.zeros((n_blocks, width), jnp.float32)
            for _ in range(MOBA_TOPK):
                best = jnp.max(gate, axis=0, keepdims=True)
                pick = jnp.min(jnp.where(gate == best, blk_f, float(n_blocks)), axis=0, keepdims=True)
                taken = blk_f == pick
                chosen = jnp.where(taken, 1.0, chosen)
                gate = jnp.where(taken, NEG_INF, gate)
            neg = jnp.where((chosen > 0.5) & past, 0.0, NEG_INF)
            for j in range(SELECT_TILES):
                neg_ref[h, first_tile + j, 0] = jnp.zeros((1, blk), jnp.float32)
                for u in range(1, n_blocks):
                    neg_ref[h, first_tile + j, u] = neg[u - 1:u, j * blk:(j + 1) * blk]


def _moba_open(qi, kaug_ref, qaug_ref, s_ref):
    blk = MOBA_BLOCK
    key_i = lax.broadcasted_iota(jnp.int32, (blk, blk), 0)
    qry_i = lax.broadcasted_iota(jnp.int32, (blk, blk), 1)
    causal = key_i <= qry_i
    for h in range(kaug_ref.shape[0]):
        s = jnp.dot(kaug_ref[h, qi], qaug_ref[h, qi], preferred_element_type=jnp.float32)
        s_ref[0, h] = jnp.where(causal, s, NEG_INF)


def _moba_tile(qi, o_ref, kaug_ref, vt_ref, qaug_ref, neg_ref, s_ref, p_ref, acc_ref, *, n_blocks):
    blk = MOBA_BLOCK
    n_heads = kaug_ref.shape[0]
    n_tiles = n_heads // HEADS_PER_TILE
    q_rows = pl.ds(pl.multiple_of(qi * blk, blk), blk)
    q_aug = [qaug_ref[h, qi] for h in range(n_heads)]
    for h in range(n_heads):
        p_ref[1, h] = jnp.zeros((blk, blk), jnp.bfloat16)
        acc_ref[h] = jnp.zeros((PV_ROWS, blk), jnp.float32)

    def block_of(u):
        return jnp.where(u == 0, qi, jnp.minimum(u - 1, n_blocks - 1))

    def step(u, cur, m_runs):
        b_prev = block_of(jnp.maximum(u - 1, 0))
        k_next = jnp.minimum(u, n_blocks - 1)
        pv = [jnp.dot(vt_ref[h, b_prev], p_ref[1 - cur, h], preferred_element_type=jnp.float32)
              for h in range(n_heads)]
        for h in range(n_heads):
            s_ref[1 - cur, h] = jnp.dot(kaug_ref[h, k_next], q_aug[h],
                                        preferred_element_type=jnp.float32)
        m_news = []
        for h in range(n_heads):
            s = s_ref[cur, h]
            neg = neg_ref[h, qi, u]
            m_new = jnp.maximum(m_runs[h], jnp.max(s, axis=0, keepdims=True) + neg)
            alpha = jnp.exp2(m_runs[h] - m_new)
            p = jnp.exp2(s - (m_new - neg))
            p_ref[cur, h] = p.astype(jnp.bfloat16)
            acc_ref[h] = alpha * (acc_ref[h] + pv[h])
            m_news.append(m_new)
        return tuple(m_news)

    def step_pair(i, m_runs):
        return step(2 * i + 1, 1, step(2 * i, 0, m_runs))

    m_init = jnp.full((1, blk), NEG_INF, jnp.float32)
    lax.fori_loop(0, (qi + 2) // 2, step_pair, (m_init,) * n_heads)
    last = 2 * ((qi + 2) // 2) - 1
    for t in range(n_tiles):
        outs = []
        for h in range(t * HEADS_PER_TILE, (t + 1) * HEADS_PER_TILE):
            acc = acc_ref[h] + jnp.dot(vt_ref[h, block_of(last)], p_ref[1, h],
                                       preferred_element_type=jnp.float32)
            outs.append(acc[:HEAD_DIM] / acc[HEAD_DIM:HEAD_DIM + 1])
        o_ref[q_rows, t * LANES:(t + 1) * LANES] = jnp.concatenate(outs, axis=0).T.astype(jnp.bfloat16)


MOBA_TILES_PER_STEP = 2


def _moba(q, k, v, slopes):
    bn, s, width = q.shape
    step_w = MOBA_TILES_PER_STEP * LANES
    assert s % MOBA_BLOCK == 0 and width % step_w == 0
    n_blocks = s // MOBA_BLOCK
    n_heads = MOBA_TILES_PER_STEP * HEADS_PER_TILE
    assert ALIBI_TERMS * ALIBI_PIECES <= ALIBI_ROWS <= HEAD_DIM and HEADS_PER_TILE == 2
    assert n_blocks <= 256 and MOBA_BLOCK <= 256
    seq = pl.BlockSpec((None, s, step_w), lambda b, hg: (b, 0, hg))
    return pl.pallas_call(
        functools.partial(_moba_kernel, n_blocks=n_blocks),
        grid=(bn, width // step_w),
        in_specs=[pl.BlockSpec(memory_space=pltpu.SMEM), seq, seq, seq],
        out_specs=seq,
        out_shape=jax.ShapeDtypeStruct((bn, s, width), jnp.bfloat16),
        scratch_shapes=[
            pltpu.VMEM((n_heads, GATE_PIECES * n_blocks, LANES), jnp.bfloat16),
            pltpu.VMEM((n_heads, n_blocks, MOBA_BLOCK, LANES), jnp.bfloat16),
            pltpu.VMEM((n_heads, n_blocks, PV_ROWS, MOBA_BLOCK), jnp.bfloat16),
            pltpu.VMEM((n_heads, n_blocks, LANES, MOBA_BLOCK), jnp.bfloat16),
            pltpu.VMEM((n_heads, n_blocks, n_blocks, 1, MOBA_BLOCK), jnp.float32),
            pltpu.VMEM((2, n_heads, MOBA_BLOCK, MOBA_BLOCK), jnp.float32),
            pltpu.VMEM((2, n_heads, MOBA_BLOCK, MOBA_BLOCK), jnp.bfloat16),
            pltpu.VMEM((n_heads, PV_ROWS, MOBA_BLOCK), jnp.float32),
        ],
        compiler_params=pltpu.CompilerParams(
            dimension_semantics=("arbitrary", "arbitrary"),
            vmem_limit_bytes=VMEM_LIMIT),
        name="moba",
    )(slopes, q, k, v)


def _mix_out_kernel(x_ref, ya_ref, yb_ref, yc_ref, wout_ref, gpost_ref, gprex_ref, wq_ref,
                    kx_ref, vx_ref, wo_ref, gpostx_ref, *rest):
    n_cast = len(rest) // 2
    o_ref = rest[n_cast]
    _cast_riders(rest[:n_cast], rest[n_cast + 1:])

    d = x_ref.shape[-1]
    dh = d // X_HEADS
    nt = (((1,), (1,)), ((), ()))
    rows = x_ref.shape[0] // ROW_GROUPS
    groups = [slice(r * rows, (r + 1) * rows) for r in range(ROW_GROUPS)]

    ys = [jnp.concatenate([ya_ref[g, :], yb_ref[g, :], yc_ref[g, :]], axis=-1) for g in groups]
    mixes = [jnp.dot(y, wout_ref[...], preferred_element_type=jnp.float32) for y in ys]
    x1s = [x_ref[g, :] + _rms(mix, gpost_ref[...]) for g, mix in zip(groups, mixes)]
    hs = [_rms(x1, gprex_ref[...]).astype(jnp.bfloat16) for x1 in x1s]
    qxs = [(jnp.dot(h, wq_ref[...], preferred_element_type=jnp.float32) * (dh ** -0.5)
            ).astype(jnp.bfloat16) for h in hs]
    outs = []
    for qx in qxs:
        heads = []
        for hh in range(X_HEADS):
            cols = slice(hh * dh, (hh + 1) * dh)
            s = lax.dot_general(qx[:, cols], kx_ref[:, cols], nt, preferred_element_type=jnp.float32)
            p = jnp.exp(s - jnp.max(s, axis=-1, keepdims=True))
            denom = jnp.sum(p, axis=-1, keepdims=True)
            oh = jnp.dot(p.astype(jnp.bfloat16), vx_ref[:, cols], preferred_element_type=jnp.float32)
            heads.append((oh / denom).astype(jnp.bfloat16))
        outs.append(jnp.concatenate(heads, axis=-1))
    atts = [jnp.dot(o, wo_ref[...], preferred_element_type=jnp.float32) for o in outs]
    for g, x1, att in zip(groups, x1s, atts):
        o_ref[g, :] = x1 + _rms(att, gpostx_ref[...])


def _mix_out(x, ya, yb, yc, l, w_out, g_post, g_prex, w_xq, kx, vx, w_xo, g_postx, next_weights):
    bn, s, d = x.shape
    tm = min(ROW_GROUPS * TOKEN_TILE, s)
    m = kx.shape[2]
    n_i = s // tm
    lsel3 = lambda b, i: (l, 0, 0)
    first3 = lambda b, i: (0, 0, 0)
    tok = lambda w: pl.BlockSpec((None, tm, w), lambda b, i: (b, i, 0))
    gain = pl.BlockSpec((None, 1, d), lsel3)
    wsq = _const_spec((None, d, d), first3)
    mem = pl.BlockSpec((None, None, m, d), lambda b, i: (l, b, 0, 0))
    cast_in, cast_out, cast_shape = _cast_rider_specs(
        next_weights, l + 1, bn * n_i, lambda b, i: b * n_i + i)
    out = pl.pallas_call(
        _mix_out_kernel,
        grid=(bn, n_i),
        in_specs=[tok(d), tok(ya.shape[-1]), tok(yb.shape[-1]), tok(yc.shape[-1]),
                  wsq, gain, gain, wsq, mem, mem, wsq, gain] + cast_in,
        out_specs=[tok(d)] + cast_out,
        out_shape=[jax.ShapeDtypeStruct((bn, s, d), jnp.float32)] + cast_shape,
        compiler_params=pltpu.CompilerParams(
            dimension_semantics=("arbitrary", "arbitrary"), vmem_limit_bytes=VMEM_LIMIT),
        name="mix_out",
    )(x, ya, yb, yc, w_out, g_post, g_prex, w_xq, kx, vx, w_xo, g_postx, *next_weights)
    return out[0], tuple(out[1:])


def _ffn_kernel(x_ref, gpre_ref, w1_ref, w2_ref, gpost_ref, o_ref):
    rows = x_ref.shape[0] // ROW_GROUPS
    for r in range(ROW_GROUPS):
        g = slice(r * rows, (r + 1) * rows)
        x = x_ref[g, :]
        h = _rms(x, gpre_ref[...]).astype(jnp.bfloat16)
        a = jnp.maximum(jnp.dot(h, w1_ref[...], preferred_element_type=jnp.float32), 0.0)
        f = jnp.dot((a * a).astype(jnp.bfloat16), w2_ref[...], preferred_element_type=jnp.float32)
        o_ref[g, :] = x + _rms(f, gpost_ref[...])


def _ffn(x, l, g_pre, w1, w2, g_post):
    bn, s, d = x.shape
    tm = min(ROW_GROUPS * TOKEN_TILE, s)
    dff = w1.shape[-1]
    lsel3 = lambda b, i: (l, 0, 0)
    first3 = lambda b, i: (0, 0, 0)
    tok = pl.BlockSpec((None, tm, d), lambda b, i: (b, i, 0))
    gain = pl.BlockSpec((None, 1, d), lsel3)
    return pl.pallas_call(
        _ffn_kernel,
        grid=(bn, s // tm),
        in_specs=[tok, gain, _const_spec((None, d, dff), first3),
                  _const_spec((None, dff, d), first3), gain],
        out_specs=tok,
        out_shape=jax.ShapeDtypeStruct((bn, s, d), jnp.float32),
        compiler_params=pltpu.CompilerParams(
            dimension_semantics=("arbitrary", "arbitrary"), vmem_limit_bytes=VMEM_LIMIT),
        name="ffn",
    )(x, g_pre, w1, w2, g_post)


def kernel(x, mem, pre_mix_g, w_in, gate_ln_g, gate_ln_b, w_s, b_s, w_dw, b_dw, conv_gn_g, conv_gn_b, w_out, post_mix_g, pre_x_g, mem_g, w_xq, w_xkv, w_xo, post_x_g, pre_ffn_g, w_ff1, w_ff2, post_ffn_g):
    depth, d, _ = w_in.shape
    a_width = gate_ln_g.shape[-1]
    c_width = w_dw.shape[-1]
    b_width = d - a_width - c_width
    a_heads = a_width // HEAD_DIM
    b_heads = b_width // HEAD_DIM
    assert w_in.shape[-1] == 2 * a_width + 3 * b_width + 2 * c_width
    assert w_s.shape[1:] == (a_heads, CHUNK, CHUNK) and w_dw.shape[1] == CONV_WIDTH
    assert x.shape[1] % MOBA_BLOCK == 0 and x.shape[1] % TOKEN_TILE == 0

    row = lambda g: g[:, None, :]
    ws_cat = w_s.transpose(0, 2, 1, 3).reshape(depth, CHUNK, a_heads * CHUNK)
    bs_x = jnp.repeat(b_s.transpose(0, 2, 1), HEAD_DIM, axis=-1)
    slopes = 2.0 ** (-8.0 * jnp.arange(1, b_heads + 1, dtype=jnp.float32) / b_heads)
    layer_weights = (w_in, w_out, w_xq, w_xo, w_ff1, w_ff2)
    kx, vx, (w_in_l, w_out_l, w_xq_l, w_xo_l, w_ff1_l, w_ff2_l) = _mem_kv(
        mem, row(mem_g), w_xkv, layer_weights)
    for l in range(depth):
        ya, q, k, v, yc = _mix_in(
            x, l, row(pre_mix_g), w_in_l, row(gate_ln_g), row(gate_ln_b), ws_cat, bs_x, w_dw,
            row(b_dw), row(conv_gn_g), row(conv_gn_b),
            a_width=a_width, b_width=b_width, c_width=c_width)
        yb = _moba(q, k, v, slopes)
        x, next_l = _mix_out(x, ya, yb, yc, l, w_out_l, row(post_mix_g), row(pre_x_g), w_xq_l, kx, vx,
                             w_xo_l, row(post_x_g), layer_weights if l + 1 < depth else ())
        x = _ffn(x, l, row(pre_ffn_g), w_ff1_l, w_ff2_l, row(post_ffn_g))
        if next_l:
            w_in_l, w_out_l, w_xq_l, w_xo_l, w_ff1_l, w_ff2_l = next_l
    return x
```

```python
import functools
import math

import jax
import jax.numpy as jnp
from jax import lax
from jax.experimental import pallas as pl
from jax.experimental.pallas import tpu as pltpu

HEAD_DIM = 64
CHUNK = 128
MOBA_BLOCK = 256
MOBA_TOPK = 3
CONV_WIDTH = 31
X_HEADS = 4
EPS = 1e-6
LOG2_E = math.log2(math.e)

LANES = 128
SUBLANES = 8
HALO = 32
CONV_ROWS = 128
TOKEN_TILE = 512
MIX_IN_TILE = 1024
ROW_GROUPS = 2
VMEM_LIMIT = 56 * 1024 * 1024

NEG_INF = float("-inf")


def _rms(x, g):
    return x * lax.rsqrt(jnp.mean(x * x, axis=-1, keepdims=True) + EPS) * g


def _const_spec(shape, index_map):
    return pl.BlockSpec(shape, index_map, pipeline_mode=pl.Buffered(1))


def _cast_rider_specs(weights, layer, n_steps, step_of):
    in_specs, out_specs, out_shapes = [], [], []
    for w in weights:
        _, rows, cols = w.shape
        assert rows % (n_steps * BF16_SUBLANES) == 0
        chunk = (None, rows // n_steps, cols)
        in_specs.append(pl.BlockSpec(chunk, lambda *g: (layer, step_of(*g), 0)))
        out_specs.append(pl.BlockSpec(chunk, lambda *g: (0, step_of(*g), 0)))
        out_shapes.append(jax.ShapeDtypeStruct((1, rows, cols), jnp.bfloat16))
    return in_specs, out_specs, out_shapes


def _cast_riders(src_refs, dst_refs):
    for src_ref, dst_ref in zip(src_refs, dst_refs):
        dst_ref[...] = src_ref[...].astype(jnp.bfloat16)


def _mem_kv_kernel(mem_ref, g_ref, w_ref, *rest):
    n_cast = (len(rest) - 2) // 2
    k_ref, v_ref = rest[n_cast:n_cast + 2]
    _cast_riders(rest[:n_cast], rest[n_cast + 2:])
    d = mem_ref.shape[-1]
    m = _rms(mem_ref[...], g_ref[...]).astype(jnp.bfloat16)
    kv = jnp.dot(m, w_ref[...].astype(jnp.bfloat16), preferred_element_type=jnp.float32)
    k_ref[...] = kv[:, :d].astype(jnp.bfloat16)
    v_ref[...] = kv[:, d:].astype(jnp.bfloat16)


def _mem_kv(mem, mem_g, w_xkv, layer_weights):
    bn, m, d = mem.shape
    depth = w_xkv.shape[0]
    out = jax.ShapeDtypeStruct((depth, bn, m, d), jnp.bfloat16)
    cast_in, cast_out, cast_shape = _cast_rider_specs(
        layer_weights, 0, depth * bn, lambda l, b: l * bn + b)
    res = pl.pallas_call(
        _mem_kv_kernel,
        grid=(depth, bn),
        in_specs=[
            pl.BlockSpec((None, m, d), lambda l, b: (b, 0, 0)),
            pl.BlockSpec((None, 1, d), lambda l, b: (l, 0, 0)),
            pl.BlockSpec((None, d, 2 * d), lambda l, b: (l, 0, 0)),
        ] + cast_in,
        out_specs=[
            pl.BlockSpec((None, None, m, d), lambda l, b: (l, b, 0, 0)),
            pl.BlockSpec((None, None, m, d), lambda l, b: (l, b, 0, 0)),
        ] + cast_out,
        out_shape=[out, out] + cast_shape,
        compiler_params=pltpu.CompilerParams(
            dimension_semantics=("arbitrary", "arbitrary"), vmem_limit_bytes=VMEM_LIMIT),
        name="mem_kv",
    )(mem, mem_g, w_xkv, *layer_weights)
    return res[0], res[1], tuple(res[2:])


def _mix_in_kernel(x_ref, g_ref, win_ref, lng_ref, lnb_ref, ws_ref, bs_ref, wdw_ref, bdw_ref,
                   gng_ref, gnb_ref, ya_ref, q_ref, k_ref, v_ref, yc_ref, hist_ref, conv_ref, shift_ref,
                   *, a_width, b_width, c_width):
    tm = x_ref.shape[0]

    @pl.when(pl.program_id(1) == 0)
    def _():
        hist_ref[0:HALO, :] = jnp.zeros((HALO, c_width), jnp.float32)

    h = _rms(x_ref[...], g_ref[...]).astype(jnp.bfloat16)

    def project(lo, hi):
        return jnp.dot(h, win_ref[:, lo:hi], preferred_element_type=jnp.float32)

    o = 2 * a_width + 3 * b_width
    zc = project(o, o + 2 * c_width)
    hist_ref[HALO:HALO + tm, :] = zc[:, :c_width] * (1.0 / (1.0 + jnp.exp(-zc[:, c_width:])))

    o = 2 * a_width
    q = (project(o, o + b_width) * (HEAD_DIM ** -0.5 * LOG2_E)).astype(jnp.bfloat16)
    k = project(o + b_width, o + 2 * b_width).astype(jnp.bfloat16)
    v = project(o + 2 * b_width, o + 3 * b_width).astype(jnp.bfloat16)
    za = project(0, 2 * a_width)

    first = HALO - (CONV_WIDTH - 1)
    for c in range(tm // CONV_ROWS):
        r0 = c * CONV_ROWS
        acc = jnp.zeros((CONV_ROWS, c_width), jnp.float32)
        for phase in range(SUBLANES):
            taps = [j for j in range(CONV_WIDTH) if (first + j) % SUBLANES == phase]
            span = first + taps[-1] - phase
            rows = CONV_ROWS + span
            if phase:
                shift_ref[phase, 0:rows, :] = hist_ref[r0 + phase:r0 + phase + rows, :]
            for j in taps:
                off = first + j - phase
                window = (shift_ref[phase, off:off + CONV_ROWS, :] if phase else
                          hist_ref[r0 + off:r0 + off + CONV_ROWS, :])
                acc = acc + wdw_ref[j:j + 1, :] * window
        conv_ref[r0:r0 + CONV_ROWS, :] = acc + bdw_ref[...]
    hist_ref[0:HALO, :] = hist_ref[tm:tm + HALO, :]
    q_ref[...] = q
    k_ref[...] = k
    v_ref[...] = v

    ga = 0.5 * za * (1.0 + lax.erf(za * (2.0 ** -0.5)))
    u = ga[:, :a_width]
    vv = ga[:, a_width:]
    mu = jnp.mean(vv, axis=-1, keepdims=True)
    vc = vv - mu
    var = jnp.mean(vc * vc, axis=-1, keepdims=True)
    vn = (vc * lax.rsqrt(var + EPS) * lng_ref[...] + lnb_ref[...]).astype(jnp.bfloat16)
    n_heads = a_width // HEAD_DIM
    t_idx = lax.broadcasted_iota(jnp.int32, (CHUNK, n_heads * CHUNK), 0)
    s_idx = lax.broadcasted_iota(jnp.int32, (CHUNK, n_heads * CHUNK), 1) % CHUNK
    w_tril = jnp.where(s_idx <= t_idx, ws_ref[...], 0.0).astype(jnp.bfloat16)
    r_head = lax.broadcasted_iota(jnp.int32, (n_heads * CHUNK, a_width), 0) // CHUNK
    c_head = lax.broadcasted_iota(jnp.int32, (n_heads * CHUNK, a_width), 1) // HEAD_DIM
    head_match = r_head == c_head
    for c in range(tm // CHUNK):
        rows = slice(c * CHUNK, (c + 1) * CHUNK)
        v_chunk = vn[rows]
        v_bd = jnp.where(head_match, jnp.concatenate([v_chunk] * n_heads, axis=0), 0.0)
        mixed = jnp.dot(w_tril, v_bd.astype(jnp.bfloat16), preferred_element_type=jnp.float32)
        ya_ref[rows, :] = (u[rows] * (mixed + bs_ref[...])).astype(jnp.bfloat16)

    gr = lax.broadcasted_iota(jnp.int32, (2 * c_width, c_width), 0) % c_width // HEAD_DIM
    gc = lax.broadcasted_iota(jnp.int32, (2 * c_width, c_width), 1) // HEAD_DIM
    gmat2 = jnp.where(gr == gc, 1.0 / HEAD_DIM, 0.0).astype(jnp.bfloat16)

    def group_mean(a):
        hi = a.astype(jnp.bfloat16)
        lo = (a - hi.astype(jnp.float32)).astype(jnp.bfloat16)
        return jnp.dot(jnp.concatenate([hi, lo], axis=-1), gmat2, preferred_element_type=jnp.float32)

    y = conv_ref[...]
    yd = y - group_mean(y)
    yn = yd * lax.rsqrt(group_mean(yd * yd) + EPS) * gng_ref[...] + gnb_ref[...]
    yc_ref[...] = (yn * (1.0 / (1.0 + jnp.exp(-yn)))).astype(jnp.bfloat16)


def _mix_in(x, l, pre_g, w_in, ln_g, ln_b, ws_cat, bs_x, w_dw, b_dw, gn_g, gn_b,
            *, a_width, b_width, c_width):
    bn, s, d = x.shape
    tm = min(MIX_IN_TILE, s)
    in_cols = w_in.shape[-1]
    n_heads = a_width // HEAD_DIM
    lsel3 = lambda b, i: (l, 0, 0)
    tok = lambda w: pl.BlockSpec((None, tm, w), lambda b, i: (b, i, 0))
    bf = lambda w: jax.ShapeDtypeStruct((bn, s, w), jnp.bfloat16)
    return pl.pallas_call(
        functools.partial(_mix_in_kernel, a_width=a_width, b_width=b_width, c_width=c_width),
        grid=(bn, s // tm),
        in_specs=[
            tok(d),
            pl.BlockSpec((None, 1, d), lsel3),
            _const_spec((None, d, in_cols), lambda b, i: (0, 0, 0)),
            pl.BlockSpec((None, 1, a_width), lsel3),
            pl.BlockSpec((None, 1, a_width), lsel3),
            pl.BlockSpec((None, CHUNK, n_heads * CHUNK), lsel3),
            pl.BlockSpec((None, CHUNK, a_width), lsel3),
            pl.BlockSpec((None, CONV_WIDTH, c_width), lsel3),
            pl.BlockSpec((None, 1, c_width), lsel3),
            pl.BlockSpec((None, 1, c_width), lsel3),
            pl.BlockSpec((None, 1, c_width), lsel3),
        ],
        out_specs=[tok(a_width), tok(b_width), tok(b_width), tok(b_width), tok(c_width)],
        out_shape=[bf(a_width), bf(b_width), bf(b_width), bf(b_width), bf(c_width)],
        scratch_shapes=[pltpu.VMEM((tm + HALO, c_width), jnp.float32),
                        pltpu.VMEM((tm, c_width), jnp.float32),
                        pltpu.VMEM((SUBLANES, CONV_ROWS + HALO, c_width), jnp.float32)],
        compiler_params=pltpu.CompilerParams(
            dimension_semantics=("arbitrary", "arbitrary"), vmem_limit_bytes=VMEM_LIMIT),
        name="mix_in",
    )(x, pre_g, w_in, ln_g, ln_b, ws_cat, bs_x, w_dw, b_dw, gn_g, gn_b)


ALIBI_TERMS = 4
ALIBI_PIECES = 3
ALIBI_ROWS = 16
HEADS_PER_TILE = LANES // HEAD_DIM
BF16_SUBLANES = 16
PV_ROWS = HEAD_DIM + BF16_SUBLANES
GATE_PIECES = 3
SELECT_TILES = 4


def _own_lanes(h, idx):
    hh = h % HEADS_PER_TILE
    return (idx >= hh * HEAD_DIM) & (idx < (hh + 1) * HEAD_DIM)


def _slot_base(h):
    return ((h + 1) % HEADS_PER_TILE) * HEAD_DIM


def _moba_kernel(slopes_ref, q_ref, k_ref, v_ref, o_ref, km_ref, kaug_ref, vt_ref, qaug_ref, neg_ref,
                 s_ref, p_ref, acc_ref, *, n_blocks):
    blk = MOBA_BLOCK
    n_tiles = q_ref.shape[-1] // LANES

    def prepare():
        lane = lax.broadcasted_iota(jnp.int32, (blk, LANES), 1)
        j_pos = lax.broadcasted_iota(jnp.int32, (blk, LANES), 0).astype(jnp.float32)
        ones_rows = (lax.broadcasted_iota(jnp.int32, (BF16_SUBLANES, blk), 0) == 0
                     ).astype(jnp.float32)
        km_row = lax.broadcasted_iota(jnp.int32, (n_blocks, LANES), 0)
        km_lane = lax.broadcasted_iota(jnp.int32, (n_blocks, LANES), 1)
        sum_rows = jnp.ones((BF16_SUBLANES, blk), jnp.bfloat16)
        for t in range(n_tiles):
            cols = slice(t * LANES, (t + 1) * LANES)
            kmean = jnp.zeros((n_blocks, LANES), jnp.float32)
            for n in range(n_blocks):
                rows = slice(n * blk, (n + 1) * blk)
                kb = k_ref[rows, cols]
                k_sum = jnp.dot(sum_rows, kb, preferred_element_type=jnp.float32)[:1]
                kmean = jnp.where(km_row == n, k_sum * (1.0 / blk), kmean)
                v_t = v_ref[rows, cols].astype(jnp.float32).T
                for hh in range(HEADS_PER_TILE):
                    h = t * HEADS_PER_TILE + hh
                    term = (lane - _slot_base(h)) // ALIBI_PIECES
                    aug = jnp.where(term == 0, j_pos,
                                    jnp.where(term == 1, float(n),
                                              jnp.where((term == 2) | (term == 3), 1.0, 0.0)))
                    kaug_ref[h, n] = jnp.where(_own_lanes(h, lane), kb, aug.astype(jnp.bfloat16))
                    vt_ref[h, n] = jnp.concatenate(
                        [v_t[hh * HEAD_DIM:(hh + 1) * HEAD_DIM], ones_rows], axis=0).astype(jnp.bfloat16)
            for hh in range(HEADS_PER_TILE):
                h = t * HEADS_PER_TILE + hh
                rest = jnp.where(_own_lanes(h, km_lane), kmean, 0.0)
                pieces = []
                for _ in range(GATE_PIECES):
                    piece = rest.astype(jnp.bfloat16).astype(jnp.float32)
                    pieces.append(piece)
                    rest = rest - piece
                km_ref[h] = jnp.concatenate(pieces[::-1], axis=0).astype(jnp.bfloat16)

    prepare()

    def select(g, carry):
        _moba_select(g, slopes_ref, q_ref, km_ref, qaug_ref, neg_ref, n_blocks=n_blocks)
        return carry

    lax.fori_loop(0, n_blocks // SELECT_TILES, select, 0)

    _moba_open(0, kaug_ref, qaug_ref, s_ref)

    def sweep(qi, carry):
        _moba_tile(qi, o_ref, kaug_ref, vt_ref, qaug_ref, neg_ref, s_ref, p_ref, acc_ref,
                   n_blocks=n_blocks)
        _moba_open(jnp.minimum(qi + 1, n_blocks - 1), kaug_ref, qaug_ref, s_ref)
        return carry

    lax.fori_loop(0, n_blocks, sweep, 0)


def _moba_select(g, slopes_ref, q_ref, km_ref, qaug_ref, neg_ref, *, n_blocks):
    blk = MOBA_BLOCK
    width = SELECT_TILES * blk
    n_tiles = q_ref.shape[-1] // LANES
    n_heads = n_tiles * HEADS_PER_TILE
    hg = pl.program_id(1)
    first_tile = jnp.asarray(g, jnp.int32) * SELECT_TILES
    q_rows = pl.ds(pl.multiple_of(first_tile * blk, width), width)

    def tile_of(shape):
        return first_tile + lax.broadcasted_iota(jnp.int32, shape, 1) // blk

    slot = lax.broadcasted_iota(jnp.int32, (ALIBI_ROWS, width), 0)
    i_pos = (lax.broadcasted_iota(jnp.int32, (ALIBI_ROWS, width), 1) % blk).astype(jnp.float32)
    tile_f = tile_of((ALIBI_ROWS, width)).astype(jnp.float32)
    pad = jnp.zeros((HEAD_DIM - ALIBI_ROWS, width), jnp.float32)
    blk_id = lax.broadcasted_iota(jnp.int32, (n_blocks, width), 0)
    past = blk_id < tile_of((n_blocks, width))
    for t in range(n_tiles):
        q_t = q_ref[q_rows, t * LANES:(t + 1) * LANES].astype(jnp.float32).T
        for hh in range(HEADS_PER_TILE):
            h = t * HEADS_PER_TILE + hh
            c = slopes_ref[hg * n_heads + h] * LOG2_E
            term = slot // ALIBI_PIECES
            full = jnp.where(term == 0, c,
                             jnp.where(term == 1, c * blk,
                                       jnp.where(term == 2, -(c * (blk * tile_f)),
                                                 jnp.where(term == 3, -(c * i_pos), 0.0))))
            piece = full.astype(jnp.bfloat16).astype(jnp.float32)
            aug = piece
            for r in range(1, ALIBI_PIECES):
                full = full - piece
                piece = full.astype(jnp.bfloat16).astype(jnp.float32)
                aug = jnp.where(slot % ALIBI_PIECES == r, piece, aug)
            own = q_t[hh * HEAD_DIM:(hh + 1) * HEAD_DIM]
            rows = [own, aug, pad] if _slot_base(h) > 0 else [aug, pad, own]
            q_aug = jnp.concatenate(rows, axis=0).astype(jnp.bfloat16)
            for j in range(SELECT_TILES):
                qaug_ref[h, first_tile + j] = q_aug[:, j * blk:(j + 1) * blk]

            g3 = jnp.dot(km_ref[h], q_aug, preferred_element_type=jnp.float32)
            gate = g3[:n_blocks] + g3[n_blocks:2 * n_blocks] + g3[2 * n_blocks:]
            gate = jnp.where(past, gate, NEG_INF)
            blk_f = blk_id.astype(jnp.float32)
            chosen = jnp.zeros((n_blocks, width), jnp.float32)
            for _ in range(MOBA_TOPK):
                best = jnp.max(gate, axis=0, keepdims=True)
                pick = jnp.min(jnp.where(gate == best, blk_f, float(n_blocks)), axis=0, keepdims=True)
                taken = blk_f == pick
                chosen = jnp.where(taken, 1.0, chosen)
                gate = jnp.where(taken, NEG_INF, gate)
            neg = jnp.where((chosen > 0.5) & past, 0.0, NEG_INF)
            for j in range(SELECT_TILES):
                neg_ref[h, first_tile + j, 0] = jnp.zeros((1, blk), jnp.float32)
                for u in range(1, n_blocks):
                    neg_ref[h, first_tile + j, u] = neg[u - 1:u, j * blk:(j + 1) * blk]


def _moba_open(qi, kaug_ref, qaug_ref, s_ref):
    blk = MOBA_BLOCK
    key_i = lax.broadcasted_iota(jnp.int32, (blk, blk), 0)
    qry_i = lax.broadcasted_iota(jnp.int32, (blk, blk), 1)
    causal = key_i <= qry_i
    for h in range(kaug_ref.shape[0]):
        s = jnp.dot(kaug_ref[h, qi], qaug_ref[h, qi], preferred_element_type=jnp.float32)
        s_ref[0, h] = jnp.where(causal, s, NEG_INF)


def _moba_tile(qi, o_ref, kaug_ref, vt_ref, qaug_ref, neg_ref, s_ref, p_ref, acc_ref, *, n_blocks):
    blk = MOBA_BLOCK
    n_heads = kaug_ref.shape[0]
    n_tiles = n_heads // HEADS_PER_TILE
    q_rows = pl.ds(pl.multiple_of(qi * blk, blk), blk)
    q_aug = [qaug_ref[h, qi] for h in range(n_heads)]
    for h in range(n_heads):
        p_ref[1, h] = jnp.zeros((blk, blk), jnp.bfloat16)
        acc_ref[h] = jnp.zeros((PV_ROWS, blk), jnp.float32)

    def block_of(u):
        return jnp.where(u == 0, qi, jnp.minimum(u - 1, n_blocks - 1))

    def step(u, cur, m_runs):
        b_prev = block_of(jnp.maximum(u - 1, 0))
        k_next = jnp.minimum(u, n_blocks - 1)
        pv = [jnp.dot(vt_ref[h, b_prev], p_ref[1 - cur, h], preferred_element_type=jnp.float32)
              for h in range(n_heads)]
        for h in range(n_heads):
            s_ref[1 - cur, h] = jnp.dot(kaug_ref[h, k_next], q_aug[h],
                                        preferred_element_type=jnp.float32)
        m_news = []
        for h in range(n_heads):
            s = s_ref[cur, h]
            neg = neg_ref[h, qi, u]
            m_new = jnp.maximum(m_runs[h], jnp.max(s, axis=0, keepdims=True) + neg)
            alpha = jnp.exp2(m_runs[h] - m_new)
            p = jnp.exp2(s - (m_new - neg))
            p_ref[cur, h] = p.astype(jnp.bfloat16)
            acc_ref[h] = alpha * (acc_ref[h] + pv[h])
            m_news.append(m_new)
        return tuple(m_news)

    def step_pair(i, m_runs):
        return step(2 * i + 1, 1, step(2 * i, 0, m_runs))

    m_init = jnp.full((1, blk), NEG_INF, jnp.float32)
    lax.fori_loop(0, (qi + 2) // 2, step_pair, (m_init,) * n_heads)
    last = 2 * ((qi + 2) // 2) - 1
    for t in range(n_tiles):
        outs = []
        for h in range(t * HEADS_PER_TILE, (t + 1) * HEADS_PER_TILE):
            acc = acc_ref[h] + jnp.dot(vt_ref[h, block_of(last)], p_ref[1, h],
                                       preferred_element_type=jnp.float32)
            outs.append(acc[:HEAD_DIM] / acc[HEAD_DIM:HEAD_DIM + 1])
        o_ref[q_rows, t * LANES:(t + 1) * LANES] = jnp.concatenate(outs, axis=0).T.astype(jnp.bfloat16)


MOBA_TILES_PER_STEP = 2


def _moba(q, k, v, slopes):
    bn, s, width = q.shape
    step_w = MOBA_TILES_PER_STEP * LANES
    assert s % MOBA_BLOCK == 0 and width % step_w == 0
    n_blocks = s // MOBA_BLOCK
    n_heads = MOBA_TILES_PER_STEP * HEADS_PER_TILE
    assert ALIBI_TERMS * ALIBI_PIECES <= ALIBI_ROWS <= HEAD_DIM and HEADS_PER_TILE == 2
    assert n_blocks <= 256 and MOBA_BLOCK <= 256
    seq = pl.BlockSpec((None, s, step_w), lambda b, hg: (b, 0, hg))
    return pl.pallas_call(
        functools.partial(_moba_kernel, n_blocks=n_blocks),
        grid=(bn, width // step_w),
        in_specs=[pl.BlockSpec(memory_space=pltpu.SMEM), seq, seq, seq],
        out_specs=seq,
        out_shape=jax.ShapeDtypeStruct((bn, s, width), jnp.bfloat16),
        scratch_shapes=[
            pltpu.VMEM((n_heads, GATE_PIECES * n_blocks, LANES), jnp.bfloat16),
            pltpu.VMEM((n_heads, n_blocks, MOBA_BLOCK, LANES), jnp.bfloat16),
            pltpu.VMEM((n_heads, n_blocks, PV_ROWS, MOBA_BLOCK), jnp.bfloat16),
            pltpu.VMEM((n_heads, n_blocks, LANES, MOBA_BLOCK), jnp.bfloat16),
            pltpu.VMEM((n_heads, n_blocks, n_blocks, 1, MOBA_BLOCK), jnp.float32),
            pltpu.VMEM((2, n_heads, MOBA_BLOCK, MOBA_BLOCK), jnp.float32),
            pltpu.VMEM((2, n_heads, MOBA_BLOCK, MOBA_BLOCK), jnp.bfloat16),
            pltpu.VMEM((n_heads, PV_ROWS, MOBA_BLOCK), jnp.float32),
        ],
        compiler_params=pltpu.CompilerParams(
            dimension_semantics=("arbitrary", "arbitrary"),
            vmem_limit_bytes=VMEM_LIMIT),
        name="moba",
    )(slopes, q, k, v)


def _mix_out_kernel(x_ref, ya_ref, yb_ref, yc_ref, wout_ref, gpost_ref, gprex_ref, wq_ref,
                    kx_ref, vx_ref, wo_ref, gpostx_ref, *rest):
    n_cast = len(rest) // 2
    o_ref = rest[n_cast]
    _cast_riders(rest[:n_cast], rest[n_cast + 1:])

    d = x_ref.shape[-1]
    dh = d // X_HEADS
    nt = (((1,), (1,)), ((), ()))
    rows = x_ref.shape[0] // ROW_GROUPS
    groups = [slice(r * rows, (r + 1) * rows) for r in range(ROW_GROUPS)]

    ys = [jnp.concatenate([ya_ref[g, :], yb_ref[g, :], yc_ref[g, :]], axis=-1) for g in groups]
    mixes = [jnp.dot(y, wout_ref[...], preferred_element_type=jnp.float32) for y in ys]
    x1s = [x_ref[g, :] + _rms(mix, gpost_ref[...]) for g, mix in zip(groups, mixes)]
    hs = [_rms(x1, gprex_ref[...]).astype(jnp.bfloat16) for x1 in x1s]
    qxs = [(jnp.dot(h, wq_ref[...], preferred_element_type=jnp.float32) * (dh ** -0.5)
            ).astype(jnp.bfloat16) for h in hs]
    outs = []
    for qx in qxs:
        heads = []
        for hh in range(X_HEADS):
            cols = slice(hh * dh, (hh + 1) * dh)
            s = lax.dot_general(qx[:, cols], kx_ref[:, cols], nt, preferred_element_type=jnp.float32)
            p = jnp.exp(s - jnp.max(s, axis=-1, keepdims=True))
            denom = jnp.sum(p, axis=-1, keepdims=True)
            oh = jnp.dot(p.astype(jnp.bfloat16), vx_ref[:, cols], preferred_element_type=jnp.float32)
            heads.append((oh / denom).astype(jnp.bfloat16))
        outs.append(jnp.concatenate(heads, axis=-1))
    atts = [jnp.dot(o, wo_ref[...], preferred_element_type=jnp.float32) for o in outs]
    for g, x1, att in zip(groups, x1s, atts):
        o_ref[g, :] = x1 + _rms(att, gpostx_ref[...])


def _mix_out(x, ya, yb, yc, l, w_out, g_post, g_prex, w_xq, kx, vx, w_xo, g_postx, next_weights):
    bn, s, d = x.shape
    tm = min(ROW_GROUPS * TOKEN_TILE, s)
    m = kx.shape[2]
    n_i = s // tm
    lsel3 = lambda b, i: (l, 0, 0)
    first3 = lambda b, i: (0, 0, 0)
    tok = lambda w: pl.BlockSpec((None, tm, w), lambda b, i: (b, i, 0))
    gain = pl.BlockSpec((None, 1, d), lsel3)
    wsq = _const_spec((None, d, d), first3)
    mem = pl.BlockSpec((None, None, m, d), lambda b, i: (l, b, 0, 0))
    cast_in, cast_out, cast_shape = _cast_rider_specs(
        next_weights, l + 1, bn * n_i, lambda b, i: b * n_i + i)
    out = pl.pallas_call(
        _mix_out_kernel,
        grid=(bn, n_i),
        in_specs=[tok(d), tok(ya.shape[-1]), tok(yb.shape[-1]), tok(yc.shape[-1]),
                  wsq, gain, gain, wsq, mem, mem, wsq, gain] + cast_in,
        out_specs=[tok(d)] + cast_out,
        out_shape=[jax.ShapeDtypeStruct((bn, s, d), jnp.float32)] + cast_shape,
        compiler_params=pltpu.CompilerParams(
            dimension_semantics=("arbitrary", "arbitrary"), vmem_limit_bytes=VMEM_LIMIT),
        name="mix_out",
    )(x, ya, yb, yc, w_out, g_post, g_prex, w_xq, kx, vx, w_xo, g_postx, *next_weights)
    return out[0], tuple(out[1:])


def _ffn_kernel(x_ref, gpre_ref, w1_ref, w2_ref, gpost_ref, o_ref):
    rows = x_ref.shape[0] // ROW_GROUPS
    for r in range(ROW_GROUPS):
        g = slice(r * rows, (r + 1) * rows)
        x = x_ref[g, :]
        h = _rms(x, gpre_ref[...]).astype(jnp.bfloat16)
        a = jnp.maximum(jnp.dot(h, w1_ref[...], preferred_element_type=jnp.float32), 0.0)
        f = jnp.dot((a * a).astype(jnp.bfloat16), w2_ref[...], preferred_element_type=jnp.float32)
        o_ref[g, :] = x + _rms(f, gpost_ref[...])


def _ffn(x, l, g_pre, w1, w2, g_post):
    bn, s, d = x.shape
    tm = min(ROW_GROUPS * TOKEN_TILE, s)
    dff = w1.shape[-1]
    lsel3 = lambda b, i: (l, 0, 0)
    first3 = lambda b, i: (0, 0, 0)
    tok = pl.BlockSpec((None, tm, d), lambda b, i: (b, i, 0))
    gain = pl.BlockSpec((None, 1, d), lsel3)
    return pl.pallas_call(
        _ffn_kernel,
        grid=(bn, s // tm),
        in_specs=[tok, gain, _const_spec((None, d, dff), first3),
                  _const_spec((None, dff, d), first3), gain],
        out_specs=tok,
        out_shape=jax.ShapeDtypeStruct((bn, s, d), jnp.float32),
        compiler_params=pltpu.CompilerParams(
            dimension_semantics=("arbitrary", "arbitrary"), vmem_limit_bytes=VMEM_LIMIT),
        name="ffn",
    )(x, g_pre, w1, w2, g_post)


def kernel(x, mem, pre_mix_g, w_in, gate_ln_g, gate_ln_b, w_s, b_s, w_dw, b_dw, conv_gn_g, conv_gn_b, w_out, post_mix_g, pre_x_g, mem_g, w_xq, w_xkv, w_xo, post_x_g, pre_ffn_g, w_ff1, w_ff2, post_ffn_g):
    depth, d, _ = w_in.shape
    a_width = gate_ln_g.shape[-1]
    c_width = w_dw.shape[-1]
    b_width = d - a_width - c_width
    a_heads = a_width // HEAD_DIM
    b_heads = b_width // HEAD_DIM
    assert w_in.shape[-1] == 2 * a_width + 3 * b_width + 2 * c_width
    assert w_s.shape[1:] == (a_heads, CHUNK, CHUNK) and w_dw.shape[1] == CONV_WIDTH
    assert x.shape[1] % MOBA_BLOCK == 0 and x.shape[1] % TOKEN_TILE == 0

    row = lambda g: g[:, None, :]
    ws_cat = w_s.transpose(0, 2, 1, 3).reshape(depth, CHUNK, a_heads * CHUNK)
    bs_x = jnp.repeat(b_s.transpose(0, 2, 1), HEAD_DIM, axis=-1)
    slopes = 2.0 ** (-8.0 * jnp.arange(1, b_heads + 1, dtype=jnp.float32) / b_heads)
    layer_weights = (w_in, w_out, w_xq, w_xo, w_ff1, w_ff2)
    kx, vx, (w_in_l, w_out_l, w_xq_l, w_xo_l, w_ff1_l, w_ff2_l) = _mem_kv(
        mem, row(mem_g), w_xkv, layer_weights)
    for l in range(depth):
        ya, q, k, v, yc = _mix_in(
            x, l, row(pre_mix_g), w_in_l, row(gate_ln_g), row(gate_ln_b), ws_cat, bs_x, w_dw,
            row(b_dw), row(conv_gn_g), row(conv_gn_b),
            a_width=a_width, b_width=b_width, c_width=c_width)
        yb = _moba(q, k, v, slopes)
        x, next_l = _mix_out(x, ya, yb, yc, l, w_out_l, row(post_mix_g), row(pre_x_g), w_xq_l, kx, vx,
                             w_xo_l, row(post_x_g), layer_weights if l + 1 < depth else ())
        x = _ffn(x, l, row(pre_ffn_g), w_ff1_l, w_ff2_l, row(post_ffn_g))
        if next_l:
            w_in_l, w_out_l, w_xq_l, w_xo_l, w_ff1_l, w_ff2_l = next_l
    return x
```

```python
import functools
import math

import jax
import jax.numpy as jnp
from jax import lax
from jax.experimental import pallas as pl
from jax.experimental.pallas import tpu as pltpu

HEAD_DIM = 64
CHUNK = 128
MOBA_BLOCK = 256
MOBA_TOPK = 3
CONV_WIDTH = 31
X_HEADS = 4
EPS = 1e-6
LOG2_E = math.log2(math.e)

LANES = 128
SUBLANES = 8
HALO = 32
CONV_ROWS = 128
TOKEN_TILE = 512
ROW_GROUPS = 2
MIX_IN_TILE = 1024
BF16_SUBLANES = 16
VMEM_LIMIT = 56 * 1024 * 1024

NEG_INF = float("-inf")


def _rms(x, g):
    return x * lax.rsqrt(jnp.mean(x * x, axis=-1, keepdims=True) + EPS) * g


def _const_spec(shape, index_map):
    return pl.BlockSpec(shape, index_map, pipeline_mode=pl.Buffered(1))


def _cast_rider_specs(weights, layer, n_steps, step_of):
    in_specs, out_specs, out_shapes = [], [], []
    for w in weights:
        _, rows, cols = w.shape
        assert rows % (n_steps * BF16_SUBLANES) == 0
        chunk = (None, rows // n_steps, cols)
        in_specs.append(pl.BlockSpec(chunk, lambda *g: (layer, step_of(*g), 0)))
        out_specs.append(pl.BlockSpec(chunk, lambda *g: (0, step_of(*g), 0)))
        out_shapes.append(jax.ShapeDtypeStruct((1, rows, cols), jnp.bfloat16))
    return in_specs, out_specs, out_shapes


def _cast_riders(src_refs, dst_refs):
    for src_ref, dst_ref in zip(src_refs, dst_refs):
        dst_ref[...] = src_ref[...].astype(jnp.bfloat16)


def _mem_kv_kernel(mem_ref, g_ref, w_ref, *rest):
    n_cast = (len(rest) - 2) // 2
    k_ref, v_ref = rest[n_cast:n_cast + 2]
    _cast_riders(rest[:n_cast], rest[n_cast + 2:])
    d = mem_ref.shape[-1]
    m = _rms(mem_ref[...], g_ref[...]).astype(jnp.bfloat16)
    kv = jnp.dot(m, w_ref[...].astype(jnp.bfloat16), preferred_element_type=jnp.float32)
    k_ref[...] = kv[:, :d].astype(jnp.bfloat16)
    v_ref[...] = kv[:, d:].astype(jnp.bfloat16)


def _mem_kv(mem, mem_g, w_xkv, layer_weights):
    bn, m, d = mem.shape
    depth = w_xkv.shape[0]
    out = jax.ShapeDtypeStruct((depth, bn, m, d), jnp.bfloat16)
    cast_in, cast_out, cast_shape = _cast_rider_specs(
        layer_weights, 0, depth * bn, lambda l, b: l * bn + b)
    res = pl.pallas_call(
        _mem_kv_kernel,
        grid=(depth, bn),
        in_specs=[
            pl.BlockSpec((None, m, d), lambda l, b: (b, 0, 0)),
            pl.BlockSpec((None, 1, d), lambda l, b: (l, 0, 0)),
            pl.BlockSpec((None, d, 2 * d), lambda l, b: (l, 0, 0)),
        ] + cast_in,
        out_specs=[
            pl.BlockSpec((None, None, m, d), lambda l, b: (l, b, 0, 0)),
            pl.BlockSpec((None, None, m, d), lambda l, b: (l, b, 0, 0)),
        ] + cast_out,
        out_shape=[out, out] + cast_shape,
        compiler_params=pltpu.CompilerParams(
            dimension_semantics=("arbitrary", "arbitrary"), vmem_limit_bytes=VMEM_LIMIT),
        name="mem_kv",
    )(mem, mem_g, w_xkv, *layer_weights)
    return res[0], res[1], tuple(res[2:])


def _mix_in_kernel(x_ref, g_ref, win_ref, lng_ref, lnb_ref, ws_ref, bs_ref, wdw_ref, bdw_ref,
                   gng_ref, gnb_ref, *rest, a_width, b_width, c_width):
    n_cast = (len(rest) - 8) // 2
    ya_ref, q_ref, k_ref, v_ref, yc_ref = rest[n_cast:n_cast + 5]
    hist_ref, conv_ref, shift_ref = rest[-3:]
    _cast_riders(rest[:n_cast], rest[n_cast + 5:-3])
    tm = x_ref.shape[0]

    @pl.when(pl.program_id(1) == 0)
    def _():
        hist_ref[0:HALO, :] = jnp.zeros((HALO, c_width), jnp.float32)

    h = _rms(x_ref[...], g_ref[...]).astype(jnp.bfloat16)

    def project(lo, hi):
        return jnp.dot(h, win_ref[:, lo:hi], preferred_element_type=jnp.float32)

    o = 2 * a_width + 3 * b_width
    zc = project(o, o + 2 * c_width)
    hist_ref[HALO:HALO + tm, :] = zc[:, :c_width] * (1.0 / (1.0 + jnp.exp(-zc[:, c_width:])))

    o = 2 * a_width
    q = (project(o, o + b_width) * (HEAD_DIM ** -0.5 * LOG2_E)).astype(jnp.bfloat16)
    k = project(o + b_width, o + 2 * b_width).astype(jnp.bfloat16)
    v = project(o + 2 * b_width, o + 3 * b_width).astype(jnp.bfloat16)
    za = project(0, 2 * a_width)

    first = HALO - (CONV_WIDTH - 1)
    for c in range(tm // CONV_ROWS):
        r0 = c * CONV_ROWS
        acc = jnp.zeros((CONV_ROWS, c_width), jnp.float32)
        for phase in range(SUBLANES):
            taps = [j for j in range(CONV_WIDTH) if (first + j) % SUBLANES == phase]
            span = first + taps[-1] - phase
            rows = CONV_ROWS + span
            if phase:
                shift_ref[phase, 0:rows, :] = hist_ref[r0 + phase:r0 + phase + rows, :]
            for j in taps:
                off = first + j - phase
                window = (shift_ref[phase, off:off + CONV_ROWS, :] if phase else
                          hist_ref[r0 + off:r0 + off + CONV_ROWS, :])
                acc = acc + wdw_ref[j:j + 1, :] * window
        conv_ref[r0:r0 + CONV_ROWS, :] = acc + bdw_ref[...]
    hist_ref[0:HALO, :] = hist_ref[tm:tm + HALO, :]
    q_ref[...] = q
    k_ref[...] = k
    v_ref[...] = v

    ga = 0.5 * za * (1.0 + lax.erf(za * (2.0 ** -0.5)))
    u = ga[:, :a_width]
    vv = ga[:, a_width:]
    mu = jnp.mean(vv, axis=-1, keepdims=True)
    vc = vv - mu
    var = jnp.mean(vc * vc, axis=-1, keepdims=True)
    vn = (vc * lax.rsqrt(var + EPS) * lng_ref[...] + lnb_ref[...]).astype(jnp.bfloat16)
    n_heads = a_width // HEAD_DIM
    t_idx = lax.broadcasted_iota(jnp.int32, (CHUNK, n_heads * CHUNK), 0)
    s_idx = lax.broadcasted_iota(jnp.int32, (CHUNK, n_heads * CHUNK), 1) % CHUNK
    w_tril = jnp.where(s_idx <= t_idx, ws_ref[...], 0.0).astype(jnp.bfloat16)
    r_head = lax.broadcasted_iota(jnp.int32, (n_heads * CHUNK, a_width), 0) // CHUNK
    c_head = lax.broadcasted_iota(jnp.int32, (n_heads * CHUNK, a_width), 1) // HEAD_DIM
    head_match = r_head == c_head
    for c in range(tm // CHUNK):
        rows = slice(c * CHUNK, (c + 1) * CHUNK)
        v_chunk = vn[rows]
        v_bd = jnp.where(head_match, jnp.concatenate([v_chunk] * n_heads, axis=0), 0.0)
        mixed = jnp.dot(w_tril, v_bd.astype(jnp.bfloat16), preferred_element_type=jnp.float32)
        ya_ref[rows, :] = (u[rows] * (mixed + bs_ref[...])).astype(jnp.bfloat16)

    gr = lax.broadcasted_iota(jnp.int32, (2 * c_width, c_width), 0) % c_width // HEAD_DIM
    gc = lax.broadcasted_iota(jnp.int32, (2 * c_width, c_width), 1) // HEAD_DIM
    gmat2 = jnp.where(gr == gc, 1.0 / HEAD_DIM, 0.0).astype(jnp.bfloat16)

    def group_mean(a):
        hi = a.astype(jnp.bfloat16)
        lo = (a - hi.astype(jnp.float32)).astype(jnp.bfloat16)
        return jnp.dot(jnp.concatenate([hi, lo], axis=-1), gmat2, preferred_element_type=jnp.float32)

    y = conv_ref[...]
    yd = y - group_mean(y)
    yn = yd * lax.rsqrt(group_mean(yd * yd) + EPS) * gng_ref[...] + gnb_ref[...]
    yc_ref[...] = (yn * (1.0 / (1.0 + jnp.exp(-yn)))).astype(jnp.bfloat16)


def _mix_in(x, l, pre_g, w_in, ln_g, ln_b, ws_cat, bs_x, w_dw, b_dw, gn_g, gn_b, own_weights,
            *, a_width, b_width, c_width):
    bn, s, d = x.shape
    tm = min(MIX_IN_TILE, s)
    n_i = s // tm
    in_cols = w_in.shape[-1]
    n_heads = a_width // HEAD_DIM
    lsel3 = lambda b, i: (l, 0, 0)
    tok = lambda w: pl.BlockSpec((None, tm, w), lambda b, i: (b, i, 0))
    bf = lambda w: jax.ShapeDtypeStruct((bn, s, w), jnp.bfloat16)
    cast_in, cast_out, cast_shape = _cast_rider_specs(
        own_weights, l, bn * n_i, lambda b, i: b * n_i + i)
    res = pl.pallas_call(
        functools.partial(_mix_in_kernel, a_width=a_width, b_width=b_width, c_width=c_width),
        grid=(bn, s // tm),
        in_specs=[
            tok(d),
            pl.BlockSpec((None, 1, d), lsel3),
            _const_spec((None, d, in_cols), lambda b, i: (0, 0, 0)),
            pl.BlockSpec((None, 1, a_width), lsel3),
            pl.BlockSpec((None, 1, a_width), lsel3),
            pl.BlockSpec((None, CHUNK, n_heads * CHUNK), lsel3),
            pl.BlockSpec((None, CHUNK, a_width), lsel3),
            pl.BlockSpec((None, CONV_WIDTH, c_width), lsel3),
            pl.BlockSpec((None, 1, c_width), lsel3),
            pl.BlockSpec((None, 1, c_width), lsel3),
            pl.BlockSpec((None, 1, c_width), lsel3),
        ] + cast_in,
        out_specs=[tok(a_width), tok(b_width), tok(b_width), tok(b_width), tok(c_width)] + cast_out,
        out_shape=[bf(a_width), bf(b_width), bf(b_width), bf(b_width), bf(c_width)] + cast_shape,
        scratch_shapes=[pltpu.VMEM((tm + HALO, c_width), jnp.float32),
                        pltpu.VMEM((tm, c_width), jnp.float32),
                        pltpu.VMEM((SUBLANES, CONV_ROWS + HALO, c_width), jnp.float32)],
        compiler_params=pltpu.CompilerParams(
            dimension_semantics=("arbitrary", "arbitrary"), vmem_limit_bytes=VMEM_LIMIT),
        name="mix_in",
    )(x, pre_g, w_in, ln_g, ln_b, ws_cat, bs_x, w_dw, b_dw, gn_g, gn_b, *own_weights)
    return tuple(res[:5]), tuple(res[5:])


ALIBI_TERMS = 4
ALIBI_PIECES = 3
ALIBI_ROWS = 16
HEADS_PER_TILE = LANES // HEAD_DIM
PV_ROWS = HEAD_DIM + BF16_SUBLANES
GATE_PIECES = 3
SELECT_TILES = 4


def _own_lanes(h, idx):
    hh = h % HEADS_PER_TILE
    return (idx >= hh * HEAD_DIM) & (idx < (hh + 1) * HEAD_DIM)


def _slot_base(h):
    return ((h + 1) % HEADS_PER_TILE) * HEAD_DIM


def _moba_kernel(slopes_ref, q_ref, k_ref, v_ref, o_ref, km_ref, kaug_ref, vt_ref, qaug_ref, neg_ref,
                 s_ref, p_ref, acc_ref, *, n_blocks):
    blk = MOBA_BLOCK
    n_tiles = q_ref.shape[-1] // LANES

    def prepare():
        lane = lax.broadcasted_iota(jnp.int32, (blk, LANES), 1)
        j_pos = lax.broadcasted_iota(jnp.int32, (blk, LANES), 0).astype(jnp.float32)
        ones_rows = (lax.broadcasted_iota(jnp.int32, (BF16_SUBLANES, blk), 0) == 0
                     ).astype(jnp.float32)
        km_row = lax.broadcasted_iota(jnp.int32, (n_blocks, LANES), 0)
        km_lane = lax.broadcasted_iota(jnp.int32, (n_blocks, LANES), 1)
        sum_rows = jnp.ones((BF16_SUBLANES, blk), jnp.bfloat16)
        for t in range(n_tiles):
            cols = slice(t * LANES, (t + 1) * LANES)
            kmean = jnp.zeros((n_blocks, LANES), jnp.float32)
            for n in range(n_blocks):
                rows = slice(n * blk, (n + 1) * blk)
                kb = k_ref[rows, cols]
                k_sum = jnp.dot(sum_rows, kb, preferred_element_type=jnp.float32)[:1]
                kmean = jnp.where(km_row == n, k_sum * (1.0 / blk), kmean)
                v_t = v_ref[rows, cols].astype(jnp.float32).T
                for hh in range(HEADS_PER_TILE):
                    h = t * HEADS_PER_TILE + hh
                    term = (lane - _slot_base(h)) // ALIBI_PIECES
                    aug = jnp.where(term == 0, j_pos,
                                    jnp.where(term == 1, float(n),
                                              jnp.where((term == 2) | (term == 3), 1.0, 0.0)))
                    kaug_ref[h, n] = jnp.where(_own_lanes(h, lane), kb, aug.astype(jnp.bfloat16))
                    vt_ref[h, n] = jnp.concatenate(
                        [v_t[hh * HEAD_DIM:(hh + 1) * HEAD_DIM], ones_rows], axis=0).astype(jnp.bfloat16)
            for hh in range(HEADS_PER_TILE):
                h = t * HEADS_PER_TILE + hh
                rest = jnp.where(_own_lanes(h, km_lane), kmean, 0.0)
                pieces = []
                for _ in range(GATE_PIECES):
                    piece = rest.astype(jnp.bfloat16).astype(jnp.float32)
                    pieces.append(piece)
                    rest = rest - piece
                km_ref[h] = jnp.concatenate(pieces[::-1], axis=0).astype(jnp.bfloat16)

    prepare()

    def select(g, carry):
        _moba_select(g, slopes_ref, q_ref, km_ref, qaug_ref, neg_ref, n_blocks=n_blocks)
        return carry

    lax.fori_loop(0, n_blocks // SELECT_TILES, select, 0)

    _moba_open(0, kaug_ref, qaug_ref, s_ref)

    def sweep(qi, carry):
        _moba_tile(qi, o_ref, kaug_ref, vt_ref, qaug_ref, neg_ref, s_ref, p_ref, acc_ref,
                   n_blocks=n_blocks)
        _moba_open(jnp.minimum(qi + 1, n_blocks - 1), kaug_ref, qaug_ref, s_ref)
        return carry

    lax.fori_loop(0, n_blocks, sweep, 0)


def _moba_select(g, slopes_ref, q_ref, km_ref, qaug_ref, neg_ref, *, n_blocks):
    blk = MOBA_BLOCK
    width = SELECT_TILES * blk
    n_tiles = q_ref.shape[-1] // LANES
    n_heads = n_tiles * HEADS_PER_TILE
    hg = pl.program_id(1)
    first_tile = jnp.asarray(g, jnp.int32) * SELECT_TILES
    q_rows = pl.ds(pl.multiple_of(first_tile * blk, width), width)

    def tile_of(shape):
        return first_tile + lax.broadcasted_iota(jnp.int32, shape, 1) // blk

    slot = lax.broadcasted_iota(jnp.int32, (ALIBI_ROWS, width), 0)
    i_pos = (lax.broadcasted_iota(jnp.int32, (ALIBI_ROWS, width), 1) % blk).astype(jnp.float32)
    tile_f = tile_of((ALIBI_ROWS, width)).astype(jnp.float32)
    pad = jnp.zeros((HEAD_DIM - ALIBI_ROWS, width), jnp.float32)
    blk_id = lax.broadcasted_iota(jnp.int32, (n_blocks, width), 0)
    past = blk_id < tile_of((n_blocks, width))
    for t in range(n_tiles):
        q_t = q_ref[q_rows, t * LANES:(t + 1) * LANES].astype(jnp.float32).T
        for hh in range(HEADS_PER_TILE):
            h = t * HEADS_PER_TILE + hh
            c = slopes_ref[hg * n_heads + h] * LOG2_E
            term = slot // ALIBI_PIECES
            full = jnp.where(term == 0, c,
                             jnp.where(term == 1, c * blk,
                                       jnp.where(term == 2, -(c * (blk * tile_f)),
                                                 jnp.where(term == 3, -(c * i_pos), 0.0))))
            piece = full.astype(jnp.bfloat16).astype(jnp.float32)
            aug = piece
            for r in range(1, ALIBI_PIECES):
                full = full - piece
                piece = full.astype(jnp.bfloat16).astype(jnp.float32)
                aug = jnp.where(slot % ALIBI_PIECES == r, piece, aug)
            own = q_t[hh * HEAD_DIM:(hh + 1) * HEAD_DIM]
            rows = [own, aug, pad] if _slot_base(h) > 0 else [aug, pad, own]
            q_aug = jnp.concatenate(rows, axis=0).astype(jnp.bfloat16)
            for j in range(SELECT_TILES):
                qaug_ref[h, first_tile + j] = q_aug[:, j * blk:(j + 1) * blk]

            g3 = jnp.dot(km_ref[h], q_aug, preferred_element_type=jnp.float32)
            gate = g3[:n_blocks] + g3[n_blocks:2 * n_blocks] + g3[2 * n_blocks:]
            gate = jnp.where(past, gate, NEG_INF)
            blk_f = blk_id.astype(jnp.float32)
            chosen = jnp.zeros((n_blocks, width), jnp.float32)
            for _ in range(MOBA_TOPK):
                best = jnp.max(gate, axis=0, keepdims=True)
                pick = jnp.min(jnp.where(gate == best, blk_f, float(n_blocks)), axis=0, keepdims=True)
                taken = blk_f == pick
                chosen = jnp.where(taken, 1.0, chosen)
                gate = jnp.where(taken, NEG_INF, gate)
            neg = jnp.where((chosen > 0.5) & past, 0.0, NEG_INF)
            for j in range(SELECT_TILES):
                neg_ref[h, first_tile + j, 0] = jnp.zeros((1, blk), jnp.float32)
                for u in range(1, n_blocks):
                    neg_ref[h, first_tile + j, u] = neg[u - 1:u, j * blk:(j + 1) * blk]


def _moba_open(qi, kaug_ref, qaug_ref, s_ref):
    blk = MOBA_BLOCK
    key_i = lax.broadcasted_iota(jnp.int32, (blk, blk), 0)
    qry_i = lax.broadcasted_iota(jnp.int32, (blk, blk), 1)
    causal = key_i <= qry_i
    for h in range(kaug_ref.shape[0]):
        s = jnp.dot(kaug_ref[h, qi], qaug_ref[h, qi], preferred_element_type=jnp.float32)
        s_ref[0, h] = jnp.where(causal, s, NEG_INF)


def _moba_tile(qi, o_ref, kaug_ref, vt_ref, qaug_ref, neg_ref, s_ref, p_ref, acc_ref, *, n_blocks):
    blk = MOBA_BLOCK
    n_heads = kaug_ref.shape[0]
    n_tiles = n_heads // HEADS_PER_TILE
    q_rows = pl.ds(pl.multiple_of(qi * blk, blk), blk)
    q_aug = [qaug_ref[h, qi] for h in range(n_heads)]
    for h in range(n_heads):
        p_ref[1, h] = jnp.zeros((blk, blk), jnp.bfloat16)
        acc_ref[h] = jnp.zeros((PV_ROWS, blk), jnp.float32)

    def block_of(u):
        return jnp.where(u == 0, qi, jnp.minimum(u - 1, n_blocks - 1))

    def step(u, cur, m_runs):
        b_prev = block_of(jnp.maximum(u - 1, 0))
        k_next = jnp.minimum(u, n_blocks - 1)
        pv = [jnp.dot(vt_ref[h, b_prev], p_ref[1 - cur, h], preferred_element_type=jnp.float32)
              for h in range(n_heads)]
        for h in range(n_heads):
            s_ref[1 - cur, h] = jnp.dot(kaug_ref[h, k_next], q_aug[h],
                                        preferred_element_type=jnp.float32)
        m_news = []
        for h in range(n_heads):
            s = s_ref[cur, h]
            neg = neg_ref[h, qi, u]
            m_new = jnp.maximum(m_runs[h], jnp.max(s, axis=0, keepdims=True) + neg)
            alpha = jnp.exp2(m_runs[h] - m_new)
            p = jnp.exp2(s - (m_new - neg))
            p_ref[cur, h] = p.astype(jnp.bfloat16)
            acc_ref[h] = alpha * (acc_ref[h] + pv[h])
            m_news.append(m_new)
        return tuple(m_news)

    def step_pair(i, m_runs):
        return step(2 * i + 1, 1, step(2 * i, 0, m_runs))

    m_init = jnp.full((1, blk), NEG_INF, jnp.float32)
    lax.fori_loop(0, (qi + 2) // 2, step_pair, (m_init,) * n_heads)
    last = 2 * ((qi + 2) // 2) - 1
    for t in range(n_tiles):
        outs = []
        for h in range(t * HEADS_PER_TILE, (t + 1) * HEADS_PER_TILE):
            acc = acc_ref[h] + jnp.dot(vt_ref[h, block_of(last)], p_ref[1, h],
                                       preferred_element_type=jnp.float32)
            outs.append(acc[:HEAD_DIM] / acc[HEAD_DIM:HEAD_DIM + 1])
        o_ref[q_rows, t * LANES:(t + 1) * LANES] = jnp.concatenate(outs, axis=0).T.astype(jnp.bfloat16)


MOBA_TILES_PER_STEP = 2


def _moba(q, k, v, slopes):
    bn, s, width = q.shape
    step_w = MOBA_TILES_PER_STEP * LANES
    assert s % MOBA_BLOCK == 0 and width % step_w == 0
    n_blocks = s // MOBA_BLOCK
    n_heads = MOBA_TILES_PER_STEP * HEADS_PER_TILE
    assert ALIBI_TERMS * ALIBI_PIECES <= ALIBI_ROWS <= HEAD_DIM and HEADS_PER_TILE == 2
    assert n_blocks <= 256 and MOBA_BLOCK <= 256
    seq = pl.BlockSpec((None, s, step_w), lambda b, hg: (b, 0, hg))
    return pl.pallas_call(
        functools.partial(_moba_kernel, n_blocks=n_blocks),
        grid=(bn, width // step_w),
        in_specs=[pl.BlockSpec(memory_space=pltpu.SMEM), seq, seq, seq],
        out_specs=seq,
        out_shape=jax.ShapeDtypeStruct((bn, s, width), jnp.bfloat16),
        scratch_shapes=[
            pltpu.VMEM((n_heads, GATE_PIECES * n_blocks, LANES), jnp.bfloat16),
            pltpu.VMEM((n_heads, n_blocks, MOBA_BLOCK, LANES), jnp.bfloat16),
            pltpu.VMEM((n_heads, n_blocks, PV_ROWS, MOBA_BLOCK), jnp.bfloat16),
            pltpu.VMEM((n_heads, n_blocks, LANES, MOBA_BLOCK), jnp.bfloat16),
            pltpu.VMEM((n_heads, n_blocks, n_blocks, 1, MOBA_BLOCK), jnp.float32),
            pltpu.VMEM((2, n_heads, MOBA_BLOCK, MOBA_BLOCK), jnp.float32),
            pltpu.VMEM((2, n_heads, MOBA_BLOCK, MOBA_BLOCK), jnp.bfloat16),
            pltpu.VMEM((n_heads, PV_ROWS, MOBA_BLOCK), jnp.float32),
        ],
        compiler_params=pltpu.CompilerParams(
            dimension_semantics=("arbitrary", "arbitrary"),
            vmem_limit_bytes=VMEM_LIMIT),
        name="moba",
    )(slopes, q, k, v)


def _mix_out_kernel(x_ref, ya_ref, yb_ref, yc_ref, wout_ref, gpost_ref, gprex_ref, wq_ref,
                    kx_ref, vx_ref, wo_ref, gpostx_ref, *rest):
    n_cast = len(rest) // 2
    o_ref = rest[n_cast]
    _cast_riders(rest[:n_cast], rest[n_cast + 1:])

    d = x_ref.shape[-1]
    dh = d // X_HEADS
    nt = (((1,), (1,)), ((), ()))
    rows = x_ref.shape[0] // ROW_GROUPS
    groups = [slice(r * rows, (r + 1) * rows) for r in range(ROW_GROUPS)]

    ys = [jnp.concatenate([ya_ref[g, :], yb_ref[g, :], yc_ref[g, :]], axis=-1) for g in groups]
    mixes = [jnp.dot(y, wout_ref[...], preferred_element_type=jnp.float32) for y in ys]
    x1s = [x_ref[g, :] + _rms(mix, gpost_ref[...]) for g, mix in zip(groups, mixes)]
    hs = [_rms(x1, gprex_ref[...]).astype(jnp.bfloat16) for x1 in x1s]
    qxs = [(jnp.dot(h, wq_ref[...], preferred_element_type=jnp.float32) * (dh ** -0.5)
            ).astype(jnp.bfloat16) for h in hs]
    outs = []
    for qx in qxs:
        heads = []
        for hh in range(X_HEADS):
            cols = slice(hh * dh, (hh + 1) * dh)
            s = lax.dot_general(qx[:, cols], kx_ref[:, cols], nt, preferred_element_type=jnp.float32)
            p = jnp.exp(s - jnp.max(s, axis=-1, keepdims=True))
            denom = jnp.sum(p, axis=-1, keepdims=True)
            oh = jnp.dot(p.astype(jnp.bfloat16), vx_ref[:, cols], preferred_element_type=jnp.float32)
            heads.append((oh / denom).astype(jnp.bfloat16))
        outs.append(jnp.concatenate(heads, axis=-1))
    atts = [jnp.dot(o, wo_ref[...], preferred_element_type=jnp.float32) for o in outs]
    for g, x1, att in zip(groups, x1s, atts):
        o_ref[g, :] = x1 + _rms(att, gpostx_ref[...])


def _mix_out(x, ya, yb, yc, l, w_out, g_post, g_prex, w_xq, kx, vx, w_xo, g_postx, next_weights):
    bn, s, d = x.shape
    tm = min(ROW_GROUPS * TOKEN_TILE, s)
    m = kx.shape[2]
    n_i = s // tm
    lsel3 = lambda b, i: (l, 0, 0)
    first3 = lambda b, i: (0, 0, 0)
    tok = lambda w: pl.BlockSpec((None, tm, w), lambda b, i: (b, i, 0))
    gain = pl.BlockSpec((None, 1, d), lsel3)
    wsq = _const_spec((None, d, d), first3)
    mem = pl.BlockSpec((None, None, m, d), lambda b, i: (l, b, 0, 0))
    cast_in, cast_out, cast_shape = _cast_rider_specs(
        next_weights, l + 1, bn * n_i, lambda b, i: b * n_i + i)
    out = pl.pallas_call(
        _mix_out_kernel,
        grid=(bn, n_i),
        in_specs=[tok(d), tok(ya.shape[-1]), tok(yb.shape[-1]), tok(yc.shape[-1]),
                  wsq, gain, gain, wsq, mem, mem, wsq, gain] + cast_in,
        out_specs=[tok(d)] + cast_out,
        out_shape=[jax.ShapeDtypeStruct((bn, s, d), jnp.float32)] + cast_shape,
        compiler_params=pltpu.CompilerParams(
            dimension_semantics=("arbitrary", "arbitrary"), vmem_limit_bytes=VMEM_LIMIT),
        name="mix_out",
    )(x, ya, yb, yc, w_out, g_post, g_prex, w_xq, kx, vx, w_xo, g_postx, *next_weights)
    return out[0], tuple(out[1:])


def _ffn_kernel(x_ref, gpre_ref, w1_ref, w2_ref, gpost_ref, o_ref):
    rows = x_ref.shape[0] // ROW_GROUPS
    for r in range(ROW_GROUPS):
        g = slice(r * rows, (r + 1) * rows)
        x = x_ref[g, :]
        h = _rms(x, gpre_ref[...]).astype(jnp.bfloat16)
        a = jnp.maximum(jnp.dot(h, w1_ref[...], preferred_element_type=jnp.float32), 0.0)
        f = jnp.dot((a * a).astype(jnp.bfloat16), w2_ref[...], preferred_element_type=jnp.float32)
        o_ref[g, :] = x + _rms(f, gpost_ref[...])


def _ffn(x, l, g_pre, w1, w2, g_post):
    bn, s, d = x.shape
    tm = min(ROW_GROUPS * TOKEN_TILE, s)
    dff = w1.shape[-1]
    lsel3 = lambda b, i: (l, 0, 0)
    first3 = lambda b, i: (0, 0, 0)
    tok = pl.BlockSpec((None, tm, d), lambda b, i: (b, i, 0))
    gain = pl.BlockSpec((None, 1, d), lsel3)
    return pl.pallas_call(
        _ffn_kernel,
        grid=(bn, s // tm),
        in_specs=[tok, gain, _const_spec((None, d, dff), first3),
                  _const_spec((None, dff, d), first3), gain],
        out_specs=tok,
        out_shape=jax.ShapeDtypeStruct((bn, s, d), jnp.float32),
        compiler_params=pltpu.CompilerParams(
            dimension_semantics=("arbitrary", "arbitrary"), vmem_limit_bytes=VMEM_LIMIT),
        name="ffn",
    )(x, g_pre, w1, w2, g_post)


def kernel(x, mem, pre_mix_g, w_in, gate_ln_g, gate_ln_b, w_s, b_s, w_dw, b_dw, conv_gn_g, conv_gn_b, w_out, post_mix_g, pre_x_g, mem_g, w_xq, w_xkv, w_xo, post_x_g, pre_ffn_g, w_ff1, w_ff2, post_ffn_g):
    depth, d, _ = w_in.shape
    a_width = gate_ln_g.shape[-1]
    c_width = w_dw.shape[-1]
    b_width = d - a_width - c_width
    a_heads = a_width // HEAD_DIM
    b_heads = b_width // HEAD_DIM
    assert w_in.shape[-1] == 2 * a_width + 3 * b_width + 2 * c_width
    assert w_s.shape[1:] == (a_heads, CHUNK, CHUNK) and w_dw.shape[1] == CONV_WIDTH
    assert x.shape[1] % MOBA_BLOCK == 0 and x.shape[1] % TOKEN_TILE == 0

    row = lambda g: g[:, None, :]
    ws_cat = w_s.transpose(0, 2, 1, 3).reshape(depth, CHUNK, a_heads * CHUNK)
    bs_x = jnp.repeat(b_s.transpose(0, 2, 1), HEAD_DIM, axis=-1)
    slopes = 2.0 ** (-8.0 * jnp.arange(1, b_heads + 1, dtype=jnp.float32) / b_heads)
    later_weights = (w_out, w_xq, w_xo, w_ff1, w_ff2)
    layer_weights = (w_in,) + later_weights
    kx, vx, (w_in_l,) = _mem_kv(mem, row(mem_g), w_xkv, (w_in,))
    for l in range(depth):
        (ya, q, k, v, yc), own_l = _mix_in(
            x, l, row(pre_mix_g), w_in_l, row(gate_ln_g), row(gate_ln_b), ws_cat, bs_x, w_dw,
            row(b_dw), row(conv_gn_g), row(conv_gn_b), later_weights if l == 0 else (),
            a_width=a_width, b_width=b_width, c_width=c_width)
        if own_l:
            w_out_l, w_xq_l, w_xo_l, w_ff1_l, w_ff2_l = own_l
        yb = _moba(q, k, v, slopes)
        x, next_l = _mix_out(x, ya, yb, yc, l, w_out_l, row(post_mix_g), row(pre_x_g), w_xq_l, kx, vx,
                             w_xo_l, row(post_x_g), layer_weights if l + 1 < depth else ())
        x = _ffn(x, l, row(pre_ffn_g), w_ff1_l, w_ff2_l, row(post_ffn_g))
        if next_l:
            w_in_l, w_out_l, w_xq_l, w_xo_l, w_ff1_l, w_ff2_l = next_l
    return x
```

```python
import functools
import math

import jax
import jax.numpy as jnp
from jax import lax
from jax.experimental import pallas as pl
from jax.experimental.pallas import tpu as pltpu

HEAD_DIM = 64
CHUNK = 128
MOBA_BLOCK = 256
MOBA_TOPK = 3
CONV_WIDTH = 31
X_HEADS = 4
EPS = 1e-6
LOG2_E = math.log2(math.e)

LANES = 128
SUBLANES = 8
HALO = 32
CONV_ROWS = 128
TOKEN_TILE = 512
ROW_GROUPS = 2
MIX_IN_TILE = 1024
BF16_SUBLANES = 16
VMEM_LIMIT = 56 * 1024 * 1024

NEG_INF = float("-inf")


def _rms(x, g):
    return x * lax.rsqrt(jnp.mean(x * x, axis=-1, keepdims=True) + EPS) * g


def _const_spec(shape, index_map):
    return pl.BlockSpec(shape, index_map, pipeline_mode=pl.Buffered(1))


def _cast_rider_specs(weights, layer, n_steps, step_of):
    in_specs, out_specs, out_shapes = [], [], []
    for w in weights:
        _, rows, cols = w.shape
        assert rows % (n_steps * BF16_SUBLANES) == 0
        chunk = (None, rows // n_steps, cols)
        in_specs.append(pl.BlockSpec(chunk, lambda *g: (layer, step_of(*g), 0)))
        out_specs.append(pl.BlockSpec(chunk, lambda *g: (0, step_of(*g), 0)))
        out_shapes.append(jax.ShapeDtypeStruct((1, rows, cols), jnp.bfloat16))
    return in_specs, out_specs, out_shapes


def _cast_riders(src_refs, dst_refs):
    for src_ref, dst_ref in zip(src_refs, dst_refs):
        dst_ref[...] = src_ref[...].astype(jnp.bfloat16)


def _mem_kv_kernel(mem_ref, g_ref, w_ref, *rest):
    n_cast = (len(rest) - 2) // 2
    k_ref, v_ref = rest[n_cast:n_cast + 2]
    _cast_riders(rest[:n_cast], rest[n_cast + 2:])
    d = mem_ref.shape[-1]
    m = _rms(mem_ref[...], g_ref[...]).astype(jnp.bfloat16)
    kv = jnp.dot(m, w_ref[...].astype(jnp.bfloat16), preferred_element_type=jnp.float32)
    k_ref[...] = kv[:, :d].astype(jnp.bfloat16)
    v_ref[...] = kv[:, d:].astype(jnp.bfloat16)


def _mem_kv(mem, mem_g, w_xkv, layer_weights):
    bn, m, d = mem.shape
    depth = w_xkv.shape[0]
    out = jax.ShapeDtypeStruct((depth, bn, m, d), jnp.bfloat16)
    cast_in, cast_out, cast_shape = _cast_rider_specs(
        layer_weights, 0, depth * bn, lambda l, b: l * bn + b)
    res = pl.pallas_call(
        _mem_kv_kernel,
        grid=(depth, bn),
        in_specs=[
            pl.BlockSpec((None, m, d), lambda l, b: (b, 0, 0)),
            pl.BlockSpec((None, 1, d), lambda l, b: (l, 0, 0)),
            pl.BlockSpec((None, d, 2 * d), lambda l, b: (l, 0, 0)),
        ] + cast_in,
        out_specs=[
            pl.BlockSpec((None, None, m, d), lambda l, b: (l, b, 0, 0)),
            pl.BlockSpec((None, None, m, d), lambda l, b: (l, b, 0, 0)),
        ] + cast_out,
        out_shape=[out, out] + cast_shape,
        compiler_params=pltpu.CompilerParams(
            dimension_semantics=("arbitrary", "arbitrary"), vmem_limit_bytes=VMEM_LIMIT),
        name="mem_kv",
    )(mem, mem_g, w_xkv, *layer_weights)
    return res[0], res[1], tuple(res[2:])


def _mix_in_kernel(x_ref, g_ref, win_ref, lng_ref, lnb_ref, ws_ref, bs_ref, wdw_ref, bdw_ref,
                   gng_ref, gnb_ref, *rest, a_width, b_width, c_width):
    n_cast = (len(rest) - 8) // 2
    ya_ref, q_ref, k_ref, v_ref, yc_ref = rest[n_cast:n_cast + 5]
    hist_ref, conv_ref, shift_ref = rest[-3:]
    _cast_riders(rest[:n_cast], rest[n_cast + 5:-3])
    tm = x_ref.shape[0]

    @pl.when(pl.program_id(1) == 0)
    def _():
        hist_ref[0:HALO, :] = jnp.zeros((HALO, c_width), jnp.float32)

    h = _rms(x_ref[...], g_ref[...]).astype(jnp.bfloat16)

    def project(lo, hi):
        return jnp.dot(h, win_ref[:, lo:hi], preferred_element_type=jnp.float32)

    o = 2 * a_width + 3 * b_width
    zc = project(o, o + 2 * c_width)
    hist_ref[HALO:HALO + tm, :] = zc[:, :c_width] * (1.0 / (1.0 + jnp.exp(-zc[:, c_width:])))

    o = 2 * a_width
    q = (project(o, o + b_width) * (HEAD_DIM ** -0.5 * LOG2_E)).astype(jnp.bfloat16)
    k = project(o + b_width, o + 2 * b_width).astype(jnp.bfloat16)
    v = project(o + 2 * b_width, o + 3 * b_width).astype(jnp.bfloat16)
    za = project(0, 2 * a_width)

    first = HALO - (CONV_WIDTH - 1)
    for c in range(tm // CONV_ROWS):
        r0 = c * CONV_ROWS
        acc = jnp.zeros((CONV_ROWS, c_width), jnp.float32)
        for phase in range(SUBLANES):
            taps = [j for j in range(CONV_WIDTH) if (first + j) % SUBLANES == phase]
            span = first + taps[-1] - phase
            rows = CONV_ROWS + span
            if phase:
                shift_ref[phase, 0:rows, :] = hist_ref[r0 + phase:r0 + phase + rows, :]
            for j in taps:
                off = first + j - phase
                window = (shift_ref[phase, off:off + CONV_ROWS, :] if phase else
                          hist_ref[r0 + off:r0 + off + CONV_ROWS, :])
                acc = acc + wdw_ref[j:j + 1, :] * window
        conv_ref[r0:r0 + CONV_ROWS, :] = acc + bdw_ref[...]
    hist_ref[0:HALO, :] = hist_ref[tm:tm + HALO, :]
    q_ref[...] = q
    k_ref[...] = k
    v_ref[...] = v

    ga = 0.5 * za * (1.0 + lax.erf(za * (2.0 ** -0.5)))
    u = ga[:, :a_width]
    vv = ga[:, a_width:]
    mu = jnp.mean(vv, axis=-1, keepdims=True)
    vc = vv - mu
    var = jnp.mean(vc * vc, axis=-1, keepdims=True)
    vn = (vc * lax.rsqrt(var + EPS) * lng_ref[...] + lnb_ref[...]).astype(jnp.bfloat16)
    n_heads = a_width // HEAD_DIM
    t_idx = lax.broadcasted_iota(jnp.int32, (CHUNK, n_heads * CHUNK), 0)
    s_idx = lax.broadcasted_iota(jnp.int32, (CHUNK, n_heads * CHUNK), 1) % CHUNK
    w_tril = jnp.where(s_idx <= t_idx, ws_ref[...], 0.0).astype(jnp.bfloat16)
    r_head = lax.broadcasted_iota(jnp.int32, (n_heads * CHUNK, a_width), 0) // CHUNK
    c_head = lax.broadcasted_iota(jnp.int32, (n_heads * CHUNK, a_width), 1) // HEAD_DIM
    head_match = r_head == c_head
    for c in range(tm // CHUNK):
        rows = slice(c * CHUNK, (c + 1) * CHUNK)
        v_chunk = vn[rows]
        v_bd = jnp.where(head_match, jnp.concatenate([v_chunk] * n_heads, axis=0), 0.0)
        mixed = jnp.dot(w_tril, v_bd.astype(jnp.bfloat16), preferred_element_type=jnp.float32)
        ya_ref[rows, :] = (u[rows] * (mixed + bs_ref[...])).astype(jnp.bfloat16)

    gr = lax.broadcasted_iota(jnp.int32, (2 * c_width, c_width), 0) % c_width // HEAD_DIM
    gc = lax.broadcasted_iota(jnp.int32, (2 * c_width, c_width), 1) // HEAD_DIM
    gmat2 = jnp.where(gr == gc, 1.0 / HEAD_DIM, 0.0).astype(jnp.bfloat16)

    def group_mean(a):
        hi = a.astype(jnp.bfloat16)
        lo = (a - hi.astype(jnp.float32)).astype(jnp.bfloat16)
        return jnp.dot(jnp.concatenate([hi, lo], axis=-1), gmat2, preferred_element_type=jnp.float32)

    y = conv_ref[...]
    yd = y - group_mean(y)
    yn = yd * lax.rsqrt(group_mean(yd * yd) + EPS) * gng_ref[...] + gnb_ref[...]
    yc_ref[...] = (yn * (1.0 / (1.0 + jnp.exp(-yn)))).astype(jnp.bfloat16)


def _mix_in(x, l, pre_g, w_in, ln_g, ln_b, ws_cat, bs_x, w_dw, b_dw, gn_g, gn_b, own_weights,
            *, a_width, b_width, c_width):
    bn, s, d = x.shape
    tm = min(MIX_IN_TILE, s)
    n_i = s // tm
    in_cols = w_in.shape[-1]
    n_heads = a_width // HEAD_DIM
    lsel3 = lambda b, i: (l, 0, 0)
    tok = lambda w: pl.BlockSpec((None, tm, w), lambda b, i: (b, i, 0))
    bf = lambda w: jax.ShapeDtypeStruct((bn, s, w), jnp.bfloat16)
    cast_in, cast_out, cast_shape = _cast_rider_specs(
        own_weights, l, bn * n_i, lambda b, i: b * n_i + i)
    res = pl.pallas_call(
        functools.partial(_mix_in_kernel, a_width=a_width, b_width=b_width, c_width=c_width),
        grid=(bn, s // tm),
        in_specs=[
            tok(d),
            pl.BlockSpec((None, 1, d), lsel3),
            _const_spec((None, d, in_cols), lambda b, i: (0, 0, 0)),
            pl.BlockSpec((None, 1, a_width), lsel3),
            pl.BlockSpec((None, 1, a_width), lsel3),
            pl.BlockSpec((None, CHUNK, n_heads * CHUNK), lsel3),
            pl.BlockSpec((None, CHUNK, a_width), lsel3),
            pl.BlockSpec((None, CONV_WIDTH, c_width), lsel3),
            pl.BlockSpec((None, 1, c_width), lsel3),
            pl.BlockSpec((None, 1, c_width), lsel3),
            pl.BlockSpec((None, 1, c_width), lsel3),
        ] + cast_in,
        out_specs=[tok(a_width), tok(b_width), tok(b_width), tok(b_width), tok(c_width)] + cast_out,
        out_shape=[bf(a_width), bf(b_width), bf(b_width), bf(b_width), bf(c_width)] + cast_shape,
        scratch_shapes=[pltpu.VMEM((tm + HALO, c_width), jnp.float32),
                        pltpu.VMEM((tm, c_width), jnp.float32),
                        pltpu.VMEM((SUBLANES, CONV_ROWS + HALO, c_width), jnp.float32)],
        compiler_params=pltpu.CompilerParams(
            dimension_semantics=("arbitrary", "arbitrary"), vmem_limit_bytes=VMEM_LIMIT),
        name="mix_in",
    )(x, pre_g, w_in, ln_g, ln_b, ws_cat, bs_x, w_dw, b_dw, gn_g, gn_b, *own_weights)
    return tuple(res[:5]), tuple(res[5:])


ALIBI_TERMS = 4
ALIBI_PIECES = 3
ALIBI_ROWS = 16
HEADS_PER_TILE = LANES // HEAD_DIM
PV_ROWS = HEAD_DIM + BF16_SUBLANES
GATE_PIECES = 3
SELECT_TILES = 4


def _own_lanes(h, idx):
    hh = h % HEADS_PER_TILE
    return (idx >= hh * HEAD_DIM) & (idx < (hh + 1) * HEAD_DIM)


def _slot_base(h):
    return ((h + 1) % HEADS_PER_TILE) * HEAD_DIM


def _moba_kernel(slopes_ref, q_ref, k_ref, v_ref, o_ref, km_ref, kaug_ref, vt_ref, qaug_ref, neg_ref,
                 s_ref, smax_ref, p_ref, acc_ref, *, n_blocks):
    blk = MOBA_BLOCK
    n_tiles = q_ref.shape[-1] // LANES

    def prepare():
        lane = lax.broadcasted_iota(jnp.int32, (blk, LANES), 1)
        j_pos = lax.broadcasted_iota(jnp.int32, (blk, LANES), 0).astype(jnp.float32)
        ones_rows = (lax.broadcasted_iota(jnp.int32, (BF16_SUBLANES, blk), 0) == 0
                     ).astype(jnp.float32)
        km_row = lax.broadcasted_iota(jnp.int32, (n_blocks, LANES), 0)
        km_lane = lax.broadcasted_iota(jnp.int32, (n_blocks, LANES), 1)
        sum_rows = jnp.ones((BF16_SUBLANES, blk), jnp.bfloat16)
        for t in range(n_tiles):
            cols = slice(t * LANES, (t + 1) * LANES)
            kmean = jnp.zeros((n_blocks, LANES), jnp.float32)
            for n in range(n_blocks):
                rows = slice(n * blk, (n + 1) * blk)
                kb = k_ref[rows, cols]
                k_sum = jnp.dot(sum_rows, kb, preferred_element_type=jnp.float32)[:1]
                kmean = jnp.where(km_row == n, k_sum * (1.0 / blk), kmean)
                v_t = v_ref[rows, cols].astype(jnp.float32).T
                for hh in range(HEADS_PER_TILE):
                    h = t * HEADS_PER_TILE + hh
                    term = (lane - _slot_base(h)) // ALIBI_PIECES
                    aug = jnp.where(term == 0, j_pos,
                                    jnp.where(term == 1, float(n),
                                              jnp.where((term == 2) | (term == 3), 1.0, 0.0)))
                    kaug_ref[h, n] = jnp.where(_own_lanes(h, lane), kb, aug.astype(jnp.bfloat16))
                    vt_ref[h, n] = jnp.concatenate(
                        [v_t[hh * HEAD_DIM:(hh + 1) * HEAD_DIM], ones_rows], axis=0).astype(jnp.bfloat16)
            for hh in range(HEADS_PER_TILE):
                h = t * HEADS_PER_TILE + hh
                rest = jnp.where(_own_lanes(h, km_lane), kmean, 0.0)
                pieces = []
                for _ in range(GATE_PIECES):
                    piece = rest.astype(jnp.bfloat16).astype(jnp.float32)
                    pieces.append(piece)
                    rest = rest - piece
                km_ref[h] = jnp.concatenate(pieces[::-1], axis=0).astype(jnp.bfloat16)

    prepare()

    def select(g, carry):
        _moba_select(g, slopes_ref, q_ref, km_ref, qaug_ref, neg_ref, n_blocks=n_blocks)
        return carry

    lax.fori_loop(0, n_blocks // SELECT_TILES, select, 0)

    _moba_open(0, kaug_ref, qaug_ref, s_ref, smax_ref)

    def sweep(qi, carry):
        _moba_tile(qi, o_ref, kaug_ref, vt_ref, qaug_ref, neg_ref, s_ref, smax_ref, p_ref, acc_ref,
                   n_blocks=n_blocks)
        _moba_open(jnp.minimum(qi + 1, n_blocks - 1), kaug_ref, qaug_ref, s_ref, smax_ref)
        return carry

    lax.fori_loop(0, n_blocks, sweep, 0)


def _moba_select(g, slopes_ref, q_ref, km_ref, qaug_ref, neg_ref, *, n_blocks):
    blk = MOBA_BLOCK
    width = SELECT_TILES * blk
    n_tiles = q_ref.shape[-1] // LANES
    n_heads = n_tiles * HEADS_PER_TILE
    hg = pl.program_id(1)
    first_tile = jnp.asarray(g, jnp.int32) * SELECT_TILES
    q_rows = pl.ds(pl.multiple_of(first_tile * blk, width), width)

    def tile_of(shape):
        return first_tile + lax.broadcasted_iota(jnp.int32, shape, 1) // blk

    slot = lax.broadcasted_iota(jnp.int32, (ALIBI_ROWS, width), 0)
    i_pos = (lax.broadcasted_iota(jnp.int32, (ALIBI_ROWS, width), 1) % blk).astype(jnp.float32)
    tile_f = tile_of((ALIBI_ROWS, width)).astype(jnp.float32)
    pad = jnp.zeros((HEAD_DIM - ALIBI_ROWS, width), jnp.float32)
    blk_id = lax.broadcasted_iota(jnp.int32, (n_blocks, width), 0)
    past = blk_id < tile_of((n_blocks, width))
    for t in range(n_tiles):
        q_t = q_ref[q_rows, t * LANES:(t + 1) * LANES].astype(jnp.float32).T
        for hh in range(HEADS_PER_TILE):
            h = t * HEADS_PER_TILE + hh
            c = slopes_ref[hg * n_heads + h] * LOG2_E
            term = slot // ALIBI_PIECES
            full = jnp.where(term == 0, c,
                             jnp.where(term == 1, c * blk,
                                       jnp.where(term == 2, -(c * (blk * tile_f)),
                                                 jnp.where(term == 3, -(c * i_pos), 0.0))))
            piece = full.astype(jnp.bfloat16).astype(jnp.float32)
            aug = piece
            for r in range(1, ALIBI_PIECES):
                full = full - piece
                piece = full.astype(jnp.bfloat16).astype(jnp.float32)
                aug = jnp.where(slot % ALIBI_PIECES == r, piece, aug)
            own = q_t[hh * HEAD_DIM:(hh + 1) * HEAD_DIM]
            rows = [own, aug, pad] if _slot_base(h) > 0 else [aug, pad, own]
            q_aug = jnp.concatenate(rows, axis=0).astype(jnp.bfloat16)
            for j in range(SELECT_TILES):
                qaug_ref[h, first_tile + j] = q_aug[:, j * blk:(j + 1) * blk]

            g3 = jnp.dot(km_ref[h], q_aug, preferred_element_type=jnp.float32)
            gate = g3[:n_blocks] + g3[n_blocks:2 * n_blocks] + g3[2 * n_blocks:]
            gate = jnp.where(past, gate, NEG_INF)
            blk_f = blk_id.astype(jnp.float32)
            chosen = jnp.zeros((n_blocks, width), jnp.float32)
            for _ in range(MOBA_TOPK):
                best = jnp.max(gate, axis=0, keepdims=True)
                pick = jnp.min(jnp.where(gate == best, blk_f, float(n_blocks)), axis=0, keepdims=True)
                taken = blk_f == pick
                chosen = jnp.where(taken, 1.0, chosen)
                gate = jnp.where(taken, NEG_INF, gate)
            neg = jnp.where((chosen > 0.5) & past, 0.0, NEG_INF)
            for j in range(SELECT_TILES):
                neg_ref[h, first_tile + j, 0] = jnp.zeros((1, blk), jnp.float32)
                for u in range(1, n_blocks):
                    neg_ref[h, first_tile + j, u] = neg[u - 1:u, j * blk:(j + 1) * blk]


def _moba_open(qi, kaug_ref, qaug_ref, s_ref, smax_ref):
    blk = MOBA_BLOCK
    key_i = lax.broadcasted_iota(jnp.int32, (blk, blk), 0)
    qry_i = lax.broadcasted_iota(jnp.int32, (blk, blk), 1)
    causal = key_i <= qry_i
    for h in range(kaug_ref.shape[0]):
        s = jnp.dot(kaug_ref[h, qi], qaug_ref[h, qi], preferred_element_type=jnp.float32)
        s = jnp.where(causal, s, NEG_INF)
        s_ref[0, h] = s
        smax_ref[h] = jnp.max(s, axis=0, keepdims=True)


def _moba_tile(qi, o_ref, kaug_ref, vt_ref, qaug_ref, neg_ref, s_ref, smax_ref, p_ref, acc_ref, *,
               n_blocks):
    blk = MOBA_BLOCK
    n_heads = kaug_ref.shape[0]
    n_tiles = n_heads // HEADS_PER_TILE
    q_rows = pl.ds(pl.multiple_of(qi * blk, blk), blk)
    q_aug = [qaug_ref[h, qi] for h in range(n_heads)]
    for h in range(n_heads):
        p_ref[1, h] = jnp.zeros((blk, blk), jnp.bfloat16)
        acc_ref[h] = jnp.zeros((PV_ROWS, blk), jnp.float32)

    def block_of(u):
        return jnp.where(u == 0, qi, jnp.minimum(u - 1, n_blocks - 1))

    def step(u, cur, carry):
        m_runs, s_maxs = carry[:n_heads], carry[n_heads:]
        b_prev = block_of(jnp.maximum(u - 1, 0))
        k_next = jnp.minimum(u, n_blocks - 1)
        pv = [jnp.dot(vt_ref[h, b_prev], p_ref[1 - cur, h], preferred_element_type=jnp.float32)
              for h in range(n_heads)]
        next_maxs = []
        for h in range(n_heads):
            s_next = jnp.dot(kaug_ref[h, k_next], q_aug[h], preferred_element_type=jnp.float32)
            s_ref[1 - cur, h] = s_next
            next_maxs.append(jnp.max(s_next, axis=0, keepdims=True))
        m_news = []
        for h in range(n_heads):
            neg = neg_ref[h, qi, u]
            m_new = jnp.maximum(m_runs[h], s_maxs[h] + neg)
            alpha = jnp.exp2(m_runs[h] - m_new)
            p = jnp.exp2(s_ref[cur, h] - (m_new - neg))
            p_ref[cur, h] = p.astype(jnp.bfloat16)
            acc_ref[h] = alpha * (acc_ref[h] + pv[h])
            m_news.append(m_new)
        return tuple(m_news) + tuple(next_maxs)

    def step_pair(i, carry):
        return step(2 * i + 1, 1, step(2 * i, 0, carry))

    m_init = jnp.full((1, blk), NEG_INF, jnp.float32)
    lax.fori_loop(0, (qi + 2) // 2, step_pair,
                  (m_init,) * n_heads + tuple(smax_ref[h] for h in range(n_heads)))
    last = 2 * ((qi + 2) // 2) - 1
    for t in range(n_tiles):
        outs = []
        for h in range(t * HEADS_PER_TILE, (t + 1) * HEADS_PER_TILE):
            acc = acc_ref[h] + jnp.dot(vt_ref[h, block_of(last)], p_ref[1, h],
                                       preferred_element_type=jnp.float32)
            outs.append(acc[:HEAD_DIM] / acc[HEAD_DIM:HEAD_DIM + 1])
        o_ref[q_rows, t * LANES:(t + 1) * LANES] = jnp.concatenate(outs, axis=0).T.astype(jnp.bfloat16)


MOBA_TILES_PER_STEP = 2


def _moba(q, k, v, slopes):
    bn, s, width = q.shape
    step_w = MOBA_TILES_PER_STEP * LANES
    assert s % MOBA_BLOCK == 0 and width % step_w == 0
    n_blocks = s // MOBA_BLOCK
    n_heads = MOBA_TILES_PER_STEP * HEADS_PER_TILE
    assert ALIBI_TERMS * ALIBI_PIECES <= ALIBI_ROWS <= HEAD_DIM and HEADS_PER_TILE == 2
    assert n_blocks <= 256 and MOBA_BLOCK <= 256
    seq = pl.BlockSpec((None, s, step_w), lambda b, hg: (b, 0, hg))
    return pl.pallas_call(
        functools.partial(_moba_kernel, n_blocks=n_blocks),
        grid=(bn, width // step_w),
        in_specs=[pl.BlockSpec(memory_space=pltpu.SMEM), seq, seq, seq],
        out_specs=seq,
        out_shape=jax.ShapeDtypeStruct((bn, s, width), jnp.bfloat16),
        scratch_shapes=[
            pltpu.VMEM((n_heads, GATE_PIECES * n_blocks, LANES), jnp.bfloat16),
            pltpu.VMEM((n_heads, n_blocks, MOBA_BLOCK, LANES), jnp.bfloat16),
            pltpu.VMEM((n_heads, n_blocks, PV_ROWS, MOBA_BLOCK), jnp.bfloat16),
            pltpu.VMEM((n_heads, n_blocks, LANES, MOBA_BLOCK), jnp.bfloat16),
            pltpu.VMEM((n_heads, n_blocks, n_blocks, 1, MOBA_BLOCK), jnp.float32),
            pltpu.VMEM((2, n_heads, MOBA_BLOCK, MOBA_BLOCK), jnp.float32),
            pltpu.VMEM((n_heads, 1, MOBA_BLOCK), jnp.float32),
            pltpu.VMEM((2, n_heads, MOBA_BLOCK, MOBA_BLOCK), jnp.bfloat16),
            pltpu.VMEM((n_heads, PV_ROWS, MOBA_BLOCK), jnp.float32),
        ],
        compiler_params=pltpu.CompilerParams(
            dimension_semantics=("arbitrary", "arbitrary"),
            vmem_limit_bytes=VMEM_LIMIT),
        name="moba",
    )(slopes, q, k, v)


def _mix_out_kernel(x_ref, ya_ref, yb_ref, yc_ref, wout_ref, gpost_ref, gprex_ref, wq_ref,
                    kx_ref, vx_ref, wo_ref, gpostx_ref, *rest):
    n_cast = len(rest) // 2
    o_ref = rest[n_cast]
    _cast_riders(rest[:n_cast], rest[n_cast + 1:])

    d = x_ref.shape[-1]
    dh = d // X_HEADS
    nt = (((1,), (1,)), ((), ()))
    rows = x_ref.shape[0] // ROW_GROUPS
    groups = [slice(r * rows, (r + 1) * rows) for r in range(ROW_GROUPS)]

    ys = [jnp.concatenate([ya_ref[g, :], yb_ref[g, :], yc_ref[g, :]], axis=-1) for g in groups]
    mixes = [jnp.dot(y, wout_ref[...], preferred_element_type=jnp.float32) for y in ys]
    x1s = [x_ref[g, :] + _rms(mix, gpost_ref[...]) for g, mix in zip(groups, mixes)]
    hs = [_rms(x1, gprex_ref[...]).astype(jnp.bfloat16) for x1 in x1s]
    qxs = [(jnp.dot(h, wq_ref[...], preferred_element_type=jnp.float32) * (dh ** -0.5)
            ).astype(jnp.bfloat16) for h in hs]
    outs = []
    for qx in qxs:
        heads = []
        for hh in range(X_HEADS):
            cols = slice(hh * dh, (hh + 1) * dh)
            s = lax.dot_general(qx[:, cols], kx_ref[:, cols], nt, preferred_element_type=jnp.float32)
            p = jnp.exp(s - jnp.max(s, axis=-1, keepdims=True))
            denom = jnp.sum(p, axis=-1, keepdims=True)
            oh = jnp.dot(p.astype(jnp.bfloat16), vx_ref[:, cols], preferred_element_type=jnp.float32)
            heads.append((oh / denom).astype(jnp.bfloat16))
        outs.append(jnp.concatenate(heads, axis=-1))
    atts = [jnp.dot(o, wo_ref[...], preferred_element_type=jnp.float32) for o in outs]
    for g, x1, att in zip(groups, x1s, atts):
        o_ref[g, :] = x1 + _rms(att, gpostx_ref[...])


def _mix_out(x, ya, yb, yc, l, w_out, g_post, g_prex, w_xq, kx, vx, w_xo, g_postx, next_weights):
    bn, s, d = x.shape
    tm = min(ROW_GROUPS * TOKEN_TILE, s)
    m = kx.shape[2]
    n_i = s // tm
    lsel3 = lambda b, i: (l, 0, 0)
    first3 = lambda b, i: (0, 0, 0)
    tok = lambda w: pl.BlockSpec((None, tm, w), lambda b, i: (b, i, 0))
    gain = pl.BlockSpec((None, 1, d), lsel3)
    wsq = _const_spec((None, d, d), first3)
    mem = pl.BlockSpec((None, None, m, d), lambda b, i: (l, b, 0, 0))
    cast_in, cast_out, cast_shape = _cast_rider_specs(
        next_weights, l + 1, bn * n_i, lambda b, i: b * n_i + i)
    out = pl.pallas_call(
        _mix_out_kernel,
        grid=(bn, n_i),
        in_specs=[tok(d), tok(ya.shape[-1]), tok(yb.shape[-1]), tok(yc.shape[-1]),
                  wsq, gain, gain, wsq, mem, mem, wsq, gain] + cast_in,
        out_specs=[tok(d)] + cast_out,
        out_shape=[jax.ShapeDtypeStruct((bn, s, d), jnp.float32)] + cast_shape,
        compiler_params=pltpu.CompilerParams(
            dimension_semantics=("arbitrary", "arbitrary"), vmem_limit_bytes=VMEM_LIMIT),
        name="mix_out",
    )(x, ya, yb, yc, w_out, g_post, g_prex, w_xq, kx, vx, w_xo, g_postx, *next_weights)
    return out[0], tuple(out[1:])


def _ffn_kernel(x_ref, gpre_ref, w1_ref, w2_ref, gpost_ref, o_ref):
    rows = x_ref.shape[0] // ROW_GROUPS
    for r in range(ROW_GROUPS):
        g = slice(r * rows, (r + 1) * rows)
        x = x_ref[g, :]
        h = _rms(x, gpre_ref[...]).astype(jnp.bfloat16)
        a = jnp.maximum(jnp.dot(h, w1_ref[...], preferred_element_type=jnp.float32), 0.0)
        f = jnp.dot((a * a).astype(jnp.bfloat16), w2_ref[...], preferred_element_type=jnp.float32)
        o_ref[g, :] = x + _rms(f, gpost_ref[...])


def _ffn(x, l, g_pre, w1, w2, g_post):
    bn, s, d = x.shape
    tm = min(ROW_GROUPS * TOKEN_TILE, s)
    dff = w1.shape[-1]
    lsel3 = lambda b, i: (l, 0, 0)
    first3 = lambda b, i: (0, 0, 0)
    tok = pl.BlockSpec((None, tm, d), lambda b, i: (b, i, 0))
    gain = pl.BlockSpec((None, 1, d), lsel3)
    return pl.pallas_call(
        _ffn_kernel,
        grid=(bn, s // tm),
        in_specs=[tok, gain, _const_spec((None, d, dff), first3),
                  _const_spec((None, dff, d), first3), gain],
        out_specs=tok,
        out_shape=jax.ShapeDtypeStruct((bn, s, d), jnp.float32),
        compiler_params=pltpu.CompilerParams(
            dimension_semantics=("arbitrary", "arbitrary"), vmem_limit_bytes=VMEM_LIMIT),
        name="ffn",
    )(x, g_pre, w1, w2, g_post)


def kernel(x, mem, pre_mix_g, w_in, gate_ln_g, gate_ln_b, w_s, b_s, w_dw, b_dw, conv_gn_g, conv_gn_b, w_out, post_mix_g, pre_x_g, mem_g, w_xq, w_xkv, w_xo, post_x_g, pre_ffn_g, w_ff1, w_ff2, post_ffn_g):
    depth, d, _ = w_in.shape
    a_width = gate_ln_g.shape[-1]
    c_width = w_dw.shape[-1]
    b_width = d - a_width - c_width
    a_heads = a_width // HEAD_DIM
    b_heads = b_width // HEAD_DIM
    assert w_in.shape[-1] == 2 * a_width + 3 * b_width + 2 * c_width
    assert w_s.shape[1:] == (a_heads, CHUNK, CHUNK) and w_dw.shape[1] == CONV_WIDTH
    assert x.shape[1] % MOBA_BLOCK == 0 and x.shape[1] % TOKEN_TILE == 0

    row = lambda g: g[:, None, :]
    ws_cat = w_s.transpose(0, 2, 1, 3).reshape(depth, CHUNK, a_heads * CHUNK)
    bs_x = jnp.repeat(b_s.transpose(0, 2, 1), HEAD_DIM, axis=-1)
    slopes = 2.0 ** (-8.0 * jnp.arange(1, b_heads + 1, dtype=jnp.float32) / b_heads)
    later_weights = (w_out, w_xq, w_xo, w_ff1, w_ff2)
    layer_weights = (w_in,) + later_weights
    kx, vx, (w_in_l,) = _mem_kv(mem, row(mem_g), w_xkv, (w_in,))
    for l in range(depth):
        (ya, q, k, v, yc), own_l = _mix_in(
            x, l, row(pre_mix_g), w_in_l, row(gate_ln_g), row(gate_ln_b), ws_cat, bs_x, w_dw,
            row(b_dw), row(conv_gn_g), row(conv_gn_b), later_weights if l == 0 else (),
            a_width=a_width, b_width=b_width, c_width=c_width)
        if own_l:
            w_out_l, w_xq_l, w_xo_l, w_ff1_l, w_ff2_l = own_l
        yb = _moba(q, k, v, slopes)
        x, next_l = _mix_out(x, ya, yb, yc, l, w_out_l, row(post_mix_g), row(pre_x_g), w_xq_l, kx, vx,
                             w_xo_l, row(post_x_g), layer_weights if l + 1 < depth else ())
        x = _ffn(x, l, row(pre_ffn_g), w_ff1_l, w_ff2_l, row(post_ffn_g))
        if next_l:
            w_in_l, w_out_l, w_xq_l, w_xo_l, w_ff1_l, w_ff2_l = next_l
    return x
```

```python
import functools
import math

import jax
import jax.numpy as jnp
from jax import lax
from jax.experimental import pallas as pl
from jax.experimental.pallas import tpu as pltpu

HEAD_DIM = 64
CHUNK = 128
MOBA_BLOCK = 256
MOBA_TOPK = 3
CONV_WIDTH = 31
X_HEADS = 4
EPS = 1e-6
LOG2_E = math.log2(math.e)

LANES = 128
SUBLANES = 8
HALO = 32
CONV_ROWS = 128
TOKEN_TILE = 512
ROW_GROUPS = 2
MIX_IN_TILE = 1024
BF16_SUBLANES = 16
VMEM_LIMIT = 56 * 1024 * 1024

NEG_INF = float("-inf")


def _rms(x, g):
    return x * lax.rsqrt(jnp.mean(x * x, axis=-1, keepdims=True) + EPS) * g


def _const_spec(shape, index_map):
    return pl.BlockSpec(shape, index_map, pipeline_mode=pl.Buffered(1))


def _cast_rider_specs(weights, layer, n_steps, step_of):
    in_specs, out_specs, out_shapes = [], [], []
    for w in weights:
        _, rows, cols = w.shape
        assert rows % (n_steps * BF16_SUBLANES) == 0
        chunk = (None, rows // n_steps, cols)
        in_specs.append(pl.BlockSpec(chunk, lambda *g: (layer, step_of(*g), 0)))
        out_specs.append(pl.BlockSpec(chunk, lambda *g: (0, step_of(*g), 0)))
        out_shapes.append(jax.ShapeDtypeStruct((1, rows, cols), jnp.bfloat16))
    return in_specs, out_specs, out_shapes


def _cast_riders(src_refs, dst_refs):
    for src_ref, dst_ref in zip(src_refs, dst_refs):
        dst_ref[...] = src_ref[...].astype(jnp.bfloat16)


def _mem_kv_kernel(mem_ref, g_ref, w_ref, *rest):
    n_cast = (len(rest) - 2) // 2
    k_ref, v_ref = rest[n_cast:n_cast + 2]
    _cast_riders(rest[:n_cast], rest[n_cast + 2:])
    d = mem_ref.shape[-1]
    m = _rms(mem_ref[...], g_ref[...]).astype(jnp.bfloat16)
    kv = jnp.dot(m, w_ref[...].astype(jnp.bfloat16), preferred_element_type=jnp.float32)
    k_ref[...] = kv[:, :d].astype(jnp.bfloat16)
    v_ref[...] = kv[:, d:].astype(jnp.bfloat16)


def _mem_kv(mem, mem_g, w_xkv, layer_weights):
    bn, m, d = mem.shape
    depth = w_xkv.shape[0]
    out = jax.ShapeDtypeStruct((depth, bn, m, d), jnp.bfloat16)
    cast_in, cast_out, cast_shape = _cast_rider_specs(
        layer_weights, 0, depth * bn, lambda l, b: l * bn + b)
    res = pl.pallas_call(
        _mem_kv_kernel,
        grid=(depth, bn),
        in_specs=[
            pl.BlockSpec((None, m, d), lambda l, b: (b, 0, 0)),
            pl.BlockSpec((None, 1, d), lambda l, b: (l, 0, 0)),
            pl.BlockSpec((None, d, 2 * d), lambda l, b: (l, 0, 0)),
        ] + cast_in,
        out_specs=[
            pl.BlockSpec((None, None, m, d), lambda l, b: (l, b, 0, 0)),
            pl.BlockSpec((None, None, m, d), lambda l, b: (l, b, 0, 0)),
        ] + cast_out,
        out_shape=[out, out] + cast_shape,
        compiler_params=pltpu.CompilerParams(
            dimension_semantics=("arbitrary", "arbitrary"), vmem_limit_bytes=VMEM_LIMIT),
        name="mem_kv",
    )(mem, mem_g, w_xkv, *layer_weights)
    return res[0], res[1], tuple(res[2:])


def _mix_in_kernel(x_ref, g_ref, win_ref, lng_ref, lnb_ref, ws_ref, bs_ref, wdw_ref, bdw_ref,
                   gng_ref, gnb_ref, *rest, a_width, b_width, c_width):
    n_cast = (len(rest) - 8) // 2
    ya_ref, q_ref, k_ref, v_ref, yc_ref = rest[n_cast:n_cast + 5]
    hist_ref, conv_ref, shift_ref = rest[-3:]
    _cast_riders(rest[:n_cast], rest[n_cast + 5:-3])
    tm = x_ref.shape[0]

    @pl.when(pl.program_id(1) == 0)
    def _():
        hist_ref[0:HALO, :] = jnp.zeros((HALO, c_width), jnp.float32)

    h = _rms(x_ref[...], g_ref[...]).astype(jnp.bfloat16)

    def project(lo, hi):
        return jnp.dot(h, win_ref[:, lo:hi], preferred_element_type=jnp.float32)

    o = 2 * a_width + 3 * b_width
    zc = project(o, o + 2 * c_width)
    hist_ref[HALO:HALO + tm, :] = zc[:, :c_width] * (1.0 / (1.0 + jnp.exp(-zc[:, c_width:])))

    o = 2 * a_width
    q = (project(o, o + b_width) * (HEAD_DIM ** -0.5 * LOG2_E)).astype(jnp.bfloat16)
    k = project(o + b_width, o + 2 * b_width).astype(jnp.bfloat16)
    v = project(o + 2 * b_width, o + 3 * b_width).astype(jnp.bfloat16)
    za = project(0, 2 * a_width)

    first = HALO - (CONV_WIDTH - 1)
    for c in range(tm // CONV_ROWS):
        r0 = c * CONV_ROWS
        acc = jnp.zeros((CONV_ROWS, c_width), jnp.float32)
        for phase in range(SUBLANES):
            taps = [j for j in range(CONV_WIDTH) if (first + j) % SUBLANES == phase]
            span = first + taps[-1] - phase
            rows = CONV_ROWS + span
            if phase:
                shift_ref[phase, 0:rows, :] = hist_ref[r0 + phase:r0 + phase + rows, :]
            for j in taps:
                off = first + j - phase
                window = (shift_ref[phase, off:off + CONV_ROWS, :] if phase else
                          hist_ref[r0 + off:r0 + off + CONV_ROWS, :])
                acc = acc + wdw_ref[j:j + 1, :] * window
        conv_ref[r0:r0 + CONV_ROWS, :] = acc + bdw_ref[...]
    hist_ref[0:HALO, :] = hist_ref[tm:tm + HALO, :]
    q_ref[...] = q
    k_ref[...] = k
    v_ref[...] = v

    ga = 0.5 * za * (1.0 + lax.erf(za * (2.0 ** -0.5)))
    u = ga[:, :a_width]
    vv = ga[:, a_width:]
    mu = jnp.mean(vv, axis=-1, keepdims=True)
    vc = vv - mu
    var = jnp.mean(vc * vc, axis=-1, keepdims=True)
    vn = (vc * lax.rsqrt(var + EPS) * lng_ref[...] + lnb_ref[...]).astype(jnp.bfloat16)
    n_heads = a_width // HEAD_DIM
    t_idx = lax.broadcasted_iota(jnp.int32, (CHUNK, n_heads * CHUNK), 0)
    s_idx = lax.broadcasted_iota(jnp.int32, (CHUNK, n_heads * CHUNK), 1) % CHUNK
    w_tril = jnp.where(s_idx <= t_idx, ws_ref[...], 0.0).astype(jnp.bfloat16)
    r_head = lax.broadcasted_iota(jnp.int32, (n_heads * CHUNK, a_width), 0) // CHUNK
    c_head = lax.broadcasted_iota(jnp.int32, (n_heads * CHUNK, a_width), 1) // HEAD_DIM
    head_match = r_head == c_head
    for c in range(tm // CHUNK):
        rows = slice(c * CHUNK, (c + 1) * CHUNK)
        v_chunk = vn[rows]
        v_bd = jnp.where(head_match, jnp.concatenate([v_chunk] * n_heads, axis=0), 0.0)
        mixed = jnp.dot(w_tril, v_bd.astype(jnp.bfloat16), preferred_element_type=jnp.float32)
        ya_ref[rows, :] = (u[rows] * (mixed + bs_ref[...])).astype(jnp.bfloat16)

    gr = lax.broadcasted_iota(jnp.int32, (2 * c_width, c_width), 0) % c_width // HEAD_DIM
    gc = lax.broadcasted_iota(jnp.int32, (2 * c_width, c_width), 1) // HEAD_DIM
    gmat2 = jnp.where(gr == gc, 1.0 / HEAD_DIM, 0.0).astype(jnp.bfloat16)

    def group_mean(a):
        hi = a.astype(jnp.bfloat16)
        lo = (a - hi.astype(jnp.float32)).astype(jnp.bfloat16)
        return jnp.dot(jnp.concatenate([hi, lo], axis=-1), gmat2, preferred_element_type=jnp.float32)

    y = conv_ref[...]
    yd = y - group_mean(y)
    yn = yd * lax.rsqrt(group_mean(yd * yd) + EPS) * gng_ref[...] + gnb_ref[...]
    yc_ref[...] = (yn * (1.0 / (1.0 + jnp.exp(-yn)))).astype(jnp.bfloat16)


def _mix_in(x, l, pre_g, w_in, ln_g, ln_b, ws_cat, bs_x, w_dw, b_dw, gn_g, gn_b, own_weights,
            *, a_width, b_width, c_width):
    bn, s, d = x.shape
    tm = min(MIX_IN_TILE, s)
    n_i = s // tm
    in_cols = w_in.shape[-1]
    n_heads = a_width // HEAD_DIM
    lsel3 = lambda b, i: (l, 0, 0)
    tok = lambda w: pl.BlockSpec((None, tm, w), lambda b, i: (b, i, 0))
    bf = lambda w: jax.ShapeDtypeStruct((bn, s, w), jnp.bfloat16)
    cast_in, cast_out, cast_shape = _cast_rider_specs(
        own_weights, l, bn * n_i, lambda b, i: b * n_i + i)
    res = pl.pallas_call(
        functools.partial(_mix_in_kernel, a_width=a_width, b_width=b_width, c_width=c_width),
        grid=(bn, s // tm),
        in_specs=[
            tok(d),
            pl.BlockSpec((None, 1, d), lsel3),
            _const_spec((None, d, in_cols), lambda b, i: (0, 0, 0)),
            pl.BlockSpec((None, 1, a_width), lsel3),
            pl.BlockSpec((None, 1, a_width), lsel3),
            pl.BlockSpec((None, CHUNK, n_heads * CHUNK), lsel3),
            pl.BlockSpec((None, CHUNK, a_width), lsel3),
            pl.BlockSpec((None, CONV_WIDTH, c_width), lsel3),
            pl.BlockSpec((None, 1, c_width), lsel3),
            pl.BlockSpec((None, 1, c_width), lsel3),
            pl.BlockSpec((None, 1, c_width), lsel3),
        ] + cast_in,
        out_specs=[tok(a_width), tok(b_width), tok(b_width), tok(b_width), tok(c_width)] + cast_out,
        out_shape=[bf(a_width), bf(b_width), bf(b_width), bf(b_width), bf(c_width)] + cast_shape,
        scratch_shapes=[pltpu.VMEM((tm + HALO, c_width), jnp.float32),
                        pltpu.VMEM((tm, c_width), jnp.float32),
                        pltpu.VMEM((SUBLANES, CONV_ROWS + HALO, c_width), jnp.float32)],
        compiler_params=pltpu.CompilerParams(
            dimension_semantics=("arbitrary", "arbitrary"), vmem_limit_bytes=VMEM_LIMIT),
        name="mix_in",
    )(x, pre_g, w_in, ln_g, ln_b, ws_cat, bs_x, w_dw, b_dw, gn_g, gn_b, *own_weights)
    return tuple(res[:5]), tuple(res[5:])


ALIBI_TERMS = 4
ALIBI_PIECES = 3
ALIBI_ROWS = 16
HEADS_PER_TILE = LANES // HEAD_DIM
PV_ROWS = HEAD_DIM + BF16_SUBLANES
GATE_PIECES = 3
SELECT_TILES = 1


def _own_lanes(h, idx):
    hh = h % HEADS_PER_TILE
    return (idx >= hh * HEAD_DIM) & (idx < (hh + 1) * HEAD_DIM)


def _slot_base(h):
    return ((h + 1) % HEADS_PER_TILE) * HEAD_DIM


def _moba_kernel(slopes_ref, q_ref, k_ref, v_ref, o_ref, km_ref, kaug_ref, vt_ref, qaug_ref, neg_ref,
                 s_ref, p_ref, acc_ref, *, n_blocks):
    blk = MOBA_BLOCK
    n_tiles = q_ref.shape[-1] // LANES

    def prepare():
        lane = lax.broadcasted_iota(jnp.int32, (blk, LANES), 1)
        j_pos = lax.broadcasted_iota(jnp.int32, (blk, LANES), 0).astype(jnp.float32)
        ones_rows = (lax.broadcasted_iota(jnp.int32, (BF16_SUBLANES, blk), 0) == 0
                     ).astype(jnp.float32)
        km_row = lax.broadcasted_iota(jnp.int32, (n_blocks, LANES), 0)
        km_lane = lax.broadcasted_iota(jnp.int32, (n_blocks, LANES), 1)
        sum_rows = jnp.ones((BF16_SUBLANES, blk), jnp.bfloat16)
        for t in range(n_tiles):
            cols = slice(t * LANES, (t + 1) * LANES)
            kmean = jnp.zeros((n_blocks, LANES), jnp.float32)
            for n in range(n_blocks):
                rows = slice(n * blk, (n + 1) * blk)
                kb = k_ref[rows, cols]
                k_sum = jnp.dot(sum_rows, kb, preferred_element_type=jnp.float32)[:1]
                kmean = jnp.where(km_row == n, k_sum * (1.0 / blk), kmean)
                v_t = v_ref[rows, cols].astype(jnp.float32).T
                for hh in range(HEADS_PER_TILE):
                    h = t * HEADS_PER_TILE + hh
                    term = (lane - _slot_base(h)) // ALIBI_PIECES
                    aug = jnp.where(term == 0, j_pos,
                                    jnp.where(term == 1, float(n),
                                              jnp.where((term == 2) | (term == 3), 1.0, 0.0)))
                    kaug_ref[h, n] = jnp.where(_own_lanes(h, lane), kb, aug.astype(jnp.bfloat16))
                    vt_ref[h, n] = jnp.concatenate(
                        [v_t[hh * HEAD_DIM:(hh + 1) * HEAD_DIM], ones_rows], axis=0).astype(jnp.bfloat16)
            for hh in range(HEADS_PER_TILE):
                h = t * HEADS_PER_TILE + hh
                rest = jnp.where(_own_lanes(h, km_lane), kmean, 0.0)
                pieces = []
                for _ in range(GATE_PIECES):
                    piece = rest.astype(jnp.bfloat16).astype(jnp.float32)
                    pieces.append(piece)
                    rest = rest - piece
                km_ref[h] = jnp.concatenate(pieces[::-1], axis=0).astype(jnp.bfloat16)

    prepare()

    def select(g):
        _moba_select(g, slopes_ref, q_ref, km_ref, qaug_ref, neg_ref, n_blocks=n_blocks)

    assert SELECT_TILES == 1
    select(0)
    select(1)
    _moba_open(0, kaug_ref, qaug_ref, s_ref)

    def sweep(qi, carry):
        _moba_tile(qi, o_ref, kaug_ref, vt_ref, qaug_ref, neg_ref, s_ref, p_ref, acc_ref,
                   n_blocks=n_blocks)
        _moba_open(jnp.minimum(qi + 1, n_blocks - 1), kaug_ref, qaug_ref, s_ref)
        select(jnp.minimum(qi + 2, n_blocks - 1))
        return carry

    lax.fori_loop(0, n_blocks, sweep, 0)


def _moba_select(g, slopes_ref, q_ref, km_ref, qaug_ref, neg_ref, *, n_blocks):
    blk = MOBA_BLOCK
    width = SELECT_TILES * blk
    n_tiles = q_ref.shape[-1] // LANES
    n_heads = n_tiles * HEADS_PER_TILE
    hg = pl.program_id(1)
    first_tile = jnp.asarray(g, jnp.int32) * SELECT_TILES
    q_rows = pl.ds(pl.multiple_of(first_tile * blk, width), width)

    def tile_of(shape):
        return first_tile + lax.broadcasted_iota(jnp.int32, shape, 1) // blk

    slot = lax.broadcasted_iota(jnp.int32, (ALIBI_ROWS, width), 0)
    i_pos = (lax.broadcasted_iota(jnp.int32, (ALIBI_ROWS, width), 1) % blk).astype(jnp.float32)
    tile_f = tile_of((ALIBI_ROWS, width)).astype(jnp.float32)
    pad = jnp.zeros((HEAD_DIM - ALIBI_ROWS, width), jnp.float32)
    blk_id = lax.broadcasted_iota(jnp.int32, (n_blocks, width), 0)
    past = blk_id < tile_of((n_blocks, width))
    for t in range(n_tiles):
        q_t = q_ref[q_rows, t * LANES:(t + 1) * LANES].astype(jnp.float32).T
        for hh in range(HEADS_PER_TILE):
            h = t * HEADS_PER_TILE + hh
            c = slopes_ref[hg * n_heads + h] * LOG2_E
            term = slot // ALIBI_PIECES
            full = jnp.where(term == 0, c,
                             jnp.where(term == 1, c * blk,
                                       jnp.where(term == 2, -(c * (blk * tile_f)),
                                                 jnp.where(term == 3, -(c * i_pos), 0.0))))
            piece = full.astype(jnp.bfloat16).astype(jnp.float32)
            aug = piece
            for r in range(1, ALIBI_PIECES):
                full = full - piece
                piece = full.astype(jnp.bfloat16).astype(jnp.float32)
                aug = jnp.where(slot % ALIBI_PIECES == r, piece, aug)
            own = q_t[hh * HEAD_DIM:(hh + 1) * HEAD_DIM]
            rows = [own, aug, pad] if _slot_base(h) > 0 else [aug, pad, own]
            q_aug = jnp.concatenate(rows, axis=0).astype(jnp.bfloat16)
            for j in range(SELECT_TILES):
                qaug_ref[h, first_tile + j] = q_aug[:, j * blk:(j + 1) * blk]

            g3 = jnp.dot(km_ref[h], q_aug, preferred_element_type=jnp.float32)
            gate = g3[:n_blocks] + g3[n_blocks:2 * n_blocks] + g3[2 * n_blocks:]
            gate = jnp.where(past, gate, NEG_INF)
            blk_f = blk_id.astype(jnp.float32)
            chosen = jnp.zeros((n_blocks, width), jnp.float32)
            for _ in range(MOBA_TOPK):
                best = jnp.max(gate, axis=0, keepdims=True)
                pick = jnp.min(jnp.where(gate == best, blk_f, float(n_blocks)), axis=0, keepdims=True)
                taken = blk_f == pick
                chosen = jnp.where(taken, 1.0, chosen)
                gate = jnp.where(taken, NEG_INF, gate)
            neg = jnp.where((chosen > 0.5) & past, 0.0, NEG_INF)
            for j in range(SELECT_TILES):
                neg_ref[h, first_tile + j, 0] = jnp.zeros((1, blk), jnp.float32)
                for u in range(1, n_blocks):
                    neg_ref[h, first_tile + j, u] = neg[u - 1:u, j * blk:(j + 1) * blk]


def _moba_open(qi, kaug_ref, qaug_ref, s_ref):
    blk = MOBA_BLOCK
    key_i = lax.broadcasted_iota(jnp.int32, (blk, blk), 0)
    qry_i = lax.broadcasted_iota(jnp.int32, (blk, blk), 1)
    causal = key_i <= qry_i
    for h in range(kaug_ref.shape[0]):
        s = jnp.dot(kaug_ref[h, qi], qaug_ref[h, qi], preferred_element_type=jnp.float32)
        s_ref[0, h] = jnp.where(causal, s, NEG_INF)


def _moba_tile(qi, o_ref, kaug_ref, vt_ref, qaug_ref, neg_ref, s_ref, p_ref, acc_ref, *, n_blocks):
    blk = MOBA_BLOCK
    n_heads = kaug_ref.shape[0]
    n_tiles = n_heads // HEADS_PER_TILE
    q_rows = pl.ds(pl.multiple_of(qi * blk, blk), blk)
    q_aug = [qaug_ref[h, qi] for h in range(n_heads)]
    for h in range(n_heads):
        p_ref[1, h] = jnp.zeros((blk, blk), jnp.bfloat16)
        acc_ref[h] = jnp.zeros((PV_ROWS, blk), jnp.float32)

    def block_of(u):
        return jnp.where(u == 0, qi, jnp.minimum(u - 1, n_blocks - 1))

    def step(u, cur, m_runs):
        b_prev = block_of(jnp.maximum(u - 1, 0))
        k_next = jnp.minimum(u, n_blocks - 1)
        pv = [jnp.dot(vt_ref[h, b_prev], p_ref[1 - cur, h], preferred_element_type=jnp.float32)
              for h in range(n_heads)]
        for h in range(n_heads):
            s_ref[1 - cur, h] = jnp.dot(kaug_ref[h, k_next], q_aug[h],
                                        preferred_element_type=jnp.float32)
        m_news = []
        for h in range(n_heads):
            s = s_ref[cur, h]
            neg = neg_ref[h, qi, u]
            m_new = jnp.maximum(m_runs[h], jnp.max(s, axis=0, keepdims=True) + neg)
            alpha = jnp.exp2(m_runs[h] - m_new)
            p = jnp.exp2(s - (m_new - neg))
            p_ref[cur, h] = p.astype(jnp.bfloat16)
            acc_ref[h] = alpha * (acc_ref[h] + pv[h])
            m_news.append(m_new)
        return tuple(m_news)

    def step_pair(i, m_runs):
        return step(2 * i + 1, 1, step(2 * i, 0, m_runs))

    m_init = jnp.full((1, blk), NEG_INF, jnp.float32)
    lax.fori_loop(0, (qi + 2) // 2, step_pair, (m_init,) * n_heads)
    last = 2 * ((qi + 2) // 2) - 1
    for t in range(n_tiles):
        outs = []
        for h in range(t * HEADS_PER_TILE, (t + 1) * HEADS_PER_TILE):
            acc = acc_ref[h] + jnp.dot(vt_ref[h, block_of(last)], p_ref[1, h],
                                       preferred_element_type=jnp.float32)
            outs.append(acc[:HEAD_DIM] / acc[HEAD_DIM:HEAD_DIM + 1])
        o_ref[q_rows, t * LANES:(t + 1) * LANES] = jnp.concatenate(outs, axis=0).T.astype(jnp.bfloat16)


MOBA_TILES_PER_STEP = 2


def _moba(q, k, v, slopes):
    bn, s, width = q.shape
    step_w = MOBA_TILES_PER_STEP * LANES
    assert s % MOBA_BLOCK == 0 and width % step_w == 0
    n_blocks = s // MOBA_BLOCK
    n_heads = MOBA_TILES_PER_STEP * HEADS_PER_TILE
    assert ALIBI_TERMS * ALIBI_PIECES <= ALIBI_ROWS <= HEAD_DIM and HEADS_PER_TILE == 2
    assert n_blocks <= 256 and MOBA_BLOCK <= 256
    seq = pl.BlockSpec((None, s, step_w), lambda b, hg: (b, 0, hg))
    return pl.pallas_call(
        functools.partial(_moba_kernel, n_blocks=n_blocks),
        grid=(bn, width // step_w),
        in_specs=[pl.BlockSpec(memory_space=pltpu.SMEM), seq, seq, seq],
        out_specs=seq,
        out_shape=jax.ShapeDtypeStruct((bn, s, width), jnp.bfloat16),
        scratch_shapes=[
            pltpu.VMEM((n_heads, GATE_PIECES * n_blocks, LANES), jnp.bfloat16),
            pltpu.VMEM((n_heads, n_blocks, MOBA_BLOCK, LANES), jnp.bfloat16),
            pltpu.VMEM((n_heads, n_blocks, PV_ROWS, MOBA_BLOCK), jnp.bfloat16),
            pltpu.VMEM((n_heads, n_blocks, LANES, MOBA_BLOCK), jnp.bfloat16),
            pltpu.VMEM((n_heads, n_blocks, n_blocks, 1, MOBA_BLOCK), jnp.float32),
            pltpu.VMEM((2, n_heads, MOBA_BLOCK, MOBA_BLOCK), jnp.float32),
            pltpu.VMEM((2, n_heads, MOBA_BLOCK, MOBA_BLOCK), jnp.bfloat16),
            pltpu.VMEM((n_heads, PV_ROWS, MOBA_BLOCK), jnp.float32),
        ],
        compiler_params=pltpu.CompilerParams(
            dimension_semantics=("arbitrary", "arbitrary"),
            vmem_limit_bytes=VMEM_LIMIT),
        name="moba",
    )(slopes, q, k, v)


def _mix_out_kernel(x_ref, ya_ref, yb_ref, yc_ref, wout_ref, gpost_ref, gprex_ref, wq_ref,
                    kx_ref, vx_ref, wo_ref, gpostx_ref, *rest):
    n_cast = len(rest) // 2
    o_ref = rest[n_cast]
    _cast_riders(rest[:n_cast], rest[n_cast + 1:])

    d = x_ref.shape[-1]
    dh = d // X_HEADS
    nt = (((1,), (1,)), ((), ()))
    rows = x_ref.shape[0] // ROW_GROUPS
    groups = [slice(r * rows, (r + 1) * rows) for r in range(ROW_GROUPS)]

    ys = [jnp.concatenate([ya_ref[g, :], yb_ref[g, :], yc_ref[g, :]], axis=-1) for g in groups]
    mixes = [jnp.dot(y, wout_ref[...], preferred_element_type=jnp.float32) for y in ys]
    x1s = [x_ref[g, :] + _rms(mix, gpost_ref[...]) for g, mix in zip(groups, mixes)]
    hs = [_rms(x1, gprex_ref[...]).astype(jnp.bfloat16) for x1 in x1s]
    qxs = [(jnp.dot(h, wq_ref[...], preferred_element_type=jnp.float32) * (dh ** -0.5)
            ).astype(jnp.bfloat16) for h in hs]
    outs = []
    for qx in qxs:
        heads = []
        for hh in range(X_HEADS):
            cols = slice(hh * dh, (hh + 1) * dh)
            s = lax.dot_general(qx[:, cols], kx_ref[:, cols], nt, preferred_element_type=jnp.float32)
            p = jnp.exp(s - jnp.max(s, axis=-1, keepdims=True))
            denom = jnp.sum(p, axis=-1, keepdims=True)
            oh = jnp.dot(p.astype(jnp.bfloat16), vx_ref[:, cols], preferred_element_type=jnp.float32)
            heads.append((oh / denom).astype(jnp.bfloat16))
        outs.append(jnp.concatenate(heads, axis=-1))
    atts = [jnp.dot(o, wo_ref[...], preferred_element_type=jnp.float32) for o in outs]
    for g, x1, att in zip(groups, x1s, atts):
        o_ref[g, :] = x1 + _rms(att, gpostx_ref[...])


def _mix_out(x, ya, yb, yc, l, w_out, g_post, g_prex, w_xq, kx, vx, w_xo, g_postx, next_weights):
    bn, s, d = x.shape
    tm = min(ROW_GROUPS * TOKEN_TILE, s)
    m = kx.shape[2]
    n_i = s // tm
    lsel3 = lambda b, i: (l, 0, 0)
    first3 = lambda b, i: (0, 0, 0)
    tok = lambda w: pl.BlockSpec((None, tm, w), lambda b, i: (b, i, 0))
    gain = pl.BlockSpec((None, 1, d), lsel3)
    wsq = _const_spec((None, d, d), first3)
    mem = pl.BlockSpec((None, None, m, d), lambda b, i: (l, b, 0, 0))
    cast_in, cast_out, cast_shape = _cast_rider_specs(
        next_weights, l + 1, bn * n_i, lambda b, i: b * n_i + i)
    out = pl.pallas_call(
        _mix_out_kernel,
        grid=(bn, n_i),
        in_specs=[tok(d), tok(ya.shape[-1]), tok(yb.shape[-1]), tok(yc.shape[-1]),
                  wsq, gain, gain, wsq, mem, mem, wsq, gain] + cast_in,
        out_specs=[tok(d)] + cast_out,
        out_shape=[jax.ShapeDtypeStruct((bn, s, d), jnp.float32)] + cast_shape,
        compiler_params=pltpu.CompilerParams(
            dimension_semantics=("arbitrary", "arbitrary"), vmem_limit_bytes=VMEM_LIMIT),
        name="mix_out",
    )(x, ya, yb, yc, w_out, g_post, g_prex, w_xq, kx, vx, w_xo, g_postx, *next_weights)
    return out[0], tuple(out[1:])


def _ffn_kernel(x_ref, gpre_ref, w1_ref, w2_ref, gpost_ref, o_ref):
    rows = x_ref.shape[0] // ROW_GROUPS
    for r in range(ROW_GROUPS):
        g = slice(r * rows, (r + 1) * rows)
        x = x_ref[g, :]
        h = _rms(x, gpre_ref[...]).astype(jnp.bfloat16)
        a = jnp.maximum(jnp.dot(h, w1_ref[...], preferred_element_type=jnp.float32), 0.0)
        f = jnp.dot((a * a).astype(jnp.bfloat16), w2_ref[...], preferred_element_type=jnp.float32)
        o_ref[g, :] = x + _rms(f, gpost_ref[...])


def _ffn(x, l, g_pre, w1, w2, g_post):
    bn, s, d = x.shape
    tm = min(ROW_GROUPS * TOKEN_TILE, s)
    dff = w1.shape[-1]
    lsel3 = lambda b, i: (l, 0, 0)
    first3 = lambda b, i: (0, 0, 0)
    tok = pl.BlockSpec((None, tm, d), lambda b, i: (b, i, 0))
    gain = pl.BlockSpec((None, 1, d), lsel3)
    return pl.pallas_call(
        _ffn_kernel,
        grid=(bn, s // tm),
        in_specs=[tok, gain, _const_spec((None, d, dff), first3),
                  _const_spec((None, dff, d), first3), gain],
        out_specs=tok,
        out_shape=jax.ShapeDtypeStruct((bn, s, d), jnp.float32),
        compiler_params=pltpu.CompilerParams(
            dimension_semantics=("arbitrary", "arbitrary"), vmem_limit_bytes=VMEM_LIMIT),
        name="ffn",
    )(x, g_pre, w1, w2, g_post)


def kernel(x, mem, pre_mix_g, w_in, gate_ln_g, gate_ln_b, w_s, b_s, w_dw, b_dw, conv_gn_g, conv_gn_b, w_out, post_mix_g, pre_x_g, mem_g, w_xq, w_xkv, w_xo, post_x_g, pre_ffn_g, w_ff1, w_ff2, post_ffn_g):
    depth, d, _ = w_in.shape
    a_width = gate_ln_g.shape[-1]
    c_width = w_dw.shape[-1]
    b_width = d - a_width - c_width
    a_heads = a_width // HEAD_DIM
    b_heads = b_width // HEAD_DIM
    assert w_in.shape[-1] == 2 * a_width + 3 * b_width + 2 * c_width
    assert w_s.shape[1:] == (a_heads, CHUNK, CHUNK) and w_dw.shape[1] == CONV_WIDTH
    assert x.shape[1] % MOBA_BLOCK == 0 and x.shape[1] % TOKEN_TILE == 0

    row = lambda g: g[:, None, :]
    ws_cat = w_s.transpose(0, 2, 1, 3).reshape(depth, CHUNK, a_heads * CHUNK)
    bs_x = jnp.repeat(b_s.transpose(0, 2, 1), HEAD_DIM, axis=-1)
    slopes = 2.0 ** (-8.0 * jnp.arange(1, b_heads + 1, dtype=jnp.float32) / b_heads)
    later_weights = (w_out, w_xq, w_xo, w_ff1, w_ff2)
    layer_weights = (w_in,) + later_weights
    kx, vx, (w_in_l,) = _mem_kv(mem, row(mem_g), w_xkv, (w_in,))
    for l in range(depth):
        (ya, q, k, v, yc), own_l = _mix_in(
            x, l, row(pre_mix_g), w_in_l, row(gate_ln_g), row(gate_ln_b), ws_cat, bs_x, w_dw,
            row(b_dw), row(conv_gn_g), row(conv_gn_b), later_weights if l == 0 else (),
            a_width=a_width, b_width=b_width, c_width=c_width)
        if own_l:
            w_out_l, w_xq_l, w_xo_l, w_ff1_l, w_ff2_l = own_l
        yb = _moba(q, k, v, slopes)
        x, next_l = _mix_out(x, ya, yb, yc, l, w_out_l, row(post_mix_g), row(pre_x_g), w_xq_l, kx, vx,
                             w_xo_l, row(post_x_g), layer_weights if l + 1 < depth else ())
        x = _ffn(x, l, row(pre_ffn_g), w_ff1_l, w_ff2_l, row(post_ffn_g))
        if next_l:
            w_in_l, w_out_l, w_xq_l, w_xo_l, w_ff1_l, w_ff2_l = next_l
    return x
```

```python
import functools
import math

import jax
import jax.numpy as jnp
from jax import lax
from jax.experimental import pallas as pl
from jax.experimental.pallas import tpu as pltpu

HEAD_DIM = 64
CHUNK = 128
MOBA_BLOCK = 256
MOBA_TOPK = 3
CONV_WIDTH = 31
X_HEADS = 4
EPS = 1e-6
LOG2_E = math.log2(math.e)

LANES = 128
SUBLANES = 8
HALO = 32
CONV_ROWS = 128
TOKEN_TILE = 512
ROW_GROUPS = 2
MIX_IN_TILE = 1024
BF16_SUBLANES = 16
VMEM_LIMIT = 56 * 1024 * 1024

NEG_INF = float("-inf")


def _rms(x, g):
    return x * lax.rsqrt(jnp.mean(x * x, axis=-1, keepdims=True) + EPS) * g


def _const_spec(shape, index_map):
    return pl.BlockSpec(shape, index_map, pipeline_mode=pl.Buffered(1))


def _cast_rider_specs(weights, layer, n_steps, step_of):
    in_specs, out_specs, out_shapes = [], [], []
    for w in weights:
        _, rows, cols = w.shape
        assert rows % (n_steps * BF16_SUBLANES) == 0
        chunk = (None, rows // n_steps, cols)
        in_specs.append(pl.BlockSpec(chunk, lambda *g: (layer, step_of(*g), 0)))
        out_specs.append(pl.BlockSpec(chunk, lambda *g: (0, step_of(*g), 0)))
        out_shapes.append(jax.ShapeDtypeStruct((1, rows, cols), jnp.bfloat16))
    return in_specs, out_specs, out_shapes


def _cast_riders(src_refs, dst_refs):
    for src_ref, dst_ref in zip(src_refs, dst_refs):
        dst_ref[...] = src_ref[...].astype(jnp.bfloat16)


def _mem_kv_kernel(mem_ref, g_ref, w_ref, *rest):
    n_cast = (len(rest) - 2) // 2
    k_ref, v_ref = rest[n_cast:n_cast + 2]
    _cast_riders(rest[:n_cast], rest[n_cast + 2:])
    d = mem_ref.shape[-1]
    m = _rms(mem_ref[...], g_ref[...]).astype(jnp.bfloat16)
    kv = jnp.dot(m, w_ref[...].astype(jnp.bfloat16), preferred_element_type=jnp.float32)
    k_ref[...] = kv[:, :d].astype(jnp.bfloat16)
    v_ref[...] = kv[:, d:].astype(jnp.bfloat16)


def _mem_kv(mem, mem_g, w_xkv, layer_weights):
    bn, m, d = mem.shape
    depth = w_xkv.shape[0]
    out = jax.ShapeDtypeStruct((depth, bn, m, d), jnp.bfloat16)
    cast_in, cast_out, cast_shape = _cast_rider_specs(
        layer_weights, 0, depth * bn, lambda l, b: l * bn + b)
    res = pl.pallas_call(
        _mem_kv_kernel,
        grid=(depth, bn),
        in_specs=[
            pl.BlockSpec((None, m, d), lambda l, b: (b, 0, 0)),
            pl.BlockSpec((None, 1, d), lambda l, b: (l, 0, 0)),
            pl.BlockSpec((None, d, 2 * d), lambda l, b: (l, 0, 0)),
        ] + cast_in,
        out_specs=[
            pl.BlockSpec((None, None, m, d), lambda l, b: (l, b, 0, 0)),
            pl.BlockSpec((None, None, m, d), lambda l, b: (l, b, 0, 0)),
        ] + cast_out,
        out_shape=[out, out] + cast_shape,
        compiler_params=pltpu.CompilerParams(
            dimension_semantics=("arbitrary", "arbitrary"), vmem_limit_bytes=VMEM_LIMIT),
        name="mem_kv",
    )(mem, mem_g, w_xkv, *layer_weights)
    return res[0], res[1], tuple(res[2:])


def _mix_in_kernel(x_ref, g_ref, win_ref, lng_ref, lnb_ref, ws_ref, bs_ref, wdw_ref, bdw_ref,
                   gng_ref, gnb_ref, *rest, a_width, b_width, c_width):
    n_cast = (len(rest) - 8) // 2
    ya_ref, q_ref, k_ref, v_ref, yc_ref = rest[n_cast:n_cast + 5]
    hist_ref, conv_ref, shift_ref = rest[-3:]
    _cast_riders(rest[:n_cast], rest[n_cast + 5:-3])
    tm = x_ref.shape[0]

    @pl.when(pl.program_id(1) == 0)
    def _():
        hist_ref[0:HALO, :] = jnp.zeros((HALO, c_width), jnp.float32)

    h = _rms(x_ref[...], g_ref[...]).astype(jnp.bfloat16)

    def project(lo, hi):
        return jnp.dot(h, win_ref[:, lo:hi], preferred_element_type=jnp.float32)

    o = 2 * a_width + 3 * b_width
    zc = project(o, o + 2 * c_width)
    hist_ref[HALO:HALO + tm, :] = zc[:, :c_width] * (1.0 / (1.0 + jnp.exp(-zc[:, c_width:])))

    o = 2 * a_width
    q = (project(o, o + b_width) * (HEAD_DIM ** -0.5 * LOG2_E)).astype(jnp.bfloat16)
    k = project(o + b_width, o + 2 * b_width).astype(jnp.bfloat16)
    v = project(o + 2 * b_width, o + 3 * b_width).astype(jnp.bfloat16)
    za = project(0, 2 * a_width)

    first = HALO - (CONV_WIDTH - 1)
    for c in range(tm // CONV_ROWS):
        r0 = c * CONV_ROWS
        acc = jnp.zeros((CONV_ROWS, c_width), jnp.float32)
        for phase in range(SUBLANES):
            taps = [j for j in range(CONV_WIDTH) if (first + j) % SUBLANES == phase]
            span = first + taps[-1] - phase
            rows = CONV_ROWS + span
            if phase:
                shift_ref[phase, 0:rows, :] = hist_ref[r0 + phase:r0 + phase + rows, :]
            for j in taps:
                off = first + j - phase
                window = (shift_ref[phase, off:off + CONV_ROWS, :] if phase else
                          hist_ref[r0 + off:r0 + off + CONV_ROWS, :])
                acc = acc + wdw_ref[j:j + 1, :] * window
        conv_ref[r0:r0 + CONV_ROWS, :] = acc + bdw_ref[...]
    hist_ref[0:HALO, :] = hist_ref[tm:tm + HALO, :]
    q_ref[...] = q
    k_ref[...] = k
    v_ref[...] = v

    ga = 0.5 * za * (1.0 + lax.erf(za * (2.0 ** -0.5)))
    u = ga[:, :a_width]
    vv = ga[:, a_width:]
    mu = jnp.mean(vv, axis=-1, keepdims=True)
    vc = vv - mu
    var = jnp.mean(vc * vc, axis=-1, keepdims=True)
    vn = (vc * lax.rsqrt(var + EPS) * lng_ref[...] + lnb_ref[...]).astype(jnp.bfloat16)
    n_heads = a_width // HEAD_DIM
    t_idx = lax.broadcasted_iota(jnp.int32, (CHUNK, n_heads * CHUNK), 0)
    s_idx = lax.broadcasted_iota(jnp.int32, (CHUNK, n_heads * CHUNK), 1) % CHUNK
    w_tril = jnp.where(s_idx <= t_idx, ws_ref[...], 0.0).astype(jnp.bfloat16)
    r_head = lax.broadcasted_iota(jnp.int32, (n_heads * CHUNK, a_width), 0) // CHUNK
    c_head = lax.broadcasted_iota(jnp.int32, (n_heads * CHUNK, a_width), 1) // HEAD_DIM
    head_match = r_head == c_head
    for c in range(tm // CHUNK):
        rows = slice(c * CHUNK, (c + 1) * CHUNK)
        v_chunk = vn[rows]
        v_bd = jnp.where(head_match, jnp.concatenate([v_chunk] * n_heads, axis=0), 0.0)
        mixed = jnp.dot(w_tril, v_bd.astype(jnp.bfloat16), preferred_element_type=jnp.float32)
        ya_ref[rows, :] = (u[rows] * (mixed + bs_ref[...])).astype(jnp.bfloat16)

    gr = lax.broadcasted_iota(jnp.int32, (2 * c_width, c_width), 0) % c_width // HEAD_DIM
    gc = lax.broadcasted_iota(jnp.int32, (2 * c_width, c_width), 1) // HEAD_DIM
    gmat2 = jnp.where(gr == gc, 1.0 / HEAD_DIM, 0.0).astype(jnp.bfloat16)

    def group_mean(a):
        hi = a.astype(jnp.bfloat16)
        lo = (a - hi.astype(jnp.float32)).astype(jnp.bfloat16)
        return jnp.dot(jnp.concatenate([hi, lo], axis=-1), gmat2, preferred_element_type=jnp.float32)

    y = conv_ref[...]
    yd = y - group_mean(y)
    yn = yd * lax.rsqrt(group_mean(yd * yd) + EPS) * gng_ref[...] + gnb_ref[...]
    yc_ref[...] = (yn * (1.0 / (1.0 + jnp.exp(-yn)))).astype(jnp.bfloat16)


def _mix_in(x, l, pre_g, w_in, ln_g, ln_b, ws_cat, bs_x, w_dw, b_dw, gn_g, gn_b, own_weights,
            *, a_width, b_width, c_width):
    bn, s, d = x.shape
    tm = min(MIX_IN_TILE, s)
    n_i = s // tm
    in_cols = w_in.shape[-1]
    n_heads = a_width // HEAD_DIM
    lsel3 = lambda b, i: (l, 0, 0)
    tok = lambda w: pl.BlockSpec((None, tm, w), lambda b, i: (b, i, 0))
    bf = lambda w: jax.ShapeDtypeStruct((bn, s, w), jnp.bfloat16)
    cast_in, cast_out, cast_shape = _cast_rider_specs(
        own_weights, l, bn * n_i, lambda b, i: b * n_i + i)
    res = pl.pallas_call(
        functools.partial(_mix_in_kernel, a_width=a_width, b_width=b_width, c_width=c_width),
        grid=(bn, s // tm),
        in_specs=[
            tok(d),
            pl.BlockSpec((None, 1, d), lsel3),
            _const_spec((None, d, in_cols), lambda b, i: (0, 0, 0)),
            pl.BlockSpec((None, 1, a_width), lsel3),
            pl.BlockSpec((None, 1, a_width), lsel3),
            pl.BlockSpec((None, CHUNK, n_heads * CHUNK), lsel3),
            pl.BlockSpec((None, CHUNK, a_width), lsel3),
            pl.BlockSpec((None, CONV_WIDTH, c_width), lsel3),
            pl.BlockSpec((None, 1, c_width), lsel3),
            pl.BlockSpec((None, 1, c_width), lsel3),
            pl.BlockSpec((None, 1, c_width), lsel3),
        ] + cast_in,
        out_specs=[tok(a_width), tok(b_width), tok(b_width), tok(b_width), tok(c_width)] + cast_out,
        out_shape=[bf(a_width), bf(b_width), bf(b_width), bf(b_width), bf(c_width)] + cast_shape,
        scratch_shapes=[pltpu.VMEM((tm + HALO, c_width), jnp.float32),
                        pltpu.VMEM((tm, c_width), jnp.float32),
                        pltpu.VMEM((SUBLANES, CONV_ROWS + HALO, c_width), jnp.float32)],
        compiler_params=pltpu.CompilerParams(
            dimension_semantics=("arbitrary", "arbitrary"), vmem_limit_bytes=VMEM_LIMIT),
        name="mix_in",
    )(x, pre_g, w_in, ln_g, ln_b, ws_cat, bs_x, w_dw, b_dw, gn_g, gn_b, *own_weights)
    return tuple(res[:5]), tuple(res[5:])


ALIBI_TERMS = 4
ALIBI_PIECES = 3
ALIBI_ROWS = 16
HEADS_PER_TILE = LANES // HEAD_DIM
PV_ROWS = HEAD_DIM + BF16_SUBLANES
GATE_PIECES = 3
SELECT_TILES = 1


def _own_lanes(h, idx):
    hh = h % HEADS_PER_TILE
    return (idx >= hh * HEAD_DIM) & (idx < (hh + 1) * HEAD_DIM)


def _slot_base(h):
    return ((h + 1) % HEADS_PER_TILE) * HEAD_DIM


def _moba_kernel(slopes_ref, q_ref, k_ref, v_ref, o_ref, km_ref, kaug_ref, vt_ref, qaug_ref, neg_ref,
                 s_ref, p_ref, acc_ref, *, n_blocks):
    blk = MOBA_BLOCK
    n_tiles = q_ref.shape[-1] // LANES

    def prepare():
        lane = lax.broadcasted_iota(jnp.int32, (blk, LANES), 1)
        j_pos = lax.broadcasted_iota(jnp.int32, (blk, LANES), 0).astype(jnp.float32)
        ones_rows = (lax.broadcasted_iota(jnp.int32, (BF16_SUBLANES, blk), 0) == 0
                     ).astype(jnp.float32)
        km_row = lax.broadcasted_iota(jnp.int32, (n_blocks, LANES), 0)
        km_lane = lax.broadcasted_iota(jnp.int32, (n_blocks, LANES), 1)
        sum_rows = jnp.ones((BF16_SUBLANES, blk), jnp.bfloat16)
        for t in range(n_tiles):
            cols = slice(t * LANES, (t + 1) * LANES)
            kmean = jnp.zeros((n_blocks, LANES), jnp.float32)
            for n in range(n_blocks):
                rows = slice(n * blk, (n + 1) * blk)
                kb = k_ref[rows, cols]
                k_sum = jnp.dot(sum_rows, kb, preferred_element_type=jnp.float32)[:1]
                kmean = jnp.where(km_row == n, k_sum * (1.0 / blk), kmean)
                v_t = v_ref[rows, cols].astype(jnp.float32).T
                for hh in range(HEADS_PER_TILE):
                    h = t * HEADS_PER_TILE + hh
                    term = (lane - _slot_base(h)) // ALIBI_PIECES
                    aug = jnp.where(term == 0, j_pos,
                                    jnp.where(term == 1, float(n),
                                              jnp.where((term == 2) | (term == 3), 1.0, 0.0)))
                    kaug_ref[h, n] = jnp.where(_own_lanes(h, lane), kb, aug.astype(jnp.bfloat16))
                    vt_ref[h, n] = jnp.concatenate(
                        [v_t[hh * HEAD_DIM:(hh + 1) * HEAD_DIM], ones_rows], axis=0).astype(jnp.bfloat16)
            for hh in range(HEADS_PER_TILE):
                h = t * HEADS_PER_TILE + hh
                rest = jnp.where(_own_lanes(h, km_lane), kmean, 0.0)
                pieces = []
                for _ in range(GATE_PIECES):
                    piece = rest.astype(jnp.bfloat16).astype(jnp.float32)
                    pieces.append(piece)
                    rest = rest - piece
                km_ref[h] = jnp.concatenate(pieces[::-1], axis=0).astype(jnp.bfloat16)

    prepare()

    def select(g):
        _moba_select(g, slopes_ref, q_ref, km_ref, qaug_ref, neg_ref, n_blocks=n_blocks)

    assert SELECT_TILES == 1
    select(0)
    select(1)
    _moba_open(0, kaug_ref, qaug_ref, s_ref)

    def sweep(qi, carry):
        _moba_tile(qi, o_ref, kaug_ref, vt_ref, qaug_ref, neg_ref, s_ref, p_ref, acc_ref,
                   n_blocks=n_blocks)
        _moba_open(jnp.minimum(qi + 1, n_blocks - 1), kaug_ref, qaug_ref, s_ref)
        select(jnp.minimum(qi + 2, n_blocks - 1))
        return carry

    lax.fori_loop(0, n_blocks, sweep, 0)


def _moba_select(g, slopes_ref, q_ref, km_ref, qaug_ref, neg_ref, *, n_blocks):
    blk = MOBA_BLOCK
    width = SELECT_TILES * blk
    n_tiles = q_ref.shape[-1] // LANES
    n_heads = n_tiles * HEADS_PER_TILE
    hg = pl.program_id(1)
    first_tile = jnp.asarray(g, jnp.int32) * SELECT_TILES
    q_rows = pl.ds(pl.multiple_of(first_tile * blk, width), width)

    def tile_of(shape):
        return first_tile + lax.broadcasted_iota(jnp.int32, shape, 1) // blk

    slot = lax.broadcasted_iota(jnp.int32, (ALIBI_ROWS, width), 0)
    i_pos = (lax.broadcasted_iota(jnp.int32, (ALIBI_ROWS, width), 1) % blk).astype(jnp.float32)
    tile_f = tile_of((ALIBI_ROWS, width)).astype(jnp.float32)
    pad = jnp.zeros((HEAD_DIM - ALIBI_ROWS, width), jnp.float32)
    blk_id = lax.broadcasted_iota(jnp.int32, (n_blocks, width), 0)
    past = blk_id < tile_of((n_blocks, width))
    for t in range(n_tiles):
        q_t = q_ref[q_rows, t * LANES:(t + 1) * LANES].astype(jnp.float32).T
        for hh in range(HEADS_PER_TILE):
            h = t * HEADS_PER_TILE + hh
            c = slopes_ref[hg * n_heads + h] * LOG2_E
            term = slot // ALIBI_PIECES
            full = jnp.where(term == 0, c,
                             jnp.where(term == 1, c * blk,
                                       jnp.where(term == 2, -(c * (blk * tile_f)),
                                                 jnp.where(term == 3, -(c * i_pos), 0.0))))
            piece = full.astype(jnp.bfloat16).astype(jnp.float32)
            aug = piece
            for r in range(1, ALIBI_PIECES):
                full = full - piece
                piece = full.astype(jnp.bfloat16).astype(jnp.float32)
                aug = jnp.where(slot % ALIBI_PIECES == r, piece, aug)
            own = q_t[hh * HEAD_DIM:(hh + 1) * HEAD_DIM]
            rows = [own, aug, pad] if _slot_base(h) > 0 else [aug, pad, own]
            q_aug = jnp.concatenate(rows, axis=0).astype(jnp.bfloat16)
            for j in range(SELECT_TILES):
                qaug_ref[h, first_tile + j] = q_aug[:, j * blk:(j + 1) * blk]

            g3 = jnp.dot(km_ref[h], q_aug, preferred_element_type=jnp.float32)
            gate = g3[:n_blocks] + g3[n_blocks:2 * n_blocks] + g3[2 * n_blocks:]
            gate = jnp.where(past, gate, NEG_INF)
            blk_f = blk_id.astype(jnp.float32)
            chosen = jnp.zeros((n_blocks, width), jnp.float32)
            for _ in range(MOBA_TOPK):
                best = jnp.max(gate, axis=0, keepdims=True)
                pick = jnp.min(jnp.where(gate == best, blk_f, float(n_blocks)), axis=0, keepdims=True)
                taken = blk_f == pick
                chosen = jnp.where(taken, 1.0, chosen)
                gate = jnp.where(taken, NEG_INF, gate)
            neg = jnp.where((chosen > 0.5) & past, 0.0, NEG_INF)
            for j in range(SELECT_TILES):
                neg_ref[h, first_tile + j, 0] = jnp.zeros((1, blk), jnp.float32)
                for u in range(1, n_blocks):
                    neg_ref[h, first_tile + j, u] = neg[u - 1:u, j * blk:(j + 1) * blk]


def _moba_open(qi, kaug_ref, qaug_ref, s_ref):
    blk = MOBA_BLOCK
    key_i = lax.broadcasted_iota(jnp.int32, (blk, blk), 0)
    qry_i = lax.broadcasted_iota(jnp.int32, (blk, blk), 1)
    causal = key_i <= qry_i
    for h in range(kaug_ref.shape[0]):
        s = jnp.dot(kaug_ref[h, qi], qaug_ref[h, qi], preferred_element_type=jnp.float32)
        s_ref[0, h] = jnp.where(causal, s, NEG_INF)


def _moba_tile(qi, o_ref, kaug_ref, vt_ref, qaug_ref, neg_ref, s_ref, p_ref, acc_ref, *, n_blocks):
    blk = MOBA_BLOCK
    n_heads = kaug_ref.shape[0]
    n_tiles = n_heads // HEADS_PER_TILE
    q_rows = pl.ds(pl.multiple_of(qi * blk, blk), blk)
    q_aug = [qaug_ref[h, qi] for h in range(n_heads)]
    for h in range(n_heads):
        p_ref[1, h] = jnp.zeros((blk, blk), jnp.bfloat16)
        acc_ref[h] = jnp.zeros((PV_ROWS, blk), jnp.float32)

    def block_of(u):
        return jnp.where(u == 0, qi, jnp.minimum(u - 1, n_blocks - 1))

    def step(u, cur, m_runs):
        b_prev = block_of(jnp.maximum(u - 1, 0))
        k_next = jnp.minimum(u, n_blocks - 1)
        pv = [jnp.dot(vt_ref[h, b_prev], p_ref[1 - cur, h], preferred_element_type=jnp.float32)
              for h in range(n_heads)]
        for h in range(n_heads):
            s_ref[1 - cur, h] = jnp.dot(kaug_ref[h, k_next], q_aug[h],
                                        preferred_element_type=jnp.float32)
        m_news = []
        for h in range(n_heads):
            s = s_ref[cur, h]
            neg = neg_ref[h, qi, u]
            m_new = jnp.maximum(m_runs[h], jnp.max(s, axis=0, keepdims=True) + neg)
            alpha = jnp.exp2(m_runs[h] - m_new)
            p = jnp.exp2(s - (m_new - neg))
            p_ref[cur, h] = p.astype(jnp.bfloat16)
            acc_ref[h] = alpha * (acc_ref[h] + pv[h])
            m_news.append(m_new)
        return tuple(m_news)

    def step_pair(i, m_runs):
        return step(2 * i + 1, 1, step(2 * i, 0, m_runs))

    m_init = jnp.full((1, blk), NEG_INF, jnp.float32)
    lax.fori_loop(0, (qi + 2) // 2, step_pair, (m_init,) * n_heads)
    last = 2 * ((qi + 2) // 2) - 1
    for t in range(n_tiles):
        outs = []
        for h in range(t * HEADS_PER_TILE, (t + 1) * HEADS_PER_TILE):
            acc = acc_ref[h] + jnp.dot(vt_ref[h, block_of(last)], p_ref[1, h],
                                       preferred_element_type=jnp.float32)
            outs.append(acc[:HEAD_DIM] / acc[HEAD_DIM:HEAD_DIM + 1])
        o_ref[q_rows, t * LANES:(t + 1) * LANES] = jnp.concatenate(outs, axis=0).T.astype(jnp.bfloat16)


MOBA_TILES_PER_STEP = 2


def _moba(q, k, v, slopes):
    bn, s, width = q.shape
    step_w = MOBA_TILES_PER_STEP * LANES
    assert s % MOBA_BLOCK == 0 and width % step_w == 0
    n_blocks = s // MOBA_BLOCK
    n_heads = MOBA_TILES_PER_STEP * HEADS_PER_TILE
    assert ALIBI_TERMS * ALIBI_PIECES <= ALIBI_ROWS <= HEAD_DIM and HEADS_PER_TILE == 2
    assert n_blocks <= 256 and MOBA_BLOCK <= 256
    seq = pl.BlockSpec((None, s, step_w), lambda b, hg: (b, 0, hg))
    return pl.pallas_call(
        functools.partial(_moba_kernel, n_blocks=n_blocks),
        grid=(bn, width // step_w),
        in_specs=[pl.BlockSpec(memory_space=pltpu.SMEM), seq, seq, seq],
        out_specs=seq,
        out_shape=jax.ShapeDtypeStruct((bn, s, width), jnp.bfloat16),
        scratch_shapes=[
            pltpu.VMEM((n_heads, GATE_PIECES * n_blocks, LANES), jnp.bfloat16),
            pltpu.VMEM((n_heads, n_blocks, MOBA_BLOCK, LANES), jnp.bfloat16),
            pltpu.VMEM((n_heads, n_blocks, PV_ROWS, MOBA_BLOCK), jnp.bfloat16),
            pltpu.VMEM((n_heads, n_blocks, LANES, MOBA_BLOCK), jnp.bfloat16),
            pltpu.VMEM((n_heads, n_blocks, n_blocks, 1, MOBA_BLOCK), jnp.float32),
            pltpu.VMEM((2, n_heads, MOBA_BLOCK, MOBA_BLOCK), jnp.float32),
            pltpu.VMEM((2, n_heads, MOBA_BLOCK, MOBA_BLOCK), jnp.bfloat16),
            pltpu.VMEM((n_heads, PV_ROWS, MOBA_BLOCK), jnp.float32),
        ],
        compiler_params=pltpu.CompilerParams(
            dimension_semantics=("arbitrary", "arbitrary"),
            vmem_limit_bytes=VMEM_LIMIT),
        name="moba",
    )(slopes, q, k, v)


def _mix_out_kernel(x_ref, ya_ref, yb_ref, yc_ref, wout_ref, gpost_ref, gprex_ref, wq_ref,
                    kx_ref, vx_ref, wo_ref, gpostx_ref, *rest):
    n_cast = len(rest) // 2
    o_ref = rest[n_cast]
    _cast_riders(rest[:n_cast], rest[n_cast + 1:])

    d = x_ref.shape[-1]
    dh = d // X_HEADS
    nt = (((1,), (1,)), ((), ()))
    rows = x_ref.shape[0] // ROW_GROUPS
    groups = [slice(r * rows, (r + 1) * rows) for r in range(ROW_GROUPS)]

    ys = [jnp.concatenate([ya_ref[g, :], yb_ref[g, :], yc_ref[g, :]], axis=-1) for g in groups]
    mixes = [jnp.dot(y, wout_ref[...], preferred_element_type=jnp.float32) for y in ys]
    x1s = [x_ref[g, :] + _rms(mix, gpost_ref[...]) for g, mix in zip(groups, mixes)]
    hs = [_rms(x1, gprex_ref[...]).astype(jnp.bfloat16) for x1 in x1s]
    qxs = [(jnp.dot(h, wq_ref[...], preferred_element_type=jnp.float32) * (dh ** -0.5)
            ).astype(jnp.bfloat16) for h in hs]
    outs = []
    for qx in qxs:
        heads = []
        for hh in range(X_HEADS):
            cols = slice(hh * dh, (hh + 1) * dh)
            s = lax.dot_general(qx[:, cols], kx_ref[:, cols], nt, preferred_element_type=jnp.float32)
            p = jnp.exp(s - jnp.max(s, axis=-1, keepdims=True))
            denom = jnp.sum(p, axis=-1, keepdims=True)
            oh = jnp.dot(p.astype(jnp.bfloat16), vx_ref[:, cols], preferred_element_type=jnp.float32)
            heads.append((oh / denom).astype(jnp.bfloat16))
        outs.append(jnp.concatenate(heads, axis=-1))
    atts = [jnp.dot(o, wo_ref[...], preferred_element_type=jnp.float32) for o in outs]
    for g, x1, att in zip(groups, x1s, atts):
        o_ref[g, :] = x1 + _rms(att, gpostx_ref[...])


def _mix_out(x, ya, yb, yc, l, w_out, g_post, g_prex, w_xq, kx, vx, w_xo, g_postx, next_weights):
    bn, s, d = x.shape
    tm = min(ROW_GROUPS * TOKEN_TILE, s)
    m = kx.shape[2]
    n_i = s // tm
    lsel3 = lambda b, i: (l, 0, 0)
    first3 = lambda b, i: (0, 0, 0)
    tok = lambda w: pl.BlockSpec((None, tm, w), lambda b, i: (b, i, 0))
    gain = pl.BlockSpec((None, 1, d), lsel3)
    wsq = _const_spec((None, d, d), first3)
    mem = pl.BlockSpec((None, None, m, d), lambda b, i: (l, b, 0, 0))
    cast_in, cast_out, cast_shape = _cast_rider_specs(
        next_weights, l + 1, bn * n_i, lambda b, i: b * n_i + i)
    out = pl.pallas_call(
        _mix_out_kernel,
        grid=(bn, n_i),
        in_specs=[tok(d), tok(ya.shape[-1]), tok(yb.shape[-1]), tok(yc.shape[-1]),
                  wsq, gain, gain, wsq, mem, mem, wsq, gain] + cast_in,
        out_specs=[tok(d)] + cast_out,
        out_shape=[jax.ShapeDtypeStruct((bn, s, d), jnp.float32)] + cast_shape,
        compiler_params=pltpu.CompilerParams(
            dimension_semantics=("arbitrary", "arbitrary"), vmem_limit_bytes=VMEM_LIMIT),
        name="mix_out",
    )(x, ya, yb, yc, w_out, g_post, g_prex, w_xq, kx, vx, w_xo, g_postx, *next_weights)
    return out[0], tuple(out[1:])


def _ffn_kernel(x_ref, gpre_ref, w1_ref, w2_ref, gpost_ref, o_ref):
    rows = x_ref.shape[0] // ROW_GROUPS
    half = w1_ref.shape[-1] // 2
    groups = [slice(r * rows, (r + 1) * rows) for r in range(ROW_GROUPS)]
    xs = [x_ref[g, :] for g in groups]
    hs = [_rms(x, gpre_ref[...]).astype(jnp.bfloat16) for x in xs]
    fs = [None] * ROW_GROUPS
    for c in range(2):
        cols = slice(c * half, (c + 1) * half)
        acts = [jnp.maximum(jnp.dot(h, w1_ref[:, cols], preferred_element_type=jnp.float32), 0.0)
                for h in hs]
        acts = [(a * a).astype(jnp.bfloat16) for a in acts]
        for r, a in enumerate(acts):
            part = jnp.dot(a, w2_ref[cols, :], preferred_element_type=jnp.float32)
            fs[r] = part if fs[r] is None else fs[r] + part
    for g, x, f in zip(groups, xs, fs):
        o_ref[g, :] = x + _rms(f, gpost_ref[...])


def _ffn(x, l, g_pre, w1, w2, g_post):
    bn, s, d = x.shape
    tm = min(ROW_GROUPS * TOKEN_TILE, s)
    dff = w1.shape[-1]
    lsel3 = lambda b, i: (l, 0, 0)
    first3 = lambda b, i: (0, 0, 0)
    tok = pl.BlockSpec((None, tm, d), lambda b, i: (b, i, 0))
    gain = pl.BlockSpec((None, 1, d), lsel3)
    return pl.pallas_call(
        _ffn_kernel,
        grid=(bn, s // tm),
        in_specs=[tok, gain, _const_spec((None, d, dff), first3),
                  _const_spec((None, dff, d), first3), gain],
        out_specs=tok,
        out_shape=jax.ShapeDtypeStruct((bn, s, d), jnp.float32),
        compiler_params=pltpu.CompilerParams(
            dimension_semantics=("arbitrary", "arbitrary"), vmem_limit_bytes=VMEM_LIMIT),
        name="ffn",
    )(x, g_pre, w1, w2, g_post)


def kernel(x, mem, pre_mix_g, w_in, gate_ln_g, gate_ln_b, w_s, b_s, w_dw, b_dw, conv_gn_g, conv_gn_b, w_out, post_mix_g, pre_x_g, mem_g, w_xq, w_xkv, w_xo, post_x_g, pre_ffn_g, w_ff1, w_ff2, post_ffn_g):
    depth, d, _ = w_in.shape
    a_width = gate_ln_g.shape[-1]
    c_width = w_dw.shape[-1]
    b_width = d - a_width - c_width
    a_heads = a_width // HEAD_DIM
    b_heads = b_width // HEAD_DIM
    assert w_in.shape[-1] == 2 * a_width + 3 * b_width + 2 * c_width
    assert w_s.shape[1:] == (a_heads, CHUNK, CHUNK) and w_dw.shape[1] == CONV_WIDTH
    assert x.shape[1] % MOBA_BLOCK == 0 and x.shape[1] % TOKEN_TILE == 0

    row = lambda g: g[:, None, :]
    ws_cat = w_s.transpose(0, 2, 1, 3).reshape(depth, CHUNK, a_heads * CHUNK)
    bs_x = jnp.repeat(b_s.transpose(0, 2, 1), HEAD_DIM, axis=-1)
    slopes = 2.0 ** (-8.0 * jnp.arange(1, b_heads + 1, dtype=jnp.float32) / b_heads)
    later_weights = (w_out, w_xq, w_xo, w_ff1, w_ff2)
    layer_weights = (w_in,) + later_weights
    kx, vx, (w_in_l,) = _mem_kv(mem, row(mem_g), w_xkv, (w_in,))
    for l in range(depth):
        (ya, q, k, v, yc), own_l = _mix_in(
            x, l, row(pre_mix_g), w_in_l, row(gate_ln_g), row(gate_ln_b), ws_cat, bs_x, w_dw,
            row(b_dw), row(conv_gn_g), row(conv_gn_b), later_weights if l == 0 else (),
            a_width=a_width, b_width=b_width, c_width=c_width)
        if own_l:
            w_out_l, w_xq_l, w_xo_l, w_ff1_l, w_ff2_l = own_l
        yb = _moba(q, k, v, slopes)
        x, next_l = _mix_out(x, ya, yb, yc, l, w_out_l, row(post_mix_g), row(pre_x_g), w_xq_l, kx, vx,
                             w_xo_l, row(post_x_g), layer_weights if l + 1 < depth else ())
        x = _ffn(x, l, row(pre_ffn_g), w_ff1_l, w_ff2_l, row(post_ffn_g))
        if next_l:
            w_in_l, w_out_l, w_xq_l, w_xo_l, w_ff1_l, w_ff2_l = next_l
    return x
```
